```python
import math
import jax, jax.numpy as jnp
from jax import lax
import numpy as np

D_MODEL = 4096
BATCH = 4
SEQ = 2048
DEPTH = 2
DEC_BATCH = 8
DEC_SEQ = 8
PAST_LEN = 16384
PAGE_SIZE = 128

N_EVEN = (DEPTH + 1) // 2
N_ODD = DEPTH // 2
HALF = D_MODEL // 2
A_HEADS = 4
A_DV = HALF // A_HEADS
A_DK = A_DV // 2
B_HEADS = 4
B_DV = HALF // B_HEADS
B_DK = B_DV // 2
B_RANK = 16
B_GATE_NORMALIZER = 16.0
C_DK = 128
C_HEADS = HALF // C_DK
C_DV = HALF // C_HEADS
D_HEAD_DIM = 128
D_HEADS = HALF // D_HEAD_DIM
D_FF = ((8 * D_MODEL // 3 + 255) // 256) * 256
CHUNK_MLSTM = 64
CHUNK_LIN = 32
Q_BLOCK = 128
EPS = 1e-6
EVEN_SIZES = (A_HEADS * A_DK, A_HEADS * A_DK, A_HEADS * A_DV, A_HEADS * A_DV, A_HEADS, A_HEADS,
              B_HEADS * B_DK, B_HEADS * B_DK, B_HEADS * B_DV, B_HEADS * B_DV, B_RANK)
ODD_SIZES = (C_HEADS * C_DK, C_HEADS * C_DK, C_HEADS * C_DV, C_HEADS * C_DV,
             D_HEADS * D_HEAD_DIM, D_HEADS * D_HEAD_DIM, D_HEADS * D_HEAD_DIM, D_HEADS * D_HEAD_DIM, D_HEADS)
EVEN_IN = sum(EVEN_SIZES)
ODD_IN = sum(ODD_SIZES)

kernel_name = 'hybrid_mlstm_gla_hgrn2_fox_step'

F32 = jnp.float32


def split_cols(y, sizes):
    cuts = [int(c) for c in np.cumsum(sizes)[:-1]]
    return jnp.split(y, cuts, axis=-1)


def rmsnorm(x, g):
    xf = x.astype(F32)
    y = xf * lax.rsqrt(jnp.mean(xf * xf, axis=-1, keepdims=True) + EPS) * g.astype(F32)
    return y.astype(x.dtype)


def head_rmsnorm(x, g):
    return x * lax.rsqrt(jnp.mean(x * x, axis=-1, keepdims=True) + EPS) * g.astype(F32)[None, :, None, :]


def to_heads(a, n_heads):
    b, t, _ = a.shape
    return a.reshape(b, t, n_heads, -1).transpose(0, 2, 1, 3).astype(F32)


def from_heads(a):
    b, h, t, d = a.shape
    return a.transpose(0, 2, 1, 3).reshape(b, t, h * d)


def to_chunks(a, n_chunks, chunk):
    a = a.reshape(a.shape[:2] + (n_chunks, chunk) + a.shape[3:])
    return jnp.moveaxis(a, 2, 0)


def from_chunks(a):
    a = jnp.moveaxis(a, 0, 2)
    return a.reshape(a.shape[:2] + (-1,) + a.shape[4:])


def gather_pages(cache, page_table):
    g = cache[page_table]
    return g.reshape((g.shape[0], -1) + g.shape[3:])


def gated_linear_scan(q, k, v, g, s0, chunk):
    t = q.shape[2]
    L = math.gcd(t, chunk)
    n_ch = t // L
    mask = jnp.tril(jnp.ones((L, L), dtype=bool))

    def step(s, inp):
        qc, kc, vc, gc = inp
        b = jnp.cumsum(gc, axis=-2)
        b_last = b[..., -1:, :]
        q_dec = qc * jnp.exp(b)
        scores = jnp.einsum('bhti,bhsi->bhts', q_dec, kc * jnp.exp(-b))
        scores = jnp.where(mask, scores, 0.0)
        o = jnp.einsum('bhti,bhij->bhtj', q_dec, s) + jnp.einsum('bhts,bhsj->bhtj', scores, vc)
        s_new = jnp.exp(b_last)[..., 0, :, None] * s + jnp.einsum('bhsi,bhsj->bhij', kc * jnp.exp(b_last - b), vc)
        return s_new, o

    inputs = tuple(to_chunks(a, n_ch, L) for a in (q, k, v, g))
    s_fin, o = lax.scan(step, s0.astype(F32), inputs)
    return from_chunks(o), s_fin


def mlstm_scan(q, k, v, i_pre, logf, c0, n0, m0, chunk):
    t = q.shape[2]
    L = math.gcd(t, chunk)
    n_ch = t // L
    mask = jnp.tril(jnp.ones((L, L), dtype=bool))

    def step(carry, inp):
        c, nvec, m = carry
        qc, kc, vc, ic, fc = inp
        b = jnp.cumsum(fc, axis=-1)
        dmat = jnp.where(mask, b[..., :, None] - b[..., None, :] + ic[..., None, :], -jnp.inf)
        from_state = b + m[..., None]
        m_t = jnp.maximum(from_state, jnp.max(dmat, axis=-1))
        w = jnp.exp(dmat - m_t[..., None]) * jnp.einsum('bhti,bhsi->bhts', qc, kc)
        s_state = jnp.exp(from_state - m_t)
        num = s_state[..., None] * jnp.einsum('bhti,bhij->bhtj', qc, c) + jnp.einsum('bhts,bhsj->bhtj', w, vc)
        den = s_state * jnp.einsum('bhti,bhi->bht', qc, nvec) + jnp.sum(w, axis=-1)
        h = num / jnp.maximum(jnp.abs(den), jnp.exp(-m_t))[..., None]
        m_new = m_t[..., -1]
        w_end = jnp.exp(b[..., -1:] - b + ic - m_new[..., None])[..., None]
        s_end = jnp.exp(b[..., -1] + m - m_new)
        c_new = s_end[..., None, None] * c + jnp.einsum('bhsi,bhsj->bhij', kc * w_end, vc)
        n_new = s_end[..., None] * nvec + jnp.sum(kc * w_end, axis=-2)
        return (c_new, n_new, m_new), h

    carry0 = (c0.astype(F32), n0.astype(F32), m0.astype(F32))
    inputs = tuple(to_chunks(a, n_ch, L) for a in (q, k, v, i_pre, logf))
    (c1, n1, m1), h = lax.scan(step, carry0, inputs)
    return from_chunks(h), c1, n1, m1


def fox_attention(q, k_new, v_new, logf_new, k_past, v_past, logf_past):
    bsz, t, n_heads, dh = q.shape
    n_past = k_past.shape[1]
    scale = dh ** -0.5
    c_new = jnp.cumsum(logf_new, axis=1).transpose(0, 2, 1)
    kn = k_new.astype(F32)
    vn = v_new.astype(F32)
    if n_past > 0:
        lp = logf_past.astype(F32)
        d_past = (lax.cumsum(lp, axis=1, reverse=True) - lp).transpose(0, 2, 1)
        kp = k_past.astype(F32)
        vp = v_past.astype(F32)
    qb_len = math.gcd(t, Q_BLOCK)
    n_blocks = t // qb_len
    key_idx = jnp.arange(t)

    def block(args):
        qb, cb, idx = args
        q_idx = idx * qb_len + jnp.arange(qb_len)
        qf = qb.astype(F32) * scale
        s_new = jnp.einsum('bqhd,bkhd->bhqk', qf, kn) + cb[..., None] - c_new[:, :, None, :]
        s_new = jnp.where(key_idx[None, :] <= q_idx[:, None], s_new, -jnp.inf)
        if n_past > 0:
            s_past = jnp.einsum('bqhd,bkhd->bhqk', qf, kp) + cb[..., None] + d_past[:, :, None, :]
            p = jax.nn.softmax(jnp.concatenate([s_past, s_new], axis=-1), axis=-1)
            return (jnp.einsum('bhqk,bkhd->bqhd', p[..., :n_past], vp)
                    + jnp.einsum('bhqk,bkhd->bqhd', p[..., n_past:], vn))
        p = jax.nn.softmax(s_new, axis=-1)
        return jnp.einsum('bhqk,bkhd->bqhd', p, vn)

    qs = jnp.moveaxis(q.reshape(bsz, n_blocks, qb_len, n_heads, dh), 1, 0)
    cs = jnp.moveaxis(c_new.reshape(bsz, n_heads, n_blocks, qb_len), 2, 0)
    out = lax.map(block, (qs, cs, jnp.arange(n_blocks)))
    return jnp.moveaxis(out, 0, 1).reshape(bsz, t, n_heads * dh)


def hgrn_lower_bound(logits, layer):
    cum = jnp.cumsum(jax.nn.softmax(logits.astype(F32), axis=0), axis=0)
    return cum[layer] - cum[0]


def even_mixer(h, p, e, c0, n0, m0, s0):
    aq, ak, av, ao, ai, af, bq, bk, bv, bg, bz = split_cols(h @ p['w_in_even'][e], EVEN_SIZES)
    i_pre = (ai + p['mlstm_b_i'][e]).astype(F32).transpose(0, 2, 1)
    logf = jax.nn.log_sigmoid((af + p['mlstm_b_f'][e]).astype(F32)).transpose(0, 2, 1)
    ha, c1, n1, m1 = mlstm_scan(to_heads(aq, A_HEADS), to_heads(ak, A_HEADS) * A_DK ** -0.5,
                                to_heads(av, A_HEADS), i_pre, logf, c0, n0, m0, CHUNK_MLSTM)
    ha = from_heads(head_rmsnorm(ha, p['mlstm_norm'][e])) * jax.nn.sigmoid(ao.astype(F32))
    gk = jax.nn.log_sigmoid((bz @ p['gla_w_gk2'][e] + p['gla_b_gk'][e]).astype(F32)) / B_GATE_NORMALIZER
    hb, s1 = gated_linear_scan(to_heads(bq, B_HEADS) * B_DK ** -0.5, to_heads(bk, B_HEADS),
                               to_heads(bv, B_HEADS), to_heads(gk, B_HEADS), s0, CHUNK_LIN)
    hb = from_heads(head_rmsnorm(hb, p['gla_norm'][e])) * jax.nn.silu(bg.astype(F32))
    out = jnp.concatenate([ha, hb], axis=-1).astype(h.dtype) @ p['w_out_even'][e]
    return out, c1, n1, m1, s1


def odd_mixer(h, p, o, layer, s0, fox_past):
    bsz, t, _ = h.shape
    cq, cf, ci, cg, dq, dk, dv, dg, df = split_cols(h @ p['w_in_odd'][o], ODD_SIZES)
    lb = hgrn_lower_bound(p['hgrn_lb_logits'], layer)
    z = cf.astype(F32)
    log_forget = jnp.log(lb + (1.0 - lb) * jax.nn.sigmoid(z))
    key = (1.0 - lb) * jax.nn.sigmoid(-z)
    hc, s1 = gated_linear_scan(to_heads(jax.nn.silu(cq), C_HEADS), to_heads(key, C_HEADS),
                               to_heads(ci, C_HEADS), to_heads(log_forget, C_HEADS), s0, CHUNK_LIN)
    hc = from_heads(head_rmsnorm(hc, p['hgrn_norm'][o])) * jax.nn.silu(cg.astype(F32))
    q = dq.reshape(bsz, t, D_HEADS, D_HEAD_DIM)
    k = dk.reshape(bsz, t, D_HEADS, D_HEAD_DIM)
    v = dv.reshape(bsz, t, D_HEADS, D_HEAD_DIM)
    logf = jax.nn.log_sigmoid((df + p['fox_b_f'][o]).astype(F32))
    k_past, v_past, logf_past = fox_past
    hd = fox_attention(q, k, v, logf, k_past, v_past, logf_past)
    hd = hd * jax.nn.sigmoid(dg.astype(F32))
    out = jnp.concatenate([hc, hd], axis=-1).astype(h.dtype) @ p['w_out_odd'][o]
    return out, s1, k, v, logf


def half_ffn(x, p, layer, j):
    h = rmsnorm(x, p['norm_ffn_pre'][layer, j])
    u = jax.nn.silu(h @ p['ffn_w_gate'][layer, j]) * (h @ p['ffn_w_up'][layer, j])
    return 0.5 * rmsnorm(u @ p['ffn_w_down'][layer, j], p['norm_ffn_post'][layer, j])


def trunk(x, p, mlstm_c, mlstm_n, mlstm_m, gla_s, hgrn_s, fox_past):
    c_l, n_l, m_l, g_l, h_l, k_l, v_l, f_l = [], [], [], [], [], [], [], []
    for layer in range(DEPTH):
        x = x + half_ffn(x, p, layer, 0)
        h = rmsnorm(x, p['norm_mix_pre'][layer])
        if layer % 2 == 0:
            e = layer // 2
            mix, c1, n1, m1, s1 = even_mixer(h, p, e, mlstm_c[e], mlstm_n[e], mlstm_m[e], gla_s[e])
            c_l.append(c1); n_l.append(n1); m_l.append(m1); g_l.append(s1)
        else:
            o = layer // 2
            mix, s1, k1, v1, f1 = odd_mixer(h, p, o, layer, hgrn_s[o], fox_past[o])
            h_l.append(s1); k_l.append(k1); v_l.append(v1); f_l.append(f1)
        x = x + rmsnorm(mix, p['norm_mix_post'][layer])
        x = x + half_ffn(x, p, layer, 1)
    return (x, jnp.stack(c_l), jnp.stack(n_l), jnp.stack(m_l), jnp.stack(g_l),
            jnp.stack(h_l), jnp.stack(k_l), jnp.stack(v_l), jnp.stack(f_l))


def setup_inputs(seed: int = 0) -> dict:
    key = jax.random.key(seed)
    keys = iter(jax.random.split(key, 40))

    def normal(shape, scale):
        return scale * jax.random.normal(next(keys), shape, F32)

    def gain(shape):
        return 1.0 + normal(shape, 0.05)

    n_pages = PAST_LEN // PAGE_SIZE
    n_used = DEC_BATCH * n_pages
    n_phys = (n_used * 5) // 4
    page_table = jax.random.permutation(next(keys), n_phys)[:n_used].reshape(DEC_BATCH, n_pages).astype(jnp.int32)
    return {
        'x_prompt': normal((BATCH, SEQ, D_MODEL), 1.0),
        'x_sample': normal((DEC_BATCH, DEC_SEQ, D_MODEL), 1.0),
        'state_mlstm_C': normal((N_EVEN, DEC_BATCH, A_HEADS, A_DK, A_DV), 0.05),
        'state_mlstm_n': normal((N_EVEN, DEC_BATCH, A_HEADS, A_DK), 0.05),
        'state_mlstm_m': normal((N_EVEN, DEC_BATCH, A_HEADS), 0.5),
        'state_gla_S': normal((N_EVEN, DEC_BATCH, B_HEADS, B_DK, B_DV), 0.05),
        'state_hgrn_S': normal((N_ODD, DEC_BATCH, C_HEADS, C_DK, C_DV), 0.05),
        'cache_fox_k': normal((N_ODD, n_phys, PAGE_SIZE, D_HEADS, D_HEAD_DIM), 1.0),
        'cache_fox_v': normal((N_ODD, n_phys, PAGE_SIZE, D_HEADS, D_HEAD_DIM), 1.0),
        'cache_fox_logf': jax.nn.log_sigmoid(normal((N_ODD, n_phys, PAGE_SIZE, D_HEADS), 0.5) + 8.0),
        'page_table': page_table,
        'norm_mix_pre': gain((DEPTH, D_MODEL)),
        'norm_mix_post': gain((DEPTH, D_MODEL)),
        'norm_ffn_pre': gain((DEPTH, 2, D_MODEL)),
        'norm_ffn_post': gain((DEPTH, 2, D_MODEL)),
        'ffn_w_gate': normal((DEPTH, 2, D_MODEL, D_FF), D_MODEL ** -0.5),
        'ffn_w_up': normal((DEPTH, 2, D_MODEL, D_FF), D_MODEL ** -0.5),
        'ffn_w_down': normal((DEPTH, 2, D_FF, D_MODEL), D_FF ** -0.5),
        'w_in_even': normal((N_EVEN, D_MODEL, EVEN_IN), D_MODEL ** -0.5),
        'w_out_even': normal((N_EVEN, D_MODEL, D_MODEL), D_MODEL ** -0.5),
        'mlstm_b_i': normal((N_EVEN, A_HEADS), 0.1),
        'mlstm_b_f': 3.0 + normal((N_EVEN, A_HEADS), 0.1),
        'mlstm_norm': gain((N_EVEN, A_HEADS, A_DV)),
        'gla_w_gk2': normal((N_EVEN, B_RANK, B_HEADS * B_DK), B_RANK ** -0.5),
        'gla_b_gk': normal((N_EVEN, B_HEADS * B_DK), 0.1),
        'gla_norm': gain((N_EVEN, B_HEADS, B_DV)),
        'w_in_odd': normal((N_ODD, D_MODEL, ODD_IN), D_MODEL ** -0.5),
        'w_out_odd': normal((N_ODD, D_MODEL, D_MODEL), D_MODEL ** -0.5),
        'hgrn_lb_logits': normal((DEPTH, C_HEADS * C_DK), 0.5),
        'hgrn_norm': gain((N_ODD, C_HEADS, C_DV)),
        'fox_b_f': 5.0 + normal((N_ODD, D_HEADS), 0.1),
    }


def reference(x_prompt, x_sample, state_mlstm_C, state_mlstm_n, state_mlstm_m, state_gla_S, state_hgrn_S,
              cache_fox_k, cache_fox_v, cache_fox_logf, page_table,
              norm_mix_pre, norm_mix_post, norm_ffn_pre, norm_ffn_post, ffn_w_gate, ffn_w_up, ffn_w_down,
              w_in_even, w_out_even, mlstm_b_i, mlstm_b_f, mlstm_norm, gla_w_gk2, gla_b_gk, gla_norm,
              w_in_odd, w_out_odd, hgrn_lb_logits, hgrn_norm, fox_b_f):
    p = {'norm_mix_pre': norm_mix_pre, 'norm_mix_post': norm_mix_post,
         'norm_ffn_pre': norm_ffn_pre, 'norm_ffn_post': norm_ffn_post,
         'ffn_w_gate': ffn_w_gate, 'ffn_w_up': ffn_w_up, 'ffn_w_down': ffn_w_down,
         'w_in_even': w_in_even, 'w_out_even': w_out_even, 'mlstm_b_i': mlstm_b_i, 'mlstm_b_f': mlstm_b_f,
         'mlstm_norm': mlstm_norm, 'gla_w_gk2': gla_w_gk2, 'gla_b_gk': gla_b_gk, 'gla_norm': gla_norm,
         'w_in_odd': w_in_odd, 'w_out_odd': w_out_odd, 'hgrn_lb_logits': hgrn_lb_logits,
         'hgrn_norm': hgrn_norm, 'fox_b_f': fox_b_f}
    bp = x_prompt.shape[0]
    prompt_past = [(jnp.zeros((bp, 0, D_HEADS, D_HEAD_DIM), x_prompt.dtype),
                    jnp.zeros((bp, 0, D_HEADS, D_HEAD_DIM), x_prompt.dtype),
                    jnp.zeros((bp, 0, D_HEADS), F32)) for _ in range(N_ODD)]
    (y_prompt, c_p, n_p, m_p, gla_p, hgrn_p, k_p, v_p, f_p) = trunk(
        x_prompt, p,
        jnp.zeros((N_EVEN, bp, A_HEADS, A_DK, A_DV), F32),
        jnp.zeros((N_EVEN, bp, A_HEADS, A_DK), F32),
        jnp.zeros((N_EVEN, bp, A_HEADS), F32),
        jnp.zeros((N_EVEN, bp, B_HEADS, B_DK, B_DV), F32),
        jnp.zeros((N_ODD, bp, C_HEADS, C_DK, C_DV), F32),
        prompt_past)
    sample_past = [(gather_pages(cache_fox_k[j], page_table),
                    gather_pages(cache_fox_v[j], page_table),
                    gather_pages(cache_fox_logf[j], page_table)) for j in range(N_ODD)]
    (y_sample, c_s, n_s, m_s, gla_s, hgrn_s, k_s, v_s, f_s) = trunk(
        x_sample, p, state_mlstm_C, state_mlstm_n, state_mlstm_m, state_gla_S, state_hgrn_S, sample_past)
    return (y_prompt, y_sample, c_p, n_p, m_p, gla_p, hgrn_p, k_p, v_p, f_p,
            c_s, n_s, m_s, gla_s, hgrn_s, k_s, v_s, f_s)
```

```python
import functools

import jax
import jax.numpy as jnp
from jax import lax
from jax.experimental import pallas as pl
from jax.experimental.pallas import tpu as pltpu

F32 = jnp.float32
BF16 = jnp.bfloat16
EPS = 1e-6
NEG = -1e30
LANES = 128
VMEM_LIMIT = 56 * 1024 * 1024

B_GATE_NORMALIZER = 16.0
GLA_RANK = 16
PAGE = 128


def _cparams(sem):
    return pltpu.CompilerParams(dimension_semantics=sem, vmem_limit_bytes=VMEM_LIMIT)


def _log_sigmoid(x):
    return jnp.minimum(x, 0.0) - jnp.log1p(jnp.exp(-jnp.abs(x)))


def _sigmoid(x):
    return jax.nn.sigmoid(x)


def _nt(a, b):
    return lax.dot_general(a, b, (((1,), (1,)), ((), ())), preferred_element_type=F32)


def _tn(a, b):
    return lax.dot_general(a, b, (((0,), (0,)), ((), ())), preferred_element_type=F32)


def _nn(a, b):
    return jnp.dot(a, b, preferred_element_type=F32)


def _cumsum_rows(x):
    n = x.shape[0]
    row = lax.broadcasted_iota(jnp.int32, x.shape, 0)
    s = 1
    while s < n:
        x = x + jnp.where(row >= s, pltpu.roll(x, s, 0), 0.0)
        s *= 2
    return x


def _suffix_sum_rows(x):
    n = x.shape[0]
    row = lax.broadcasted_iota(jnp.int32, x.shape, 0)
    s = 1
    while s < n:
        x = x + jnp.where(row + s < n, pltpu.roll(x, n - s, 0), 0.0)
        s *= 2
    return x


def _split3(x):
    p1 = x.astype(BF16)
    r1 = x - p1.astype(F32)
    p2 = r1.astype(BF16)
    p3 = (r1 - p2.astype(F32)).astype(BF16)
    return p1, p2, p3


def _expand_exact(x, e_bf16):
    n = x.shape[0]
    parts = jnp.concatenate(_split3(x), axis=0)
    y = _nn(parts, e_bf16)
    return y[0:n] + y[n:2 * n] + y[2 * n:3 * n]


def _prenorm_kernel(x_ref, g_ref, o_ref):
    x = x_ref[...]
    r = lax.rsqrt(jnp.mean(x * x, axis=-1, keepdims=True) + EPS)
    o_ref[...] = (x * r * g_ref[...]).astype(o_ref.dtype)


def prenorm(x, g, *, tr):
    m, d = x.shape
    return pl.pallas_call(
        _prenorm_kernel,
        grid=(m // tr,),
        in_specs=[pl.BlockSpec((tr, d), lambda i: (i, 0)), pl.BlockSpec((1, d), lambda i: (0, 0))],
        out_specs=pl.BlockSpec((tr, d), lambda i: (i, 0)),
        out_shape=jax.ShapeDtypeStruct((m, d), BF16),
        compiler_params=_cparams(("parallel",)),
        name="prenorm",
    )(x, g.reshape(1, d))


def _resid_kernel(x_ref, y_ref, gp_ref, gn_ref, xo_ref, ho_ref, *, scale):
    y = y_ref[...]
    r = lax.rsqrt(jnp.mean(y * y, axis=-1, keepdims=True) + EPS)
    xn = x_ref[...] + scale * (y * r * gp_ref[...])
    xo_ref[...] = xn
    r2 = lax.rsqrt(jnp.mean(xn * xn, axis=-1, keepdims=True) + EPS)
    ho_ref[...] = (xn * r2 * gn_ref[...]).astype(ho_ref.dtype)


def resid_norm(x, y, g_post, g_next, *, scale, tr):
    m, d = x.shape
    row = pl.BlockSpec((tr, d), lambda i: (i, 0))
    vec = pl.BlockSpec((1, d), lambda i: (0, 0))
    return pl.pallas_call(
        functools.partial(_resid_kernel, scale=scale),
        grid=(m // tr,),
        in_specs=[row, row, vec, vec],
        out_specs=[row, row],
        out_shape=[jax.ShapeDtypeStruct((m, d), F32), jax.ShapeDtypeStruct((m, d), BF16)],
        compiler_params=_cparams(("parallel",)),
        name="resid_norm",
    )(x, y, g_post.reshape(1, d), g_next.reshape(1, d))


def _mm_kernel(*refs, n_pairs):
    o_ref = refs[-1]
    acc = _nn(refs[0][...], refs[1][...])
    for p in range(1, n_pairs):
        acc = acc + _nn(refs[2 * p][...], refs[2 * p + 1][...])
    o_ref[...] = acc.astype(o_ref.dtype)


def matmul(pairs, *, out_dtype, tm, tn):
    m = pairs[0][0].shape[0]
    n = pairs[0][1].shape[1]
    in_specs, args = [], []
    for x, w in pairs:
        k = x.shape[1]
        in_specs += [pl.BlockSpec((tm, k), lambda i, j: (i, 0)), pl.BlockSpec((k, tn), lambda i, j: (0, j))]
        args += [x, w]
    return pl.pallas_call(
        functools.partial(_mm_kernel, n_pairs=len(pairs)),
        grid=(m // tm, n // tn),
        in_specs=in_specs,
        out_specs=pl.BlockSpec((tm, tn), lambda i, j: (i, j)),
        out_shape=jax.ShapeDtypeStruct((m, n), out_dtype),
        compiler_params=_cparams(("parallel", "arbitrary")),
        name="matmul",
    )(*args)


def _ffn_up_kernel(h_ref, wg_ref, wu_ref, o_ref):
    h = h_ref[...]
    g = _nn(h, wg_ref[...])
    u = _nn(h, wu_ref[...])
    o_ref[...] = (g * _sigmoid(g) * u).astype(o_ref.dtype)


def ffn_up(h, wg, wu, *, tm, tn):
    m, k = h.shape
    n = wg.shape[1]
    wspec = pl.BlockSpec((k, tn), lambda i, j: (0, j))
    return pl.pallas_call(
        _ffn_up_kernel,
        grid=(m // tm, n // tn),
        in_specs=[pl.BlockSpec((tm, k), lambda i, j: (i, 0)), wspec, wspec],
        out_specs=pl.BlockSpec((tm, tn), lambda i, j: (i, j)),
        out_shape=jax.ShapeDtypeStruct((m, n), BF16),
        compiler_params=_cparams(("parallel", "arbitrary")),
        name="ffn_up",
    )(h, wg, wu)


def _mlstm_kernel(q_ref, k_ref, v_ref, og_ref, gt_ref, gb_ref, nrm_ref, c0_ref, n0_ref, m0_ref,
                  h_ref, c_ref, n_ref, m_ref, *, L, dk, n_heads):
    hh = pl.program_id(1)

    @pl.when(pl.program_id(2) == 0)
    def _():
        c_ref[...] = c0_ref[...]
        n_ref[...] = n0_ref[...]
        m_ref[...] = m0_ref[...]

    gates = gt_ref[...] + gb_ref[...]
    lane = lax.broadcasted_iota(jnp.int32, gates.shape, 1)
    i_col = jnp.sum(jnp.where(lane == hh, gates, 0.0), axis=1, keepdims=True)
    f_col = jnp.sum(jnp.where(lane == hh + n_heads, gates, 0.0), axis=1, keepdims=True)
    logf = _log_sigmoid(f_col)

    r = lax.broadcasted_iota(jnp.int32, (L, L), 0)
    s = lax.broadcasted_iota(jnp.int32, (L, L), 1)
    tril = r >= s
    eye = r == s

    def col2row(col):
        return jnp.sum(jnp.where(eye, col, 0.0), axis=0, keepdims=True)

    logf_row = col2row(logf)
    b_col = jnp.sum(jnp.where(tril, logf_row, 0.0), axis=1, keepdims=True)
    b_row = col2row(b_col)
    i_row = col2row(i_col)
    m_prev = m_ref[...]
    dmat = jnp.where(tril, b_col - b_row + i_row, -jnp.inf)
    from_state = b_col + m_prev
    m_t = jnp.maximum(from_state, jnp.max(dmat, axis=1, keepdims=True))

    q = q_ref[...]
    k = k_ref[...] * (dk ** -0.5)
    qb = q.astype(BF16)
    vb = v_ref[...].astype(BF16)
    w = jnp.exp(dmat - m_t) * _nt(qb, k.astype(BF16))
    s_state = jnp.exp(from_state - m_t)
    c_old = c_ref[...]
    n_old = n_ref[...]
    num = s_state * _nn(qb, c_old.astype(BF16)) + _nn(w.astype(BF16), vb)
    den = s_state * jnp.sum(q * n_old, axis=1, keepdims=True) + jnp.sum(w, axis=1, keepdims=True)
    h = num / jnp.maximum(jnp.abs(den), jnp.exp(-m_t))

    m_new = m_t[L - 1:L, :]
    b_last = b_col[L - 1:L, :]
    w_end = jnp.exp(b_last - b_col + i_col - m_new)
    s_end = jnp.exp(b_last + m_prev - m_new)
    kw = k * w_end
    c_ref[...] = s_end * c_old + _tn(kw.astype(BF16), vb)
    n_ref[...] = s_end * n_old + jnp.sum(kw, axis=0, keepdims=True)
    m_ref[...] = m_new

    hn = h * lax.rsqrt(jnp.mean(h * h, axis=1, keepdims=True) + EPS) * nrm_ref[...]
    h_ref[...] = (hn * _sigmoid(og_ref[...])).astype(h_ref.dtype)


def mlstm_mixer(proj, small, gate_bias, norm, c0, n0, m0, *, B, T, row_off, cols, H, dk, dv, L):
    nC = T // L
    rb = row_off // L
    cq, ck, cv, co = cols

    def rows(b, h, c):
        return rb + b * nC + c

    in_specs = [
        pl.BlockSpec((L, dk), lambda b, h, c: (rows(b, h, c), cq // dk + h)),
        pl.BlockSpec((L, dk), lambda b, h, c: (rows(b, h, c), ck // dk + h)),
        pl.BlockSpec((L, dv), lambda b, h, c: (rows(b, h, c), cv // dv + h)),
        pl.BlockSpec((L, dv), lambda b, h, c: (rows(b, h, c), co // dv + h)),
        pl.BlockSpec((L, LANES), lambda b, h, c: (rows(b, h, c), 0)),
        pl.BlockSpec((1, LANES), lambda b, h, c: (0, 0)),
        pl.BlockSpec((None, 1, dv), lambda b, h, c: (h, 0, 0)),
        pl.BlockSpec((None, None, dk, dv), lambda b, h, c: (b, h, 0, 0)),
        pl.BlockSpec((None, None, 1, dk), lambda b, h, c: (b, h, 0, 0)),
        pl.BlockSpec((None, None, 1, 1), lambda b, h, c: (b, h, 0, 0)),
    ]
    out_specs = [
        pl.BlockSpec((L, dv), lambda b, h, c: (b * nC + c, h)),
        pl.BlockSpec((None, None, dk, dv), lambda b, h, c: (b, h, 0, 0)),
        pl.BlockSpec((None, None, 1, dk), lambda b, h, c: (b, h, 0, 0)),
        pl.BlockSpec((None, None, 1, 1), lambda b, h, c: (b, h, 0, 0)),
    ]
    out_shape = [
        jax.ShapeDtypeStruct((B * T, H * dv), BF16),
        jax.ShapeDtypeStruct((B, H, dk, dv), F32),
        jax.ShapeDtypeStruct((B, H, 1, dk), F32),
        jax.ShapeDtypeStruct((B, H, 1, 1), F32),
    ]
    h, c1, n1, m1 = pl.pallas_call(
        functools.partial(_mlstm_kernel, L=L, dk=dk, n_heads=H),
        grid=(B, H, nC),
        in_specs=in_specs, out_specs=out_specs, out_shape=out_shape,
        compiler_params=_cparams(("parallel", "parallel", "arbitrary")),
        name="mlstm",
    )(proj, proj, proj, proj, small, gate_bias, norm.reshape(H, 1, dv), c0,
      n0.reshape(B, H, 1, dk), m0.reshape(B, H, 1, 1))
    return h, c1, n1.reshape(B, H, dk), m1.reshape(B, H)


def _scan_chunk(q, k, v, g, st, *, SB):
    CH = q.shape[0]
    bc = _cumsum_rows(g)
    b_last = bc[CH - 1:CH, :]
    vb = v.astype(BF16)
    o = _nt((q * jnp.exp(bc)).astype(BF16), st.astype(BF16))
    row = lax.broadcasted_iota(jnp.int32, bc.shape, 0)
    srow = lax.broadcasted_iota(jnp.int32, (SB, CH), 0)
    scol = lax.broadcasted_iota(jnp.int32, (SB, CH), 1)
    blocks = []
    for i in range(CH // SB):
        lo, hi = i * SB, (i + 1) * SB
        ref = bc[lo - 1:lo, :] if i > 0 else jnp.zeros_like(b_last)
        qi = q[lo:hi] * jnp.exp(bc[lo:hi] - ref)
        ki = k * jnp.exp(jnp.where(row < hi, ref - bc, 0.0))
        a = _nt(qi.astype(BF16), ki.astype(BF16))
        blocks.append(jnp.where(scol <= srow + lo, a, 0.0))
    a_full = blocks[0] if len(blocks) == 1 else jnp.concatenate(blocks, axis=0)
    o = o + _nn(a_full.astype(BF16), vb)
    k_dec = k * jnp.exp(b_last - bc)
    st_new = st * jnp.exp(b_last) + _tn(vb, k_dec.astype(BF16))
    return o, st_new


def _head_out(o, nrm, gate_act):
    return o * lax.rsqrt(jnp.mean(o * o, axis=1, keepdims=True) + EPS) * nrm * gate_act


def _gla_kernel(q_ref, k_ref, v_ref, og_ref, gt_ref, w2_ref, bgk_ref, nrm_ref, s0_ref,
                o_ref, s_ref, st_scr, *, SB, dk, n_chunks):
    c = pl.program_id(2)

    @pl.when(c == 0)
    def _():
        st_scr[...] = s0_ref[...].T

    gk = _nn(gt_ref[...].astype(BF16), w2_ref[...]) + bgk_ref[...]
    g = _log_sigmoid(gk) / B_GATE_NORMALIZER
    o, st_new = _scan_chunk(q_ref[...] * (dk ** -0.5), k_ref[...], v_ref[...], g, st_scr[...], SB=SB)
    st_scr[...] = st_new
    og = og_ref[...]
    o_ref[...] = _head_out(o, nrm_ref[...], og * _sigmoid(og)).astype(o_ref.dtype)

    @pl.when(c == n_chunks - 1)
    def _():
        s_ref[...] = st_new.T


def gla_mixer(proj, small, w2pad, b_gk, norm, s0, *, B, T, row_off, cols, H, dk, dv, CH, SB):
    nC = T // CH
    rb = row_off // CH
    cq, ck, cv, co = cols

    def rows(b, h, c):
        return rb + b * nC + c

    in_specs = [
        pl.BlockSpec((CH, dk), lambda b, h, c: (rows(b, h, c), cq // dk + h)),
        pl.BlockSpec((CH, dk), lambda b, h, c: (rows(b, h, c), ck // dk + h)),
        pl.BlockSpec((CH, dv), lambda b, h, c: (rows(b, h, c), cv // dv + h)),
        pl.BlockSpec((CH, dv), lambda b, h, c: (rows(b, h, c), co // dv + h)),
        pl.BlockSpec((CH, LANES), lambda b, h, c: (rows(b, h, c), 0)),
        pl.BlockSpec((None, LANES, dk), lambda b, h, c: (h, 0, 0)),
        pl.BlockSpec((None, 1, dk), lambda b, h, c: (h, 0, 0)),
        pl.BlockSpec((None, 1, dv), lambda b, h, c: (h, 0, 0)),
        pl.BlockSpec((None, None, dk, dv), lambda b, h, c: (b, h, 0, 0)),
    ]
    out_specs = [
        pl.BlockSpec((CH, dv), lambda b, h, c: (b * nC + c, h)),
        pl.BlockSpec((None, None, dk, dv), lambda b, h, c: (b, h, 0, 0)),
    ]
    out_shape = [jax.ShapeDtypeStruct((B * T, H * dv), BF16), jax.ShapeDtypeStruct((B, H, dk, dv), F32)]
    return pl.pallas_call(
        functools.partial(_gla_kernel, SB=SB, dk=dk, n_chunks=nC),
        grid=(B, H, nC),
        in_specs=in_specs, out_specs=out_specs, out_shape=out_shape,
        scratch_shapes=[pltpu.VMEM((dv, dk), F32)],
        compiler_params=_cparams(("parallel", "parallel", "arbitrary")),
        name="gla",
    )(proj, proj, proj, proj, small, w2pad, b_gk.reshape(H, 1, dk), norm.reshape(H, 1, dv), s0)


def _hgrn_kernel(q_ref, f_ref, i_ref, og_ref, lg_ref, nrm_ref, s0_ref, o_ref, s_ref, st_scr,
                 *, SB, layer, n_chunks):
    c = pl.program_id(2)

    @pl.when(c == 0)
    def _():
        st_scr[...] = s0_ref[...].T

    lg = lg_ref[...]
    e = jnp.exp(lg - jnp.max(lg, axis=0, keepdims=True))
    sm = e / jnp.sum(e, axis=0, keepdims=True)
    lb = jnp.sum(sm[1:layer + 1], axis=0, keepdims=True) if layer > 0 else jnp.zeros_like(sm[0:1])
    z = f_ref[...]
    g = jnp.log(lb + (1.0 - lb) * _sigmoid(z))
    key = (1.0 - lb) * _sigmoid(-z)
    cq = q_ref[...]
    o, st_new = _scan_chunk(cq * _sigmoid(cq), key, i_ref[...], g, st_scr[...], SB=SB)
    st_scr[...] = st_new
    og = og_ref[...]
    o_ref[...] = _head_out(o, nrm_ref[...], og * _sigmoid(og)).astype(o_ref.dtype)

    @pl.when(c == n_chunks - 1)
    def _():
        s_ref[...] = st_new.T


def hgrn_mixer(proj, lb_logits, norm, s0, *, B, T, row_off, cols, H, dk, dv, CH, SB, layer):
    nC = T // CH
    rb = row_off // CH
    cq, cf, ci, co = cols
    depth = lb_logits.shape[0]

    def rows(b, h, c):
        return rb + b * nC + c

    def col(c0):
        return pl.BlockSpec((CH, dk), lambda b, h, c: (rows(b, h, c), c0 // dk + h))

    in_specs = [
        col(cq), col(cf), col(ci), col(co),
        pl.BlockSpec((depth, dk), lambda b, h, c: (0, h)),
        pl.BlockSpec((None, 1, dv), lambda b, h, c: (h, 0, 0)),
        pl.BlockSpec((None, None, dk, dv), lambda b, h, c: (b, h, 0, 0)),
    ]
    out_specs = [
        pl.BlockSpec((CH, dv), lambda b, h, c: (b * nC + c, h)),
        pl.BlockSpec((None, None, dk, dv), lambda b, h, c: (b, h, 0, 0)),
    ]
    out_shape = [jax.ShapeDtypeStruct((B * T, H * dv), BF16), jax.ShapeDtypeStruct((B, H, dk, dv), F32)]
    return pl.pallas_call(
        functools.partial(_hgrn_kernel, SB=SB, layer=layer, n_chunks=nC),
        grid=(B, H, nC),
        in_specs=in_specs, out_specs=out_specs, out_shape=out_shape,
        scratch_shapes=[pltpu.VMEM((dv, dk), F32)],
        compiler_params=_cparams(("parallel", "parallel", "arbitrary")),
        name="hgrn",
    )(proj, proj, proj, proj, lb_logits, norm.reshape(H, 1, dv), s0)


def _fox_cumsum_kernel(x_ref, b_ref, lf_ref, c_ref):
    lf = _log_sigmoid(x_ref[...] + b_ref[...])
    lf_ref[...] = lf
    t = lf.shape[1]
    lane = lax.broadcasted_iota(jnp.int32, lf.shape, 1)
    c = lf
    s = 1
    while s < t:
        c = c + jnp.where(lane >= s, pltpu.roll(c, s, 1), 0.0)
        s *= 2
    c_ref[...] = c


def fox_cumsum(df_t, bias):
    B, H, T = df_t.shape
    spec = pl.BlockSpec((None, H, T), lambda b: (b, 0, 0))
    return pl.pallas_call(
        _fox_cumsum_kernel,
        grid=(B,),
        in_specs=[spec, pl.BlockSpec((H, 1), lambda b: (0, 0))],
        out_specs=[spec, spec],
        out_shape=[jax.ShapeDtypeStruct((B, H, T), F32)] * 2,
        compiler_params=_cparams(("parallel",)),
        name="fox_cumsum",
    )(df_t, bias.reshape(H, 1))


def _fox_prompt_kernel(q_ref, k_ref, v_ref, og_ref, cc_ref, cr_ref, o_ref, *, TQ, dh):
    hh = pl.program_id(1)
    qi = pl.program_id(2)
    q = (q_ref[...] * (dh ** -0.5)).astype(BF16)
    cc = cc_ref[...]
    lane = lax.broadcasted_iota(jnp.int32, cc.shape, 1)
    c_col = jnp.sum(jnp.where(lane == hh, cc, 0.0), axis=1, keepdims=True)
    qpos = qi * TQ + lax.broadcasted_iota(jnp.int32, (TQ, TQ), 0)
    kofs = lax.broadcasted_iota(jnp.int32, (TQ, TQ), 1)

    def body(j, carry):
        m, l, acc = carry
        off = pl.multiple_of(j * TQ, TQ)
        kb = k_ref[pl.ds(off, TQ), :].astype(BF16)
        vb = v_ref[pl.ds(off, TQ), :].astype(BF16)
        s = _nt(q, kb) + c_col - cr_ref[:, pl.ds(off, TQ)]
        s = jnp.where(kofs + j * TQ <= qpos, s, NEG)
        m_new = jnp.maximum(m, jnp.max(s, axis=1, keepdims=True))
        alpha = jnp.exp(m - m_new)
        p = jnp.exp(s - m_new)
        l = alpha * l + jnp.sum(p, axis=1, keepdims=True)
        acc = alpha * acc + _nn(p.astype(BF16), vb)
        return m_new, l, acc

    init = (jnp.full((TQ, 1), NEG, F32), jnp.zeros((TQ, 1), F32), jnp.zeros((TQ, dh), F32))
    _, l, acc = lax.fori_loop(0, qi + 1, body, init)
    o_ref[...] = (acc / l * _sigmoid(og_ref[...])).astype(o_ref.dtype)


def fox_prompt(proj, c_col, c_row, *, B, T, row_off, cols, H, dh, TQ):
    nQ = T // TQ
    cq, ck, cv, co = cols
    in_specs = [
        pl.BlockSpec((TQ, dh), lambda b, h, i: (row_off // TQ + b * nQ + i, cq // dh + h)),
        pl.BlockSpec((T, dh), lambda b, h, i: (row_off // T + b, ck // dh + h)),
        pl.BlockSpec((T, dh), lambda b, h, i: (row_off // T + b, cv // dh + h)),
        pl.BlockSpec((TQ, dh), lambda b, h, i: (row_off // TQ + b * nQ + i, co // dh + h)),
        pl.BlockSpec((TQ, H), lambda b, h, i: (b * nQ + i, 0)),
        pl.BlockSpec((None, None, 1, T), lambda b, h, i: (b, h, 0, 0)),
    ]
    return pl.pallas_call(
        functools.partial(_fox_prompt_kernel, TQ=TQ, dh=dh),
        grid=(B, H, nQ),
        in_specs=in_specs,
        out_specs=pl.BlockSpec((TQ, dh), lambda b, h, i: (b * nQ + i, h)),
        out_shape=jax.ShapeDtypeStruct((B * T, H * dh), BF16),
        compiler_params=_cparams(("parallel", "parallel", "arbitrary")),
        name="fox_prompt",
    )(proj, proj, proj, proj, c_col, c_row)


def _fox_sample_kernel(pt_ref, q_ref, kn_ref, vn_ref, og_ref, df_ref, fb_ref, *rest, PP, T, H, dh, n_steps):
    k_refs = rest[0:PP]
    v_refs = rest[PP:2 * PP]
    f_refs = rest[2 * PP:3 * PP]
    o_ref, lf_ref = rest[3 * PP], rest[3 * PP + 1]
    qsel, acc, m_scr, l_scr, carry, crow, fpad = rest[3 * PP + 2:]
    s_id = pl.program_id(1)
    HT = H * T
    D = H * dh

    r2 = lax.broadcasted_iota(jnp.int32, (LANES, LANES), 0)
    c2 = lax.broadcasted_iota(jnp.int32, (LANES, LANES), 1)
    expand = jnp.where((r2 < H) & (c2 // T == r2), 1.0, 0.0).astype(BF16)
    eye = r2 == c2

    def row2col(x):
        return jnp.sum(jnp.where(eye, x, 0.0), axis=1, keepdims=True)

    @pl.when(s_id == 0)
    def _():
        q = q_ref[...] * (dh ** -0.5)
        qt = jnp.concatenate([q] * H, axis=0)
        rr = lax.broadcasted_iota(jnp.int32, (HT, D), 0)
        cc = lax.broadcasted_iota(jnp.int32, (HT, D), 1)
        qsel[...] = jnp.where(rr // T == cc // dh, qt, 0.0).astype(BF16)
        acc[...] = jnp.zeros_like(acc)
        m_scr[...] = jnp.full_like(m_scr, NEG)
        l_scr[...] = jnp.zeros_like(l_scr)
        carry[...] = jnp.zeros_like(carry)
        fpad[...] = jnp.zeros_like(fpad)
        lf_new = _log_sigmoid(df_ref[...] + fb_ref[...])
        lf_ref[...] = lf_new
        c_exp = _expand_exact(_cumsum_rows(lf_new), expand)
        rs = lax.broadcasted_iota(jnp.int32, (T, LANES), 0)
        cs = lax.broadcasted_iota(jnp.int32, (T, LANES), 1)
        crow[...] = jnp.sum(jnp.where(cs % T == rs, c_exp, 0.0), axis=0, keepdims=True)

    d_list = []
    cr = carry[...]
    for p in range(PP):
        fpad[p, :, 0:H] = f_refs[p][...]
        lf = fpad[p]
        d_list.append(_suffix_sum_rows(lf) - lf + cr)
        cr = cr + jnp.sum(lf, axis=0, keepdims=True)
    carry[...] = cr
    d_all = d_list[0] if PP == 1 else jnp.concatenate(d_list, axis=0)
    kb = jnp.concatenate([k_refs[p][...].astype(BF16) for p in range(PP)], axis=0) if PP > 1 \
        else k_refs[0][...].astype(BF16)
    vb = jnp.concatenate([v_refs[p][...].astype(BF16) for p in range(PP)], axis=0) if PP > 1 \
        else v_refs[0][...].astype(BF16)
    qs = qsel[...]
    c_row = crow[...]
    st = _nt(kb, qs) + _expand_exact(d_all, expand) + c_row
    m_old = m_scr[...]
    m_new = jnp.maximum(m_old, jnp.max(st, axis=0, keepdims=True))
    alpha = jnp.exp(m_old - m_new)
    p_t = jnp.exp(st - m_new)
    l_new = alpha * l_scr[...] + jnp.sum(p_t, axis=0, keepdims=True)
    acc_new = row2col(alpha) * acc[...] + _tn(p_t.astype(BF16), vb)
    m_scr[...] = m_new
    l_scr[...] = l_new
    acc[...] = acc_new

    @pl.when(s_id == n_steps - 1)
    def _():
        knb = kn_ref[...].astype(BF16)
        vnb = vn_ref[...].astype(BF16)
        lf_new = _log_sigmoid(df_ref[...] + fb_ref[...])
        c_exp = _expand_exact(_cumsum_rows(lf_new), expand)
        sn = _nt(knb, qs) + c_row - c_exp
        rs = lax.broadcasted_iota(jnp.int32, (T, LANES), 0)
        cs = lax.broadcasted_iota(jnp.int32, (T, LANES), 1)
        sn = jnp.where(rs <= cs % T, sn, NEG)
        m_fin = jnp.maximum(m_new, jnp.max(sn, axis=0, keepdims=True))
        a2 = jnp.exp(m_new - m_fin)
        pn = jnp.exp(sn - m_fin)
        l_fin = a2 * l_new + jnp.sum(pn, axis=0, keepdims=True)
        out = (row2col(a2) * acc_new + _tn(pn.astype(BF16), vnb)) / row2col(l_fin)
        og = og_ref[...]
        for h in range(H):
            blk = out[h * T:(h + 1) * T, h * dh:(h + 1) * dh]
            o_ref[:, h * dh:(h + 1) * dh] = (blk * _sigmoid(og[:, h * dh:(h + 1) * dh])).astype(o_ref.dtype)


def fox_sample(proj, small, fbias, cache_k, cache_v, cache_f, page_table, *, B, T, row_off, cols, H, dh, PP):
    assert H * T == LANES
    D = H * dh
    n_pages = page_table.shape[1]
    n_steps = n_pages // PP
    cq, ck, cv, co = cols
    rb = row_off // T

    def rowspec(c0):
        return pl.BlockSpec((T, D), lambda b, s, pt: (rb + b, c0 // D))

    def page(p):
        return lambda b, s, pt: (pt[b, n_pages - 1 - (s * PP + p)], 0, 0)

    in_specs = [rowspec(cq), rowspec(ck), rowspec(cv), rowspec(co),
                pl.BlockSpec((T, LANES), lambda b, s, pt: (rb + b, 0)),
                pl.BlockSpec((1, LANES), lambda b, s, pt: (0, 0))]
    in_specs += [pl.BlockSpec((None, PAGE, D), page(p)) for p in range(PP)]
    in_specs += [pl.BlockSpec((None, PAGE, D), page(p)) for p in range(PP)]
    in_specs += [pl.BlockSpec((None, PAGE, H), page(p)) for p in range(PP)]
    grid_spec = pltpu.PrefetchScalarGridSpec(
        num_scalar_prefetch=1,
        grid=(B, n_steps),
        in_specs=in_specs,
        out_specs=[pl.BlockSpec((T, D), lambda b, s, pt: (b, 0)),
                   pl.BlockSpec((T, LANES), lambda b, s, pt: (b, 0))],
        scratch_shapes=[
            pltpu.VMEM((LANES, D), BF16),
            pltpu.VMEM((LANES, D), F32),
            pltpu.VMEM((1, LANES), F32),
            pltpu.VMEM((1, LANES), F32),
            pltpu.VMEM((1, LANES), F32),
            pltpu.VMEM((1, LANES), F32),
            pltpu.VMEM((PP, PAGE, LANES), F32),
        ],
    )
    return pl.pallas_call(
        functools.partial(_fox_sample_kernel, PP=PP, T=T, H=H, dh=dh, n_steps=n_steps),
        grid_spec=grid_spec,
        out_shape=[jax.ShapeDtypeStruct((B * T, D), BF16), jax.ShapeDtypeStruct((B * T, LANES), F32)],
        compiler_params=_cparams(("parallel", "arbitrary")),
        name="fox_sample",
    )(page_table, proj, proj, proj, proj, small, fbias,
      *([cache_k] * PP), *([cache_v] * PP), *([cache_f] * PP))


def _pad_lanes(w):
    return jnp.pad(w, ((0, 0), (0, LANES - w.shape[1])))


def _row_tile(m, cap):
    t = min(m, cap)
    while m % t:
        t //= 2
    return t


def _col_tile(n, cap):
    t = cap
    while n % t:
        t //= 2
    return t


def _trunk(x, B, T, p, states, fox_past, page_table):
    M, D = x.shape
    depth = p['norm_mix_pre'].shape[0]
    a_heads = p['mlstm_b_i'].shape[1]
    a_dv = p['mlstm_norm'].shape[2]
    a_dk = states['mlstm_C'].shape[3]
    b_heads, b_dv = p['gla_norm'].shape[1:]
    b_dk = states['gla_S'].shape[3]
    c_heads, c_dv = p['hgrn_norm'].shape[1:]
    c_dk = states['hgrn_S'].shape[3]
    d_heads = p['fox_b_f'].shape[1]
    half = D // 2
    d_dh = half // d_heads
    ff = p['ffn_w_gate'].shape[3]

    tm = _row_tile(M, 1024)
    tr = _row_tile(M, 256)

    def ffn(h, layer, j):
        wg = p['ffn_w_gate'][layer, j].astype(BF16)
        wu = p['ffn_w_up'][layer, j].astype(BF16)
        wd = p['ffn_w_down'][layer, j].astype(BF16)
        u = ffn_up(h, wg, wu, tm=tm, tn=_col_tile(ff, 256))
        return matmul([(u, wd)], out_dtype=F32, tm=_row_tile(M, 512), tn=_col_tile(D, 512))

    outs = {k: [] for k in ('c', 'n', 'm', 'g', 'h', 'k', 'v', 'f')}
    h = prenorm(x, p['norm_ffn_pre'][0, 0], tr=tr)
    for layer in range(depth):
        y = ffn(h, layer, 0)
        x, h = resid_norm(x, y, p['norm_ffn_post'][layer, 0], p['norm_mix_pre'][layer], scale=0.5, tr=tr)
        if layer % 2 == 0:
            e = layer // 2
            w = p['w_in_even'][e]
            n_a = a_heads * (2 * a_dk + 2 * a_dv)
            n_b = b_heads * (2 * b_dk + 2 * b_dv)
            o_b = n_a + 2 * a_heads
            w_main = jnp.concatenate([w[:, :n_a], w[:, o_b:o_b + n_b]], axis=1).astype(BF16)
            w_small = _pad_lanes(jnp.concatenate([w[:, n_a:o_b], w[:, o_b + n_b:]], axis=1)).astype(BF16)
            proj = matmul([(h, w_main)], out_dtype=F32, tm=tm, tn=_col_tile(n_a + n_b, 1024))
            small = matmul([(h, w_small)], out_dtype=F32, tm=tm, tn=LANES)
            gate_bias = _pad_lanes(jnp.concatenate([p['mlstm_b_i'][e], p['mlstm_b_f'][e]])[None, :])
            qa, ka = 0, a_heads * a_dk
            va, oa = 2 * a_heads * a_dk, 2 * a_heads * a_dk + a_heads * a_dv
            L = min(T, 128)
            ha, c1, n1, m1 = mlstm_mixer(proj, small, gate_bias, p['mlstm_norm'][e],
                                         states['mlstm_C'][e], states['mlstm_n'][e], states['mlstm_m'][e],
                                         B=B, T=T, row_off=0, cols=(qa, ka, va, oa),
                                         H=a_heads, dk=a_dk, dv=a_dv, L=L)
            w2 = p['gla_w_gk2'][e]
            rank = w2.shape[0]
            w2pad = jnp.zeros((LANES, b_heads * b_dk), F32).at[2 * a_heads:2 * a_heads + rank].set(w2)
            w2pad = w2pad.reshape(LANES, b_heads, b_dk).transpose(1, 0, 2).astype(BF16)
            qb = n_a
            kb = qb + b_heads * b_dk
            vb = kb + b_heads * b_dk
            ob = vb + b_heads * b_dv
            CH = min(T, 128)
            hb, s1 = gla_mixer(proj, small, w2pad, p['gla_b_gk'][e], p['gla_norm'][e], states['gla_S'][e],
                               B=B, T=T, row_off=0, cols=(qb, kb, vb, ob),
                               H=b_heads, dk=b_dk, dv=b_dv, CH=CH, SB=min(CH, 32))
            wo = p['w_out_even'][e].astype(BF16)
            mix = matmul([(ha, wo[:half]), (hb, wo[half:])], out_dtype=F32, tm=tm, tn=_col_tile(D, 512))
            outs['c'].append(c1); outs['n'].append(n1); outs['m'].append(m1); outs['g'].append(s1)
        else:
            o = layer // 2
            w = p['w_in_odd'][o]
            n_main = 4 * c_heads * c_dk + 4 * half
            w_main = w[:, :n_main].astype(BF16)
            w_small = _pad_lanes(w[:, n_main:]).astype(BF16)
            proj = matmul([(h, w_main)], out_dtype=F32, tm=tm, tn=_col_tile(n_main, 1024))
            small = matmul([(h, w_small)], out_dtype=F32, tm=tm, tn=LANES)
            cw = c_heads * c_dk
            CH = min(T, 128)
            hc, s1 = hgrn_mixer(proj, p['hgrn_lb_logits'], p['hgrn_norm'][o], states['hgrn_S'][o],
                                B=B, T=T, row_off=0, cols=(0, cw, 2 * cw, 3 * cw),
                                H=c_heads, dk=c_dk, dv=c_dv, CH=CH, SB=min(CH, 32), layer=layer)
            dq = 4 * cw
            cols = (dq, dq + half, dq + 2 * half, dq + 3 * half)
            if fox_past is None:
                df_t = small[:, :d_heads].reshape(B, T, d_heads).transpose(0, 2, 1)
                lf_t, c_t = fox_cumsum(df_t, p['fox_b_f'][o])
                logf = lf_t.transpose(0, 2, 1)
                c_col = c_t.transpose(0, 2, 1).reshape(B * T, d_heads)
                hd = fox_prompt(proj, c_col, c_t.reshape(B, d_heads, 1, T), B=B, T=T, row_off=0,
                                cols=cols, H=d_heads, dh=d_dh, TQ=min(T, 256))
            else:
                ck, cv, cf = fox_past[o]
                n_phys = ck.shape[0]
                hd, lf = fox_sample(proj, small, _pad_lanes(p['fox_b_f'][o][None, :]),
                                    ck.reshape(n_phys, PAGE, half), cv.reshape(n_phys, PAGE, half), cf,
                                    page_table, B=B, T=T, row_off=0, cols=cols, H=d_heads, dh=d_dh, PP=4)
                logf = lf[:, :d_heads].reshape(B, T, d_heads)
            wo = p['w_out_odd'][o].astype(BF16)
            mix = matmul([(hc, wo[:half]), (hd, wo[half:])], out_dtype=F32, tm=tm, tn=_col_tile(D, 512))
            outs['h'].append(s1)
            outs['k'].append(proj[:, cols[1]:cols[1] + half].reshape(B, T, d_heads, d_dh))
            outs['v'].append(proj[:, cols[2]:cols[2] + half].reshape(B, T, d_heads, d_dh))
            outs['f'].append(logf)
        x, h = resid_norm(x, mix, p['norm_mix_post'][layer], p['norm_ffn_pre'][layer, 1], scale=1.0, tr=tr)
        y = ffn(h, layer, 1)
        g_next = p['norm_ffn_pre'][layer + 1, 0] if layer + 1 < depth else p['norm_ffn_pre'][layer, 0]
        x, h = resid_norm(x, y, p['norm_ffn_post'][layer, 1], g_next, scale=0.5, tr=tr)
    return (x,) + tuple(jnp.stack(outs[k]) for k in ('c', 'n', 'm', 'g', 'h', 'k', 'v', 'f'))


def kernel(x_prompt, x_sample, state_mlstm_C, state_mlstm_n, state_mlstm_m, state_gla_S, state_hgrn_S, cache_fox_k, cache_fox_v, cache_fox_logf, page_table, norm_mix_pre, norm_mix_post, norm_ffn_pre, norm_ffn_post, ffn_w_gate, ffn_w_up, ffn_w_down, w_in_even, w_out_even, mlstm_b_i, mlstm_b_f, mlstm_norm, gla_w_gk2, gla_b_gk, gla_norm, w_in_odd, w_out_odd, hgrn_lb_logits, hgrn_norm, fox_b_f):
    p = {'norm_mix_pre': norm_mix_pre, 'norm_mix_post': norm_mix_post,
         'norm_ffn_pre': norm_ffn_pre, 'norm_ffn_post': norm_ffn_post,
         'ffn_w_gate': ffn_w_gate, 'ffn_w_up': ffn_w_up, 'ffn_w_down': ffn_w_down,
         'w_in_even': w_in_even, 'w_out_even': w_out_even, 'mlstm_b_i': mlstm_b_i, 'mlstm_b_f': mlstm_b_f,
         'mlstm_norm': mlstm_norm, 'gla_w_gk2': gla_w_gk2, 'gla_b_gk': gla_b_gk, 'gla_norm': gla_norm,
         'w_in_odd': w_in_odd, 'w_out_odd': w_out_odd, 'hgrn_lb_logits': hgrn_lb_logits,
         'hgrn_norm': hgrn_norm, 'fox_b_f': fox_b_f}
    bp, tp, d = x_prompt.shape
    bs, ts, _ = x_sample.shape
    zero_states = {
        'mlstm_C': jnp.zeros((state_mlstm_C.shape[0], bp) + state_mlstm_C.shape[2:], F32),
        'mlstm_n': jnp.zeros((state_mlstm_n.shape[0], bp) + state_mlstm_n.shape[2:], F32),
        'mlstm_m': jnp.zeros((state_mlstm_m.shape[0], bp) + state_mlstm_m.shape[2:], F32),
        'gla_S': jnp.zeros((state_gla_S.shape[0], bp) + state_gla_S.shape[2:], F32),
        'hgrn_S': jnp.zeros((state_hgrn_S.shape[0], bp) + state_hgrn_S.shape[2:], F32),
    }
    res_p = _trunk(x_prompt.reshape(bp * tp, d), bp, tp, p, zero_states, None, None)
    sample_states = {'mlstm_C': state_mlstm_C, 'mlstm_n': state_mlstm_n, 'mlstm_m': state_mlstm_m,
                     'gla_S': state_gla_S, 'hgrn_S': state_hgrn_S}
    n_odd = cache_fox_k.shape[0]
    past = [(cache_fox_k[j], cache_fox_v[j], cache_fox_logf[j]) for j in range(n_odd)]
    res_s = _trunk(x_sample.reshape(bs * ts, d), bs, ts, p, sample_states, past, page_table)
    y_p = res_p[0].reshape(bp, tp, d)
    y_s = res_s[0].reshape(bs, ts, d)
    return (y_p, y_s) + res_p[1:] + res_s[1:]
```

```python
import functools

import jax
import jax.numpy as jnp
from jax import lax
from jax.experimental import pallas as pl
from jax.experimental.pallas import tpu as pltpu

F32 = jnp.float32
BF16 = jnp.bfloat16
EPS = 1e-6
NEG = -1e30
LANES = 128
BF16_ROWS = 16
VMEM_LIMIT = 56 * 1024 * 1024

B_GATE_NORMALIZER = 16.0
PAGE = 128


def _cparams(sem):
    return pltpu.CompilerParams(dimension_semantics=sem, vmem_limit_bytes=VMEM_LIMIT)


def _log_sigmoid(x):
    return jnp.minimum(x, 0.0) - jnp.log1p(jnp.exp(-jnp.abs(x)))


def _sigmoid(x):
    return jax.nn.sigmoid(x)


def _nt(a, b):
    return lax.dot_general(a, b, (((1,), (1,)), ((), ())), preferred_element_type=F32)


def _tn(a, b):
    return lax.dot_general(a, b, (((0,), (0,)), ((), ())), preferred_element_type=F32)


def _nn(a, b):
    return jnp.dot(a, b, preferred_element_type=F32)


def _cumsum_rows(x):
    n = x.shape[0]
    row = lax.broadcasted_iota(jnp.int32, x.shape, 0)
    s = 1
    while s < n:
        x = x + jnp.where(row >= s, pltpu.roll(x, s, 0), 0.0)
        s *= 2
    return x


def _cumsum_lanes(x, step):
    n = x.shape[1]
    lane = lax.broadcasted_iota(jnp.int32, x.shape, 1)
    s = step
    while s < n:
        x = x + jnp.where(lane >= s, pltpu.roll(x, s, 1), 0.0)
        s *= 2
    return x


def _prenorm_kernel(x_ref, g_ref, o_ref):
    x = x_ref[...]
    r = lax.rsqrt(jnp.mean(x * x, axis=-1, keepdims=True) + EPS)
    o_ref[...] = (x * r * g_ref[...]).astype(o_ref.dtype)


def prenorm(x, g, *, tr):
    m, d = x.shape
    return pl.pallas_call(
        _prenorm_kernel,
        grid=(m // tr,),
        in_specs=[pl.BlockSpec((tr, d), lambda i: (i, 0)), pl.BlockSpec((1, d), lambda i: (0, 0))],
        out_specs=pl.BlockSpec((tr, d), lambda i: (i, 0)),
        out_shape=jax.ShapeDtypeStruct((m, d), BF16),
        compiler_params=_cparams(("parallel",)),
        name="prenorm",
    )(x, g.reshape(1, d))


def _resid_kernel(x_ref, y_ref, gp_ref, gn_ref, xo_ref, ho_ref, *, scale):
    y = y_ref[...]
    r = lax.rsqrt(jnp.mean(y * y, axis=-1, keepdims=True) + EPS)
    xn = x_ref[...] + scale * (y * r * gp_ref[...])
    xo_ref[...] = xn
    r2 = lax.rsqrt(jnp.mean(xn * xn, axis=-1, keepdims=True) + EPS)
    ho_ref[...] = (xn * r2 * gn_ref[...]).astype(ho_ref.dtype)


def resid_norm(x, y, g_post, g_next, *, scale, tr):
    m, d = x.shape
    row = pl.BlockSpec((tr, d), lambda i: (i, 0))
    vec = pl.BlockSpec((1, d), lambda i: (0, 0))
    return pl.pallas_call(
        functools.partial(_resid_kernel, scale=scale),
        grid=(m // tr,),
        in_specs=[row, row, vec, vec],
        out_specs=[row, row],
        out_shape=[jax.ShapeDtypeStruct((m, d), F32), jax.ShapeDtypeStruct((m, d), BF16)],
        compiler_params=_cparams(("parallel",)),
        name="resid_norm",
    )(x, y, g_post.reshape(1, d), g_next.reshape(1, d))


def _mm_kernel(*refs, n_pairs, nk):
    o_ref = refs[-1]
    acc = _nn(refs[0][...], refs[1][...])
    for p in range(1, n_pairs):
        acc = acc + _nn(refs[2 * p][...], refs[2 * p + 1][...])
    if nk == 1:
        o_ref[...] = acc.astype(o_ref.dtype)
    else:
        k = pl.program_id(2)

        @pl.when(k == 0)
        def _():
            o_ref[...] = acc

        @pl.when(k > 0)
        def _():
            o_ref[...] += acc


def matmul(pairs, *, out_dtype, tm, tn, nk=1):
    m = pairs[0][0].shape[0]
    n = pairs[0][1].shape[-1]
    assert nk == 1 or out_dtype == F32
    in_specs, args = [], []
    for x, w, lead, kb0 in pairs:
        tk = x.shape[1] // nk
        in_specs.append(pl.BlockSpec((tm, tk), lambda i, j, k: (i, k)))
        in_specs.append(pl.BlockSpec((None,) * len(lead) + (tk, tn),
                                     lambda i, j, k, lead=lead, kb0=kb0, nk=nk: lead + (kb0 * nk + k, j)))
        args += [x, w]
    return pl.pallas_call(
        functools.partial(_mm_kernel, n_pairs=len(pairs), nk=nk),
        grid=(m // tm, n // tn, nk),
        in_specs=in_specs,
        out_specs=pl.BlockSpec((tm, tn), lambda i, j, k: (i, j)),
        out_shape=jax.ShapeDtypeStruct((m, n), out_dtype),
        compiler_params=_cparams(("parallel", "parallel", "arbitrary")),
        name="matmul",
    )(*args)


def _ffn_up_kernel(h_ref, wg_ref, wu_ref, o_ref):
    h = h_ref[...]
    g = _nn(h, wg_ref[...])
    u = _nn(h, wu_ref[...])
    o_ref[...] = (g * _sigmoid(g) * u).astype(o_ref.dtype)


def ffn_up(h, wg, wu, lead, *, tm, tn):
    m, k = h.shape
    n = wg.shape[-1]
    wspec = pl.BlockSpec((None,) * len(lead) + (k, tn), lambda i, j: lead + (0, j))
    return pl.pallas_call(
        _ffn_up_kernel,
        grid=(m // tm, n // tn),
        in_specs=[pl.BlockSpec((tm, k), lambda i, j: (i, 0)), wspec, wspec],
        out_specs=pl.BlockSpec((tm, tn), lambda i, j: (i, j)),
        out_shape=jax.ShapeDtypeStruct((m, n), BF16),
        compiler_params=_cparams(("parallel", "arbitrary")),
        name="ffn_up",
    )(h, wg, wu)


def _mlstm_kernel(q_ref, k_ref, v_ref, og_ref, gt_ref, gb_ref, nrm_ref, c0_ref, n0_ref, m0_ref,
                  h_ref, c_ref, n_ref, m_ref, *, L, dk, n_heads):
    hh = pl.program_id(1)

    @pl.when(pl.program_id(2) == 0)
    def _():
        c_ref[...] = c0_ref[...]
        n_ref[...] = n0_ref[...]
        m_ref[...] = m0_ref[...]

    gates = gt_ref[...] + gb_ref[...]
    lane = lax.broadcasted_iota(jnp.int32, gates.shape, 1)
    i_col = jnp.sum(jnp.where(lane == hh, gates, 0.0), axis=1, keepdims=True)
    f_col = jnp.sum(jnp.where(lane == hh + n_heads, gates, 0.0), axis=1, keepdims=True)
    logf = _log_sigmoid(f_col)

    r = lax.broadcasted_iota(jnp.int32, (L, L), 0)
    s = lax.broadcasted_iota(jnp.int32, (L, L), 1)
    tril = r >= s
    eye = r == s

    def col2row(col):
        return jnp.sum(jnp.where(eye, col, 0.0), axis=0, keepdims=True)

    logf_row = col2row(logf)
    b_col = jnp.sum(jnp.where(tril, logf_row, 0.0), axis=1, keepdims=True)
    b_row = col2row(b_col)
    i_row = col2row(i_col)
    m_prev = m_ref[...]
    dmat = jnp.where(tril, b_col - b_row + i_row, -jnp.inf)
    from_state = b_col + m_prev
    m_t = jnp.maximum(from_state, jnp.max(dmat, axis=1, keepdims=True))

    q = q_ref[...]
    k = k_ref[...] * (dk ** -0.5)
    qb = q.astype(BF16)
    vb = v_ref[...].astype(BF16)
    w = jnp.exp(dmat - m_t) * _nt(qb, k.astype(BF16))
    s_state = jnp.exp(from_state - m_t)
    c_old = c_ref[...]
    n_old = n_ref[...]
    num = s_state * _nn(qb, c_old.astype(BF16)) + _nn(w.astype(BF16), vb)
    den = s_state * jnp.sum(q * n_old, axis=1, keepdims=True) + jnp.sum(w, axis=1, keepdims=True)
    h = num / jnp.maximum(jnp.abs(den), jnp.exp(-m_t))

    m_new = m_t[L - 1:L, :]
    b_last = b_col[L - 1:L, :]
    w_end = jnp.exp(b_last - b_col + i_col - m_new)
    s_end = jnp.exp(b_last + m_prev - m_new)
    kw = k * w_end
    c_ref[...] = s_end * c_old + _tn(kw.astype(BF16), vb)
    n_ref[...] = s_end * n_old + jnp.sum(kw, axis=0, keepdims=True)
    m_ref[...] = m_new

    hn = h * lax.rsqrt(jnp.mean(h * h, axis=1, keepdims=True) + EPS) * nrm_ref[...]
    h_ref[...] = (hn * _sigmoid(og_ref[...])).astype(h_ref.dtype)


def mlstm_mixer(proj, small, gate_bias, norm, c0, n0, m0, *, B, T, row_off, cols, H, dk, dv, L):
    nC = T // L
    rb = row_off // L
    cq, ck, cv, co = cols

    def rows(b, h, c):
        return rb + b * nC + c

    in_specs = [
        pl.BlockSpec((L, dk), lambda b, h, c: (rows(b, h, c), cq // dk + h)),
        pl.BlockSpec((L, dk), lambda b, h, c: (rows(b, h, c), ck // dk + h)),
        pl.BlockSpec((L, dv), lambda b, h, c: (rows(b, h, c), cv // dv + h)),
        pl.BlockSpec((L, dv), lambda b, h, c: (rows(b, h, c), co // dv + h)),
        pl.BlockSpec((L, LANES), lambda b, h, c: (rows(b, h, c), 0)),
        pl.BlockSpec((1, LANES), lambda b, h, c: (0, 0)),
        pl.BlockSpec((None, 1, dv), lambda b, h, c: (h, 0, 0)),
        pl.BlockSpec((None, None, dk, dv), lambda b, h, c: (b, h, 0, 0)),
        pl.BlockSpec((None, None, 1, dk), lambda b, h, c: (b, h, 0, 0)),
        pl.BlockSpec((None, None, 1, 1), lambda b, h, c: (b, h, 0, 0)),
    ]
    out_specs = [
        pl.BlockSpec((L, dv), lambda b, h, c: (b * nC + c, h)),
        pl.BlockSpec((None, None, dk, dv), lambda b, h, c: (b, h, 0, 0)),
        pl.BlockSpec((None, None, 1, dk), lambda b, h, c: (b, h, 0, 0)),
        pl.BlockSpec((None, None, 1, 1), lambda b, h, c: (b, h, 0, 0)),
    ]
    out_shape = [
        jax.ShapeDtypeStruct((B * T, H * dv), BF16),
        jax.ShapeDtypeStruct((B, H, dk, dv), F32),
        jax.ShapeDtypeStruct((B, H, 1, dk), F32),
        jax.ShapeDtypeStruct((B, H, 1, 1), F32),
    ]
    h, c1, n1, m1 = pl.pallas_call(
        functools.partial(_mlstm_kernel, L=L, dk=dk, n_heads=H),
        grid=(B, H, nC),
        in_specs=in_specs, out_specs=out_specs, out_shape=out_shape,
        compiler_params=_cparams(("parallel", "parallel", "arbitrary")),
        name="mlstm",
    )(proj, proj, proj, proj, small, gate_bias, norm.reshape(H, 1, dv), c0,
      n0.reshape(B, H, 1, dk), m0.reshape(B, H, 1, 1))
    return h, c1, n1.reshape(B, H, dk), m1.reshape(B, H)


def _scan_chunk(q, k, v, g, st, *, SB):
    CH = q.shape[0]
    bc = _cumsum_rows(g)
    b_last = bc[CH - 1:CH, :]
    vb = v.astype(BF16)
    o = _nt((q * jnp.exp(bc)).astype(BF16), st.astype(BF16))
    row = lax.broadcasted_iota(jnp.int32, bc.shape, 0)
    srow = lax.broadcasted_iota(jnp.int32, (SB, CH), 0)
    scol = lax.broadcasted_iota(jnp.int32, (SB, CH), 1)
    blocks = []
    for i in range(CH // SB):
        lo, hi = i * SB, (i + 1) * SB
        ref = bc[lo - 1:lo, :] if i > 0 else jnp.zeros_like(b_last)
        qi = q[lo:hi] * jnp.exp(bc[lo:hi] - ref)
        ki = k * jnp.exp(jnp.where(row < hi, ref - bc, 0.0))
        a = _nt(qi.astype(BF16), ki.astype(BF16))
        blocks.append(jnp.where(scol <= srow + lo, a, 0.0))
    a_full = blocks[0] if len(blocks) == 1 else jnp.concatenate(blocks, axis=0)
    o = o + _nn(a_full.astype(BF16), vb)
    k_dec = k * jnp.exp(b_last - bc)
    st_new = st * jnp.exp(b_last) + _tn(vb, k_dec.astype(BF16))
    return o, st_new


def _head_out(o, nrm, gate_act):
    return o * lax.rsqrt(jnp.mean(o * o, axis=1, keepdims=True) + EPS) * nrm * gate_act


def _gla_kernel(q_ref, k_ref, v_ref, og_ref, gt_ref, w2_ref, bgk_ref, nrm_ref, s0_ref,
                o_ref, s_ref, st_scr, *, SB, dk, n_chunks):
    c = pl.program_id(2)

    @pl.when(c == 0)
    def _():
        st_scr[...] = s0_ref[...].T

    gk = _nn(gt_ref[...].astype(BF16), w2_ref[...]) + bgk_ref[...]
    g = _log_sigmoid(gk) / B_GATE_NORMALIZER
    o, st_new = _scan_chunk(q_ref[...] * (dk ** -0.5), k_ref[...], v_ref[...], g, st_scr[...], SB=SB)
    st_scr[...] = st_new
    og = og_ref[...]
    o_ref[...] = _head_out(o, nrm_ref[...], og * _sigmoid(og)).astype(o_ref.dtype)

    @pl.when(c == n_chunks - 1)
    def _():
        s_ref[...] = st_new.T


def gla_mixer(proj, small, w2pad, b_gk, norm, s0, *, B, T, row_off, cols, H, dk, dv, CH, SB):
    nC = T // CH
    rb = row_off // CH
    cq, ck, cv, co = cols

    def rows(b, h, c):
        return rb + b * nC + c

    in_specs = [
        pl.BlockSpec((CH, dk), lambda b, h, c: (rows(b, h, c), cq // dk + h)),
        pl.BlockSpec((CH, dk), lambda b, h, c: (rows(b, h, c), ck // dk + h)),
        pl.BlockSpec((CH, dv), lambda b, h, c: (rows(b, h, c), cv // dv + h)),
        pl.BlockSpec((CH, dv), lambda b, h, c: (rows(b, h, c), co // dv + h)),
        pl.BlockSpec((CH, LANES), lambda b, h, c: (rows(b, h, c), 0)),
        pl.BlockSpec((None, LANES, dk), lambda b, h, c: (h, 0, 0)),
        pl.BlockSpec((None, 1, dk), lambda b, h, c: (h, 0, 0)),
        pl.BlockSpec((None, 1, dv), lambda b, h, c: (h, 0, 0)),
        pl.BlockSpec((None, None, dk, dv), lambda b, h, c: (b, h, 0, 0)),
    ]
    out_specs = [
        pl.BlockSpec((CH, dv), lambda b, h, c: (b * nC + c, h)),
        pl.BlockSpec((None, None, dk, dv), lambda b, h, c: (b, h, 0, 0)),
    ]
    out_shape = [jax.ShapeDtypeStruct((B * T, H * dv), BF16), jax.ShapeDtypeStruct((B, H, dk, dv), F32)]
    return pl.pallas_call(
        functools.partial(_gla_kernel, SB=SB, dk=dk, n_chunks=nC),
        grid=(B, H, nC),
        in_specs=in_specs, out_specs=out_specs, out_shape=out_shape,
        scratch_shapes=[pltpu.VMEM((dv, dk), F32)],
        compiler_params=_cparams(("parallel", "parallel", "arbitrary")),
        name="gla",
    )(proj, proj, proj, proj, small, w2pad, b_gk.reshape(H, 1, dk), norm.reshape(H, 1, dv), s0)


def _hgrn_kernel(q_ref, f_ref, i_ref, og_ref, lg_ref, nrm_ref, s0_ref, o_ref, s_ref, st_scr,
                 *, SB, HB, dk, layer, n_chunks):
    c = pl.program_id(2)
    lg = lg_ref[...]
    e = jnp.exp(lg - jnp.max(lg, axis=0, keepdims=True))
    sm = e / jnp.sum(e, axis=0, keepdims=True)
    lb_all = jnp.sum(sm[1:layer + 1], axis=0, keepdims=True) if layer > 0 else jnp.zeros_like(sm[0:1])
    @pl.when(c == 0)
    def _():
        for hb in range(HB):
            st_scr[hb] = s0_ref[hb].T

    for hb in range(HB):
        sl = slice(hb * dk, (hb + 1) * dk)
        lb = lb_all[:, sl]
        z = f_ref[:, sl]
        g = jnp.log(lb + (1.0 - lb) * _sigmoid(z))
        key = (1.0 - lb) * _sigmoid(-z)
        cq = q_ref[:, sl]
        o, st_new = _scan_chunk(cq * _sigmoid(cq), key, i_ref[:, sl], g, st_scr[hb], SB=SB)
        st_scr[hb] = st_new
        og = og_ref[:, sl]
        o_ref[:, sl] = _head_out(o, nrm_ref[hb], og * _sigmoid(og)).astype(o_ref.dtype)

    @pl.when(c == n_chunks - 1)
    def _():
        for hb in range(HB):
            s_ref[hb] = st_scr[hb].T


def hgrn_mixer(proj, lb_logits, norm, s0, *, B, T, row_off, cols, H, dk, dv, CH, SB, HB, layer):
    assert dk == dv
    nC = T // CH
    rb = row_off // CH
    cq, cf, ci, co = cols
    depth = lb_logits.shape[0]
    wb = HB * dk

    def col(c0):
        return pl.BlockSpec((CH, wb), lambda b, h, c: (rb + b * nC + c, c0 // wb + h))

    in_specs = [
        col(cq), col(cf), col(ci), col(co),
        pl.BlockSpec((depth, wb), lambda b, h, c: (0, h)),
        pl.BlockSpec((HB, 1, dv), lambda b, h, c: (h, 0, 0)),
        pl.BlockSpec((None, HB, dk, dv), lambda b, h, c: (b, h, 0, 0)),
    ]
    out_specs = [
        pl.BlockSpec((CH, wb), lambda b, h, c: (b * nC + c, h)),
        pl.BlockSpec((None, HB, dk, dv), lambda b, h, c: (b, h, 0, 0)),
    ]
    out_shape = [jax.ShapeDtypeStruct((B * T, H * dv), BF16), jax.ShapeDtypeStruct((B, H, dk, dv), F32)]
    return pl.pallas_call(
        functools.partial(_hgrn_kernel, SB=SB, HB=HB, dk=dk, layer=layer, n_chunks=nC),
        grid=(B, H // HB, nC),
        in_specs=in_specs, out_specs=out_specs, out_shape=out_shape,
        scratch_shapes=[pltpu.VMEM((HB, dv, dk), F32)],
        compiler_params=_cparams(("parallel", "parallel", "arbitrary")),
        name="hgrn",
    )(proj, proj, proj, proj, lb_logits, norm.reshape(H, 1, dv), s0)


def _fox_cumsum_kernel(x_ref, b_ref, lf_ref, c_ref):
    lf = _log_sigmoid(x_ref[...] + b_ref[...])
    lf_ref[...] = lf
    c_ref[...] = _cumsum_lanes(lf, 1)


def fox_cumsum(df_t, bias):
    B, H, T = df_t.shape
    spec = pl.BlockSpec((None, H, T), lambda b: (b, 0, 0))
    return pl.pallas_call(
        _fox_cumsum_kernel,
        grid=(B,),
        in_specs=[spec, pl.BlockSpec((H, 1), lambda b: (0, 0))],
        out_specs=[spec, spec],
        out_shape=[jax.ShapeDtypeStruct((B, H, T), F32)] * 2,
        compiler_params=_cparams(("parallel",)),
        name="fox_cumsum",
    )(df_t, bias.reshape(H, 1))


def _fox_prompt_kernel(q_ref, k_ref, v_ref, og_ref, cr_ref, o_ref, *, TQ, T, dh):
    kb = k_ref[...].astype(BF16)
    vb = v_ref[...].astype(BF16)
    c_row = cr_ref[...]
    for i in range(T // TQ):
        lo, hi = i * TQ, (i + 1) * TQ
        q = (q_ref[lo:hi, :] * (dh ** -0.5)).astype(BF16)
        s = _nt(q, kb[:hi]) - c_row[:, :hi]
        qpos = lo + lax.broadcasted_iota(jnp.int32, (TQ, hi), 0)
        kpos = lax.broadcasted_iota(jnp.int32, (TQ, hi), 1)
        s = jnp.where(kpos <= qpos, s, NEG)
        p = jnp.exp(s - jnp.max(s, axis=1, keepdims=True))
        l = jnp.sum(p, axis=1, keepdims=True)
        o = _nn(p.astype(BF16), vb[:hi]) / l
        o_ref[lo:hi, :] = (o * _sigmoid(og_ref[lo:hi, :])).astype(o_ref.dtype)


def fox_prompt(proj, c_row, *, B, T, row_off, cols, H, dh, TQ):
    cq, ck, cv, co = cols
    rb = row_off // T

    def col(c0):
        return pl.BlockSpec((T, dh), lambda b, h: (rb + b, c0 // dh + h))

    in_specs = [col(cq), col(ck), col(cv), col(co),
                pl.BlockSpec((None, None, 1, T), lambda b, h: (b, h, 0, 0))]
    return pl.pallas_call(
        functools.partial(_fox_prompt_kernel, TQ=TQ, T=T, dh=dh),
        grid=(B, H),
        in_specs=in_specs,
        out_specs=pl.BlockSpec((T, dh), lambda b, h: (b, h)),
        out_shape=jax.ShapeDtypeStruct((B * T, H * dh), BF16),
        compiler_params=_cparams(("parallel", "parallel")),
        name="fox_prompt",
    )(proj, proj, proj, proj, c_row)


def _fox_sample_kernel(pt_ref, q_ref, kn_ref, vn_ref, og_ref, df_ref, fb_ref, *rest, PP, T, H, dh, n_steps):
    k_refs = rest[0:PP]
    v_refs = rest[PP:2 * PP]
    f_refs = rest[2 * PP:3 * PP]
    o_ref, lf_ref = rest[3 * PP], rest[3 * PP + 1]
    qall, acc, m_scr, l_scr, carry, crow, hmask = rest[3 * PP + 2:]
    s_id = pl.program_id(1)
    R = T * H
    W = PAGE * H

    @pl.when(s_id == 0)
    def _():
        qall[...] = (q_ref[...].reshape(R, dh) * (dh ** -0.5)).astype(BF16)
        acc[...] = jnp.zeros_like(acc)
        m_scr[...] = jnp.full_like(m_scr, NEG)
        l_scr[...] = jnp.zeros_like(l_scr)
        carry[...] = jnp.zeros_like(carry)
        lf_new = _log_sigmoid(df_ref[...] + fb_ref[...])
        lf_ref[...] = lf_new
        crow[...] = _cumsum_lanes(lf_new, H)
        rr = lax.broadcasted_iota(jnp.int32, (R, W), 0)
        ll = lax.broadcasted_iota(jnp.int32, (R, W), 1)
        hmask[...] = jnp.where((rr % H) == (ll % H), 0.0, NEG)

    lf = jnp.concatenate([f_refs[p][...] for p in range(PP)], axis=0)
    lane = lax.broadcasted_iota(jnp.int32, (PP, W), 1)
    suf = lf
    tot = lf
    s = H
    while s < W:
        suf = suf + jnp.where(lane + s < W, pltpu.roll(suf, W - s, 1), 0.0)
        tot = tot + pltpu.roll(tot, s, 1)
        s *= 2
    tot_cum = _cumsum_rows(tot)
    later = carry[...]
    d_all = suf - lf + (tot_cum - tot) + later
    carry[...] = later + tot_cum[PP - 1:PP, :]

    qa = qall[...]
    hm = hmask[...]
    m_old = m_scr[...]
    l_old = l_scr[...]
    a_old = acc[...]
    for p in range(PP):
        kb = k_refs[p][...].astype(BF16)
        vb = v_refs[p][...].astype(BF16)
        st = _nt(qa, kb) + (hm + d_all[p:p + 1, :])
        m_new = jnp.maximum(m_old, jnp.max(st, axis=1, keepdims=True))
        alpha = jnp.exp(m_old - m_new)
        pm = jnp.exp(st - m_new)
        l_old = alpha * l_old + jnp.sum(pm, axis=1, keepdims=True)
        a_old = alpha * a_old + _nn(pm.astype(BF16), vb)
        m_old = m_new
    m_scr[...] = m_old
    l_scr[...] = l_old
    acc[...] = a_old

    @pl.when(s_id == n_steps - 1)
    def _():
        knb = kn_ref[...].reshape(R, dh).astype(BF16)
        vnb = vn_ref[...].reshape(R, dh).astype(BF16)
        r2 = lax.broadcasted_iota(jnp.int32, (R, R), 0)
        l2 = lax.broadcasted_iota(jnp.int32, (R, R), 1)
        ok = ((r2 % H) == (l2 % H)) & (l2 // H <= r2 // H)
        sn = jnp.where(ok, _nt(qa, knb) - crow[...], NEG)
        m_fin = jnp.maximum(m_old, jnp.max(sn, axis=1, keepdims=True))
        a2 = jnp.exp(m_old - m_fin)
        pn = jnp.exp(sn - m_fin)
        l_fin = a2 * l_old + jnp.sum(pn, axis=1, keepdims=True)
        out = (a2 * a_old + _nn(pn.astype(BF16), vnb)) / l_fin
        out = out * _sigmoid(og_ref[...].reshape(R, dh))
        o_ref[...] = out.reshape(T, H, dh).astype(o_ref.dtype)


def fox_sample(proj3, df_flat, fb_flat, cache_k, cache_v, cache_f, page_table, *, B, T, page_off, cols, H, dh, PP):
    R = T * H
    W = PAGE * H
    n_pages = page_table.shape[1]
    n_steps = n_pages // PP
    cq, ck, cv, co = cols

    def rowspec(c0):
        return pl.BlockSpec((T, H, dh), lambda b, s, pt: (b, c0 // H, 0))

    def page(p):
        return lambda b, s, pt: (page_off + pt[b, n_pages - 1 - (s * PP + p)], 0, 0)

    in_specs = [rowspec(cq), rowspec(ck), rowspec(cv), rowspec(co),
                pl.BlockSpec((None, 1, R), lambda b, s, pt: (b, 0, 0)),
                pl.BlockSpec((1, R), lambda b, s, pt: (0, 0))]
    in_specs += [pl.BlockSpec((None, W, dh), page(p)) for p in range(PP)]
    in_specs += [pl.BlockSpec((None, W, dh), page(p)) for p in range(PP)]
    in_specs += [pl.BlockSpec((None, 1, W), page(p)) for p in range(PP)]
    grid_spec = pltpu.PrefetchScalarGridSpec(
        num_scalar_prefetch=1,
        grid=(B, n_steps),
        in_specs=in_specs,
        out_specs=[pl.BlockSpec((T, H, dh), lambda b, s, pt: (b, 0, 0)),
                   pl.BlockSpec((None, 1, R), lambda b, s, pt: (b, 0, 0))],
        scratch_shapes=[
            pltpu.VMEM((R, dh), BF16),
            pltpu.VMEM((R, dh), F32),
            pltpu.VMEM((R, 1), F32),
            pltpu.VMEM((R, 1), F32),
            pltpu.VMEM((1, W), F32),
            pltpu.VMEM((1, R), F32),
            pltpu.VMEM((R, W), F32),
        ],
    )
    return pl.pallas_call(
        functools.partial(_fox_sample_kernel, PP=PP, T=T, H=H, dh=dh, n_steps=n_steps),
        grid_spec=grid_spec,
        out_shape=[jax.ShapeDtypeStruct((B * T, H, dh), BF16), jax.ShapeDtypeStruct((B, 1, R), F32)],
        compiler_params=_cparams(("parallel", "arbitrary")),
        name="fox_sample",
    )(page_table, proj3, proj3, proj3, proj3, df_flat, fb_flat,
      *([cache_k] * PP), *([cache_v] * PP), *([cache_f] * PP))


def _pad_lanes(w):
    return jnp.pad(w, ((0, 0), (0, LANES - w.shape[1])))


def _row_tile(m, cap):
    best = None
    for t in range(BF16_ROWS, min(m, cap) + 1, BF16_ROWS):
        if m % t == 0:
            best = t
    assert best is not None, (m, cap)
    return best


def _col_tile(n, cap):
    t = cap
    while n % t:
        t //= 2
    return t


def kernel(x_prompt, x_sample, state_mlstm_C, state_mlstm_n, state_mlstm_m, state_gla_S, state_hgrn_S, cache_fox_k, cache_fox_v, cache_fox_logf, page_table, norm_mix_pre, norm_mix_post, norm_ffn_pre, norm_ffn_post, ffn_w_gate, ffn_w_up, ffn_w_down, w_in_even, w_out_even, mlstm_b_i, mlstm_b_f, mlstm_norm, gla_w_gk2, gla_b_gk, gla_norm, w_in_odd, w_out_odd, hgrn_lb_logits, hgrn_norm, fox_b_f):
    bp, tp, D = x_prompt.shape
    bs, ts, _ = x_sample.shape
    mp, ms = bp * tp, bs * ts
    M = mp + ms
    depth = norm_mix_pre.shape[0]
    a_heads, a_dk, a_dv = state_mlstm_C.shape[2:]
    b_heads, b_dk, b_dv = state_gla_S.shape[2:]
    c_heads, c_dk, c_dv = state_hgrn_S.shape[2:]
    d_heads = fox_b_f.shape[1]
    half = D // 2
    d_dh = half // d_heads
    ff = ffn_w_gate.shape[3]
    n_phys = cache_fox_k.shape[1]

    tm = _row_tile(M, 1408)
    tr = _row_tile(M, 256)
    wg_all = ffn_w_gate.astype(BF16)
    wu_all = ffn_w_up.astype(BF16)
    wd_all = ffn_w_down.astype(BF16)

    def ffn(h, layer, j):
        u = ffn_up(h, wg_all, wu_all, (layer, j), tm=tm, tn=_col_tile(ff, 256))
        return matmul([(u, wd_all, (layer, j), 0)], out_dtype=F32, tm=tm, tn=_col_tile(D, 512), nk=2)

    groups = ((bp, tp, 0), (bs, ts, mp))
    outs = {k: ([], []) for k in ('c', 'n', 'm', 'g', 'h', 'k', 'v', 'f')}

    x = jnp.concatenate([x_prompt.reshape(mp, D), x_sample.reshape(ms, D)], axis=0)
    h = prenorm(x, norm_ffn_pre[0, 0], tr=tr)
    for layer in range(depth):
        y = ffn(h, layer, 0)
        x, h = resid_norm(x, y, norm_ffn_post[layer, 0], norm_mix_pre[layer], scale=0.5, tr=tr)
        if layer % 2 == 0:
            e = layer // 2
            w = w_in_even[e]
            n_a = a_heads * (2 * a_dk + 2 * a_dv)
            n_b = b_heads * (2 * b_dk + 2 * b_dv)
            o_b = n_a + 2 * a_heads
            w_main = jnp.concatenate([w[:, :n_a], w[:, o_b:o_b + n_b]], axis=1).astype(BF16)
            w_small = _pad_lanes(jnp.concatenate([w[:, n_a:o_b], w[:, o_b + n_b:]], axis=1)).astype(BF16)
            proj = matmul([(h, w_main, (), 0)], out_dtype=F32, tm=tm, tn=_col_tile(n_a + n_b, 512))
            small = matmul([(h, w_small, (), 0)], out_dtype=F32, tm=tm, tn=LANES)
            gate_bias = _pad_lanes(jnp.concatenate([mlstm_b_i[e], mlstm_b_f[e]])[None, :])
            a_cols = (0, a_heads * a_dk, 2 * a_heads * a_dk, 2 * a_heads * a_dk + a_heads * a_dv)
            w2 = gla_w_gk2[e]
            w2pad = jnp.zeros((LANES, b_heads * b_dk), F32).at[2 * a_heads:2 * a_heads + w2.shape[0]].set(w2)
            w2pad = w2pad.reshape(LANES, b_heads, b_dk).transpose(1, 0, 2).astype(BF16)
            b_cols = (n_a, n_a + b_heads * b_dk, n_a + 2 * b_heads * b_dk, n_a + 2 * b_heads * b_dk + b_heads * b_dv)
            ha, hb = [], []
            for gi, (B, T, r0) in enumerate(groups):
                if gi == 0:
                    c0, n0, m0, s0 = (jnp.zeros((B,) + s.shape[2:], F32)
                                      for s in (state_mlstm_C, state_mlstm_n, state_mlstm_m, state_gla_S))
                else:
                    c0, n0, m0, s0 = state_mlstm_C[e], state_mlstm_n[e], state_mlstm_m[e], state_gla_S[e]
                CH = min(T, 128)
                ha_g, c1, n1, m1 = mlstm_mixer(proj, small, gate_bias, mlstm_norm[e], c0, n0, m0,
                                               B=B, T=T, row_off=r0, cols=a_cols,
                                               H=a_heads, dk=a_dk, dv=a_dv, L=CH)
                hb_g, s1 = gla_mixer(proj, small, w2pad, gla_b_gk[e], gla_norm[e], s0,
                                     B=B, T=T, row_off=r0, cols=b_cols,
                                     H=b_heads, dk=b_dk, dv=b_dv, CH=CH, SB=min(CH, 32))
                ha.append(ha_g); hb.append(hb_g)
                outs['c'][gi].append(c1); outs['n'][gi].append(n1)
                outs['m'][gi].append(m1); outs['g'][gi].append(s1)
            wo = w_out_even.astype(BF16)
            mix = matmul([(jnp.concatenate(ha, axis=0), wo, (e,), 0), (jnp.concatenate(hb, axis=0), wo, (e,), 1)],
                         out_dtype=F32, tm=tm, tn=_col_tile(D, 512))
        else:
            o = layer // 2
            w = w_in_odd[o]
            n_main = 4 * c_heads * c_dk + 4 * half
            w_main = w[:, :n_main].astype(BF16)
            w_small = _pad_lanes(w[:, n_main:]).astype(BF16)
            proj = matmul([(h, w_main, (), 0)], out_dtype=F32, tm=tm, tn=_col_tile(n_main, 512))
            small = matmul([(h, w_small, (), 0)], out_dtype=F32, tm=tm, tn=LANES)
            cw = c_heads * c_dk
            dq = 4 * cw
            d_cols = (dq, dq + half, dq + 2 * half, dq + 3 * half)
            hc, hd = [], []
            for gi, (B, T, r0) in enumerate(groups):
                s0 = jnp.zeros((B,) + state_hgrn_S.shape[2:], F32) if gi == 0 else state_hgrn_S[o]
                CH = min(T, 128)
                hc_g, s1 = hgrn_mixer(proj, hgrn_lb_logits, hgrn_norm[o], s0,
                                      B=B, T=T, row_off=r0, cols=(0, cw, 2 * cw, 3 * cw),
                                      H=c_heads, dk=c_dk, dv=c_dv, CH=CH, SB=min(CH, 32), HB=4, layer=layer)
                df = small[r0:r0 + B * T, :d_heads]
                if gi == 0:
                    lf_t, c_t = fox_cumsum(df.reshape(B, T, d_heads).transpose(0, 2, 1), fox_b_f[o])
                    logf = lf_t.transpose(0, 2, 1)
                    hd_g = fox_prompt(proj, c_t.reshape(B, d_heads, 1, T), B=B, T=T, row_off=r0,
                                      cols=d_cols, H=d_heads, dh=d_dh, TQ=min(T, 256))
                else:
                    n_cb = proj.shape[1] // d_dh
                    proj3 = proj[r0:r0 + B * T].reshape(B * T, n_cb, d_dh)
                    n_odd = cache_fox_k.shape[0]
                    hd3, lf = fox_sample(
                        proj3, df.reshape(B, 1, T * d_heads), jnp.tile(fox_b_f[o], T)[None, :],
                        cache_fox_k.reshape(n_odd * n_phys, PAGE * d_heads, d_dh),
                        cache_fox_v.reshape(n_odd * n_phys, PAGE * d_heads, d_dh),
                        cache_fox_logf.reshape(n_odd * n_phys, 1, PAGE * d_heads),
                        page_table, B=B, T=T, page_off=o * n_phys,
                        cols=tuple(c // d_dh for c in d_cols), H=d_heads, dh=d_dh, PP=8)
                    hd_g = hd3.reshape(B * T, half)
                    logf = lf.reshape(B, T, d_heads)
                hc.append(hc_g); hd.append(hd_g)
                outs['h'][gi].append(s1)
                kv = proj[r0:r0 + B * T]
                outs['k'][gi].append(kv[:, d_cols[1]:d_cols[1] + half].reshape(B, T, d_heads, d_dh))
                outs['v'][gi].append(kv[:, d_cols[2]:d_cols[2] + half].reshape(B, T, d_heads, d_dh))
                outs['f'][gi].append(logf)
            wo = w_out_odd.astype(BF16)
            mix = matmul([(jnp.concatenate(hc, axis=0), wo, (o,), 0), (jnp.concatenate(hd, axis=0), wo, (o,), 1)],
                         out_dtype=F32, tm=tm, tn=_col_tile(D, 512))
        x, h = resid_norm(x, mix, norm_mix_post[layer], norm_ffn_pre[layer, 1], scale=1.0, tr=tr)
        y = ffn(h, layer, 1)
        g_next = norm_ffn_pre[layer + 1, 0] if layer + 1 < depth else norm_ffn_pre[layer, 0]
        x, h = resid_norm(x, y, norm_ffn_post[layer, 1], g_next, scale=0.5, tr=tr)

    names = ('c', 'n', 'm', 'g', 'h', 'k', 'v', 'f')
    res_p = tuple(jnp.stack(outs[k][0]) for k in names)
    res_s = tuple(jnp.stack(outs[k][1]) for k in names)
    return (x[:mp].reshape(bp, tp, D), x[mp:].reshape(bs, ts, D)) + res_p + res_s
```

```python
import functools
import math

import jax
import jax.numpy as jnp
from jax import lax
from jax.experimental import pallas as pl
from jax.experimental.pallas import tpu as pltpu

F32 = jnp.float32
BF16 = jnp.bfloat16
EPS = 1e-6
NEG = -1e30
LANES = 128
BF16_ROWS = 16
VMEM_LIMIT = 56 * 1024 * 1024

B_GATE_NORMALIZER = 16.0
PAGE = 128


def _cparams(sem):
    return pltpu.CompilerParams(dimension_semantics=sem, vmem_limit_bytes=VMEM_LIMIT)


def _log_sigmoid(x):
    return jnp.minimum(x, 0.0) - jnp.log1p(jnp.exp(-jnp.abs(x)))


def _sigmoid(x):
    return jax.nn.sigmoid(x)


def _nt(a, b):
    return lax.dot_general(a, b, (((1,), (1,)), ((), ())), preferred_element_type=F32)


def _tn(a, b):
    return lax.dot_general(a, b, (((0,), (0,)), ((), ())), preferred_element_type=F32)


def _nn(a, b):
    return jnp.dot(a, b, preferred_element_type=F32)


def _cumsum_rows(x):
    n = x.shape[0]
    row = lax.broadcasted_iota(jnp.int32, x.shape, 0)
    s = 1
    while s < n:
        x = x + jnp.where(row >= s, pltpu.roll(x, s, 0), 0.0)
        s *= 2
    return x


def _cumsum_lanes(x, step):
    n = x.shape[1]
    lane = lax.broadcasted_iota(jnp.int32, x.shape, 1)
    s = step
    while s < n:
        x = x + jnp.where(lane >= s, pltpu.roll(x, s, 1), 0.0)
        s *= 2
    return x


def _prenorm_kernel(x_ref, g_ref, o_ref):
    x = x_ref[...]
    r = lax.rsqrt(jnp.mean(x * x, axis=-1, keepdims=True) + EPS)
    o_ref[...] = (x * r * g_ref[...]).astype(o_ref.dtype)


def prenorm(x, g, *, tr):
    m, d = x.shape
    return pl.pallas_call(
        _prenorm_kernel,
        grid=(m // tr,),
        in_specs=[pl.BlockSpec((tr, d), lambda i: (i, 0)), pl.BlockSpec((1, d), lambda i: (0, 0))],
        out_specs=pl.BlockSpec((tr, d), lambda i: (i, 0)),
        out_shape=jax.ShapeDtypeStruct((m, d), BF16),
        compiler_params=_cparams(("parallel",)),
        name="prenorm",
    )(x, g.reshape(1, d))


def _resid_kernel(x_ref, y_ref, gp_ref, gn_ref, xo_ref, ho_ref, *, scale):
    y = y_ref[...]
    r = lax.rsqrt(jnp.mean(y * y, axis=-1, keepdims=True) + EPS)
    xn = x_ref[...] + scale * (y * r * gp_ref[...])
    xo_ref[...] = xn
    r2 = lax.rsqrt(jnp.mean(xn * xn, axis=-1, keepdims=True) + EPS)
    ho_ref[...] = (xn * r2 * gn_ref[...]).astype(ho_ref.dtype)


def resid_norm(x, y, g_post, g_next, *, scale, tr):
    m, d = x.shape
    row = pl.BlockSpec((tr, d), lambda i: (i, 0))
    vec = pl.BlockSpec((1, d), lambda i: (0, 0))
    return pl.pallas_call(
        functools.partial(_resid_kernel, scale=scale),
        grid=(m // tr,),
        in_specs=[row, row, vec, vec],
        out_specs=[row, row],
        out_shape=[jax.ShapeDtypeStruct((m, d), F32), jax.ShapeDtypeStruct((m, d), BF16)],
        compiler_params=_cparams(("parallel",)),
        name="resid_norm",
    )(x, y, g_post.reshape(1, d), g_next.reshape(1, d))


def _resid_split_kernel(x_ref, y_ref, gp_ref, xa_ref, xb_ref, *, scale, n_a):
    i = pl.program_id(0)
    y = y_ref[...]
    r = lax.rsqrt(jnp.mean(y * y, axis=-1, keepdims=True) + EPS)
    xn = x_ref[...] + scale * (y * r * gp_ref[...])

    @pl.when(i < n_a)
    def _():
        xa_ref[...] = xn

    @pl.when(i >= n_a)
    def _():
        xb_ref[...] = xn


def resid_split(x, y, g_post, *, scale, rows_a, tr):
    m, d = x.shape
    n_a = rows_a // tr
    row = pl.BlockSpec((tr, d), lambda i: (i, 0))
    return pl.pallas_call(
        functools.partial(_resid_split_kernel, scale=scale, n_a=n_a),
        grid=(m // tr,),
        in_specs=[row, row, pl.BlockSpec((1, d), lambda i: (0, 0))],
        out_specs=[pl.BlockSpec((tr, d), lambda i: (jnp.minimum(i, n_a - 1), 0)),
                   pl.BlockSpec((tr, d), lambda i: (jnp.maximum(i - n_a, 0), 0))],
        out_shape=[jax.ShapeDtypeStruct((rows_a, d), F32), jax.ShapeDtypeStruct((m - rows_a, d), F32)],
        compiler_params=_cparams(("arbitrary",)),
        name="resid_split",
    )(x, y, g_post.reshape(1, d))


def _split_heads_kernel(k_ref, v_ref, ko_ref, vo_ref, *, H, dh):
    for h in range(H):
        ko_ref[:, h, :] = k_ref[:, h * dh:(h + 1) * dh]
        vo_ref[:, h, :] = v_ref[:, h * dh:(h + 1) * dh]


def split_heads(proj, *, rows, row_off, ck, cv, H, dh, tr):
    w = H * dh
    out = jax.ShapeDtypeStruct((rows, H, dh), proj.dtype)
    ospec = pl.BlockSpec((tr, H, dh), lambda i: (i, 0, 0))
    return pl.pallas_call(
        functools.partial(_split_heads_kernel, H=H, dh=dh),
        grid=(rows // tr,),
        in_specs=[pl.BlockSpec((tr, w), lambda i: (row_off // tr + i, ck // w)),
                  pl.BlockSpec((tr, w), lambda i: (row_off // tr + i, cv // w))],
        out_specs=[ospec, ospec],
        out_shape=[out, out],
        compiler_params=_cparams(("parallel",)),
        name="split_heads",
    )(proj, proj)


def _mm_kernel(*refs, n_pairs, nk):
    o_ref = refs[-1]
    acc = _nn(refs[0][...], refs[1][...])
    for p in range(1, n_pairs):
        acc = acc + _nn(refs[2 * p][...], refs[2 * p + 1][...])
    if nk == 1:
        o_ref[...] = acc.astype(o_ref.dtype)
    else:
        k = pl.program_id(2)

        @pl.when(k == 0)
        def _():
            o_ref[...] = acc

        @pl.when(k > 0)
        def _():
            o_ref[...] += acc


def matmul(pairs, *, out_dtype, tm, tn, nk=1, n=None):
    m = pairs[0][0].shape[0]
    n = pairs[0][1].shape[-1] if n is None else n
    assert nk == 1 or out_dtype == F32
    in_specs, args = [], []
    for x, w, lead, kb0 in pairs:
        tk = x.shape[1] // nk
        in_specs.append(pl.BlockSpec((tm, tk), lambda i, j, k: (i, k)))
        in_specs.append(pl.BlockSpec((None,) * len(lead) + (tk, tn),
                                     lambda i, j, k, lead=lead, kb0=kb0, nk=nk: lead + (kb0 * nk + k, j)))
        args += [x, w]
    return pl.pallas_call(
        functools.partial(_mm_kernel, n_pairs=len(pairs), nk=nk),
        grid=(m // tm, n // tn, nk),
        in_specs=in_specs,
        out_specs=pl.BlockSpec((tm, tn), lambda i, j, k: (i, j)),
        out_shape=jax.ShapeDtypeStruct((m, n), out_dtype),
        compiler_params=_cparams(("parallel", "parallel", "arbitrary")),
        name="matmul",
    )(*args)


def _ffn_up_kernel(h_ref, wg_ref, wu_ref, o_ref, wg_bf, wu_bf):
    @pl.when(pl.program_id(1) == 0)
    def _():
        wg_bf[...] = wg_ref[...].astype(BF16)
        wu_bf[...] = wu_ref[...].astype(BF16)

    h = h_ref[...]
    g = _nn(h, wg_bf[...])
    u = _nn(h, wu_bf[...])
    o_ref[...] = (g * _sigmoid(g) * u).astype(o_ref.dtype)


def ffn_up(h, wg, wu, lead, *, tm, tn):
    m, k = h.shape
    n = wg.shape[-1]
    wspec = pl.BlockSpec((None,) * len(lead) + (k, tn), lambda j, i: lead + (0, j))
    return pl.pallas_call(
        _ffn_up_kernel,
        grid=(n // tn, m // tm),
        in_specs=[pl.BlockSpec((tm, k), lambda j, i: (i, 0)), wspec, wspec],
        out_specs=pl.BlockSpec((tm, tn), lambda j, i: (i, j)),
        out_shape=jax.ShapeDtypeStruct((m, n), BF16),
        scratch_shapes=[pltpu.VMEM((k, tn), BF16), pltpu.VMEM((k, tn), BF16)],
        compiler_params=_cparams(("parallel", "arbitrary")),
        name="ffn_up",
    )(h, wg, wu)


def _mlstm_kernel(q_ref, k_ref, v_ref, og_ref, gt_ref, gb_ref, nrm_ref, c0_ref, n0_ref, m0_ref,
                  h_ref, c_ref, n_ref, m_ref, *, L, HB, dk, dv, n_heads):
    @pl.when(pl.program_id(2) == 0)
    def _():
        c_ref[...] = c0_ref[...]
        n_ref[...] = n0_ref[...]
        m_ref[...] = m0_ref[...]

    gates = gt_ref[...] + gb_ref[...]
    lane = lax.broadcasted_iota(jnp.int32, gates.shape, 1)
    r = lax.broadcasted_iota(jnp.int32, (L, L), 0)
    s = lax.broadcasted_iota(jnp.int32, (L, L), 1)
    tril = r >= s
    eye = r == s

    def col2row(col):
        return jnp.sum(jnp.where(eye, col, 0.0), axis=0, keepdims=True)

    for hb in range(HB):
        hh = pl.program_id(1) * HB + hb
        ks = slice(hb * dk, (hb + 1) * dk)
        vs = slice(hb * dv, (hb + 1) * dv)
        i_col = jnp.sum(jnp.where(lane == hh, gates, 0.0), axis=1, keepdims=True)
        f_col = jnp.sum(jnp.where(lane == hh + n_heads, gates, 0.0), axis=1, keepdims=True)
        logf = _log_sigmoid(f_col)
        logf_row = col2row(logf)
        b_col = jnp.sum(jnp.where(tril, logf_row, 0.0), axis=1, keepdims=True)
        b_row = col2row(b_col)
        i_row = col2row(i_col)
        m_prev = m_ref[hb]
        dmat = jnp.where(tril, b_col - b_row + i_row, -jnp.inf)
        from_state = b_col + m_prev
        m_t = jnp.maximum(from_state, jnp.max(dmat, axis=1, keepdims=True))

        q = q_ref[:, ks]
        k = k_ref[:, ks] * (dk ** -0.5)
        qb = q.astype(BF16)
        vb = v_ref[:, vs].astype(BF16)
        w = jnp.exp(dmat - m_t) * _nt(qb, k.astype(BF16))
        s_state = jnp.exp(from_state - m_t)
        c_old = c_ref[hb]
        n_old = n_ref[hb]
        num = s_state * _nn(qb, c_old.astype(BF16)) + _nn(w.astype(BF16), vb)
        den = s_state * jnp.sum(q * n_old, axis=1, keepdims=True) + jnp.sum(w, axis=1, keepdims=True)
        h = num / jnp.maximum(jnp.abs(den), jnp.exp(-m_t))

        m_new = m_t[L - 1:L, :]
        b_last = b_col[L - 1:L, :]
        w_end = jnp.exp(b_last - b_col + i_col - m_new)
        s_end = jnp.exp(b_last + m_prev - m_new)
        kw = k * w_end
        c_ref[hb] = s_end * c_old + _tn(kw.astype(BF16), vb)
        n_ref[hb] = s_end * n_old + jnp.sum(kw, axis=0, keepdims=True)
        m_ref[hb] = m_new

        hn = h * lax.rsqrt(jnp.mean(h * h, axis=1, keepdims=True) + EPS) * nrm_ref[hb]
        h_ref[:, vs] = (hn * _sigmoid(og_ref[:, vs])).astype(h_ref.dtype)


def mlstm_mixer(proj, small, gate_bias, norm, c0, n0, m0, *, B, T, row_off, cols, H, dk, dv, L, HB):
    nC = T // L
    rb = row_off // L
    cq, ck, cv, co = cols
    wk, wv = HB * dk, HB * dv

    def rows(b, h, c):
        return rb + b * nC + c

    in_specs = [
        pl.BlockSpec((L, wk), lambda b, h, c: (rows(b, h, c), cq // wk + h)),
        pl.BlockSpec((L, wk), lambda b, h, c: (rows(b, h, c), ck // wk + h)),
        pl.BlockSpec((L, wv), lambda b, h, c: (rows(b, h, c), cv // wv + h)),
        pl.BlockSpec((L, wv), lambda b, h, c: (rows(b, h, c), co // wv + h)),
        pl.BlockSpec((L, LANES), lambda b, h, c: (rows(b, h, c), 0)),
        pl.BlockSpec((1, LANES), lambda b, h, c: (0, 0)),
        pl.BlockSpec((HB, 1, dv), lambda b, h, c: (h, 0, 0)),
        pl.BlockSpec((None, HB, dk, dv), lambda b, h, c: (b, h, 0, 0)),
        pl.BlockSpec((None, HB, 1, dk), lambda b, h, c: (b, h, 0, 0)),
        pl.BlockSpec((None, HB, 1, 1), lambda b, h, c: (b, h, 0, 0)),
    ]
    out_specs = [
        pl.BlockSpec((L, wv), lambda b, h, c: (b * nC + c, h)),
        pl.BlockSpec((None, HB, dk, dv), lambda b, h, c: (b, h, 0, 0)),
        pl.BlockSpec((None, HB, 1, dk), lambda b, h, c: (b, h, 0, 0)),
        pl.BlockSpec((None, HB, 1, 1), lambda b, h, c: (b, h, 0, 0)),
    ]
    out_shape = [
        jax.ShapeDtypeStruct((B * T, H * dv), BF16),
        jax.ShapeDtypeStruct((B, H, dk, dv), F32),
        jax.ShapeDtypeStruct((B, H, 1, dk), F32),
        jax.ShapeDtypeStruct((B, H, 1, 1), F32),
    ]
    h, c1, n1, m1 = pl.pallas_call(
        functools.partial(_mlstm_kernel, L=L, HB=HB, dk=dk, dv=dv, n_heads=H),
        grid=(B, H // HB, nC),
        in_specs=in_specs, out_specs=out_specs, out_shape=out_shape,
        compiler_params=_cparams(("parallel", "parallel", "arbitrary")),
        name="mlstm",
    )(proj, proj, proj, proj, small, gate_bias, norm.reshape(H, 1, dv), c0,
      n0.reshape(B, H, 1, dk), m0.reshape(B, H, 1, 1))
    return h, c1, n1.reshape(B, H, dk), m1.reshape(B, H)


def _scan_chunk(q, k, v, g, st, *, SB):
    CH = q.shape[0]
    bc = _cumsum_rows(g)
    b_last = bc[CH - 1:CH, :]
    vb = v.astype(BF16)
    o = _nt((q * jnp.exp(bc)).astype(BF16), st.astype(BF16))
    row = lax.broadcasted_iota(jnp.int32, bc.shape, 0)
    srow = lax.broadcasted_iota(jnp.int32, (SB, CH), 0)
    scol = lax.broadcasted_iota(jnp.int32, (SB, CH), 1)
    blocks = []
    for i in range(CH // SB):
        lo, hi = i * SB, (i + 1) * SB
        ref = bc[lo - 1:lo, :] if i > 0 else jnp.zeros_like(b_last)
        qi = q[lo:hi] * jnp.exp(bc[lo:hi] - ref)
        ki = k * jnp.exp(jnp.where(row < hi, ref - bc, 0.0))
        a = _nt(qi.astype(BF16), ki.astype(BF16))
        blocks.append(jnp.where(scol <= srow + lo, a, 0.0))
    a_full = blocks[0] if len(blocks) == 1 else jnp.concatenate(blocks, axis=0)
    o = o + _nn(a_full.astype(BF16), vb)
    k_dec = k * jnp.exp(b_last - bc)
    st_new = st * jnp.exp(b_last) + _tn(vb, k_dec.astype(BF16))
    return o, st_new


def _head_out(o, nrm, gate_act):
    return o * lax.rsqrt(jnp.mean(o * o, axis=1, keepdims=True) + EPS) * nrm * gate_act


def _gla_kernel(q_ref, k_ref, v_ref, og_ref, gt_ref, w2_ref, bgk_ref, nrm_ref, s0_ref,
                o_ref, s_ref, st_scr, *, SB, HB, dk, dv, n_chunks):
    c = pl.program_id(2)

    @pl.when(c == 0)
    def _():
        for hb in range(HB):
            st_scr[hb] = s0_ref[hb].T

    gates = gt_ref[...].astype(BF16)
    for hb in range(HB):
        ks = slice(hb * dk, (hb + 1) * dk)
        vs = slice(hb * dv, (hb + 1) * dv)
        gk = _nn(gates, w2_ref[hb]) + bgk_ref[hb]
        g = _log_sigmoid(gk) / B_GATE_NORMALIZER
        o, st_new = _scan_chunk(q_ref[:, ks] * (dk ** -0.5), k_ref[:, ks], v_ref[:, vs], g, st_scr[hb], SB=SB)
        st_scr[hb] = st_new
        og = og_ref[:, vs]
        o_ref[:, vs] = _head_out(o, nrm_ref[hb], og * _sigmoid(og)).astype(o_ref.dtype)

    @pl.when(c == n_chunks - 1)
    def _():
        for hb in range(HB):
            s_ref[hb] = st_scr[hb].T


def gla_mixer(proj, small, w2pad, b_gk, norm, s0, *, B, T, row_off, cols, H, dk, dv, CH, SB, HB):
    nC = T // CH
    rb = row_off // CH
    cq, ck, cv, co = cols
    wk, wv = HB * dk, HB * dv

    def rows(b, h, c):
        return rb + b * nC + c

    in_specs = [
        pl.BlockSpec((CH, wk), lambda b, h, c: (rows(b, h, c), cq // wk + h)),
        pl.BlockSpec((CH, wk), lambda b, h, c: (rows(b, h, c), ck // wk + h)),
        pl.BlockSpec((CH, wv), lambda b, h, c: (rows(b, h, c), cv // wv + h)),
        pl.BlockSpec((CH, wv), lambda b, h, c: (rows(b, h, c), co // wv + h)),
        pl.BlockSpec((CH, LANES), lambda b, h, c: (rows(b, h, c), 0)),
        pl.BlockSpec((HB, LANES, dk), lambda b, h, c: (h, 0, 0)),
        pl.BlockSpec((HB, 1, dk), lambda b, h, c: (h, 0, 0)),
        pl.BlockSpec((HB, 1, dv), lambda b, h, c: (h, 0, 0)),
        pl.BlockSpec((None, HB, dk, dv), lambda b, h, c: (b, h, 0, 0)),
    ]
    out_specs = [
        pl.BlockSpec((CH, wv), lambda b, h, c: (b * nC + c, h)),
        pl.BlockSpec((None, HB, dk, dv), lambda b, h, c: (b, h, 0, 0)),
    ]
    out_shape = [jax.ShapeDtypeStruct((B * T, H * dv), BF16), jax.ShapeDtypeStruct((B, H, dk, dv), F32)]
    return pl.pallas_call(
        functools.partial(_gla_kernel, SB=SB, HB=HB, dk=dk, dv=dv, n_chunks=nC),
        grid=(B, H // HB, nC),
        in_specs=in_specs, out_specs=out_specs, out_shape=out_shape,
        scratch_shapes=[pltpu.VMEM((HB, dv, dk), F32)],
        compiler_params=_cparams(("parallel", "parallel", "arbitrary")),
        name="gla",
    )(proj, proj, proj, proj, small, w2pad, b_gk.reshape(H, 1, dk), norm.reshape(H, 1, dv), s0)


def _hgrn_kernel(q_ref, f_ref, i_ref, og_ref, lg_ref, nrm_ref, s0_ref, o_ref, s_ref, st_scr,
                 *, SB, HB, dk, layer, n_chunks):
    c = pl.program_id(2)
    lg = lg_ref[...]
    e = jnp.exp(lg - jnp.max(lg, axis=0, keepdims=True))
    sm = e / jnp.sum(e, axis=0, keepdims=True)
    lb_all = jnp.sum(sm[1:layer + 1], axis=0, keepdims=True) if layer > 0 else jnp.zeros_like(sm[0:1])
    @pl.when(c == 0)
    def _():
        for hb in range(HB):
            st_scr[hb] = s0_ref[hb].T

    for hb in range(HB):
        sl = slice(hb * dk, (hb + 1) * dk)
        lb = lb_all[:, sl]
        z = f_ref[:, sl]
        g = jnp.log(lb + (1.0 - lb) * _sigmoid(z))
        key = (1.0 - lb) * _sigmoid(-z)
        cq = q_ref[:, sl]
        o, st_new = _scan_chunk(cq * _sigmoid(cq), key, i_ref[:, sl], g, st_scr[hb], SB=SB)
        st_scr[hb] = st_new
        og = og_ref[:, sl]
        o_ref[:, sl] = _head_out(o, nrm_ref[hb], og * _sigmoid(og)).astype(o_ref.dtype)

    @pl.when(c == n_chunks - 1)
    def _():
        for hb in range(HB):
            s_ref[hb] = st_scr[hb].T


def hgrn_mixer(proj, lb_logits, norm, s0, *, B, T, row_off, cols, H, dk, dv, CH, SB, HB, layer):
    assert dk == dv
    nC = T // CH
    rb = row_off // CH
    cq, cf, ci, co = cols
    depth = lb_logits.shape[0]
    wb = HB * dk

    def col(c0):
        return pl.BlockSpec((CH, wb), lambda b, h, c: (rb + b * nC + c, c0 // wb + h))

    in_specs = [
        col(cq), col(cf), col(ci), col(co),
        pl.BlockSpec((depth, wb), lambda b, h, c: (0, h)),
        pl.BlockSpec((HB, 1, dv), lambda b, h, c: (h, 0, 0)),
        pl.BlockSpec((None, HB, dk, dv), lambda b, h, c: (b, h, 0, 0)),
    ]
    out_specs = [
        pl.BlockSpec((CH, wb), lambda b, h, c: (b * nC + c, h)),
        pl.BlockSpec((None, HB, dk, dv), lambda b, h, c: (b, h, 0, 0)),
    ]
    out_shape = [jax.ShapeDtypeStruct((B * T, H * dv), BF16), jax.ShapeDtypeStruct((B, H, dk, dv), F32)]
    return pl.pallas_call(
        functools.partial(_hgrn_kernel, SB=SB, HB=HB, dk=dk, layer=layer, n_chunks=nC),
        grid=(B, H // HB, nC),
        in_specs=in_specs, out_specs=out_specs, out_shape=out_shape,
        scratch_shapes=[pltpu.VMEM((HB, dv, dk), F32)],
        compiler_params=_cparams(("parallel", "parallel", "arbitrary")),
        name="hgrn",
    )(proj, proj, proj, proj, lb_logits, norm.reshape(H, 1, dv), s0)


def _fox_cumsum_kernel(x_ref, b_ref, lf_ref, c_ref):
    lf = _log_sigmoid(x_ref[...] + b_ref[...])
    lf_ref[...] = lf
    c_ref[...] = _cumsum_lanes(lf, 1)


def fox_cumsum(df_t, bias):
    B, H, T = df_t.shape
    spec = pl.BlockSpec((None, H, T), lambda b: (b, 0, 0))
    return pl.pallas_call(
        _fox_cumsum_kernel,
        grid=(B,),
        in_specs=[spec, pl.BlockSpec((H, 1), lambda b: (0, 0))],
        out_specs=[spec, spec],
        out_shape=[jax.ShapeDtypeStruct((B, H, T), F32)] * 2,
        compiler_params=_cparams(("parallel",)),
        name="fox_cumsum",
    )(df_t, bias.reshape(H, 1))


def _fox_prompt_kernel(q_ref, k_ref, v_ref, og_ref, cr_ref, o_ref, *, TQ, T, dh):
    kb = k_ref[...].astype(BF16)
    vb = v_ref[...].astype(BF16)
    c_row = cr_ref[...]
    for i in range(T // TQ):
        lo, hi = i * TQ, (i + 1) * TQ
        q = (q_ref[lo:hi, :] * (dh ** -0.5)).astype(BF16)
        s = _nt(q, kb[:hi]) - c_row[:, :hi]
        qpos = lo + lax.broadcasted_iota(jnp.int32, (TQ, hi), 0)
        kpos = lax.broadcasted_iota(jnp.int32, (TQ, hi), 1)
        s = jnp.where(kpos <= qpos, s, NEG)
        p = jnp.exp(s - jnp.max(s, axis=1, keepdims=True))
        l = jnp.sum(p, axis=1, keepdims=True)
        o = _nn(p.astype(BF16), vb[:hi]) / l
        o_ref[lo:hi, :] = (o * _sigmoid(og_ref[lo:hi, :])).astype(o_ref.dtype)


def fox_prompt(proj, c_row, *, B, T, row_off, cols, H, dh, TQ):
    cq, ck, cv, co = cols
    rb = row_off // T

    def col(c0):
        return pl.BlockSpec((T, dh), lambda b, h: (rb + b, c0 // dh + h))

    in_specs = [col(cq), col(ck), col(cv), col(co),
                pl.BlockSpec((None, None, 1, T), lambda b, h: (b, h, 0, 0))]
    return pl.pallas_call(
        functools.partial(_fox_prompt_kernel, TQ=TQ, T=T, dh=dh),
        grid=(B, H),
        in_specs=in_specs,
        out_specs=pl.BlockSpec((T, dh), lambda b, h: (b, h)),
        out_shape=jax.ShapeDtypeStruct((B * T, H * dh), BF16),
        compiler_params=_cparams(("parallel", "parallel")),
        name="fox_prompt",
    )(proj, proj, proj, proj, c_row)


def _fox_sample_kernel(pt_ref, q_ref, kn_ref, vn_ref, og_ref, df_ref, fb_ref, *rest, PP, T, H, dh, n_steps):
    k_refs = rest[0:PP]
    v_refs = rest[PP:2 * PP]
    f_refs = rest[2 * PP:3 * PP]
    o_ref, lf_ref = rest[3 * PP], rest[3 * PP + 1]
    qall, acc, m_scr, l_scr, carry, crow, hmask = rest[3 * PP + 2:]
    s_id = pl.program_id(1)
    R = T * H
    W = PAGE * H

    @pl.when(s_id == 0)
    def _():
        qall[...] = (q_ref[...].reshape(R, dh) * (dh ** -0.5)).astype(BF16)
        acc[...] = jnp.zeros_like(acc)
        m_scr[...] = jnp.full_like(m_scr, NEG)
        l_scr[...] = jnp.zeros_like(l_scr)
        carry[...] = jnp.zeros_like(carry)
        lf_new = _log_sigmoid(df_ref[...] + fb_ref[...])
        lf_ref[...] = lf_new
        crow[...] = _cumsum_lanes(lf_new, H)
        rr = lax.broadcasted_iota(jnp.int32, (R, W), 0)
        ll = lax.broadcasted_iota(jnp.int32, (R, W), 1)
        hmask[...] = jnp.where((rr % H) == (ll % H), 0.0, NEG)

    lf = jnp.concatenate([f_refs[p][...] for p in range(PP)], axis=0)
    lane = lax.broadcasted_iota(jnp.int32, (PP, W), 1)
    suf = lf
    tot = lf
    s = H
    while s < W:
        suf = suf + jnp.where(lane + s < W, pltpu.roll(suf, W - s, 1), 0.0)
        tot = tot + pltpu.roll(tot, s, 1)
        s *= 2
    tot_cum = _cumsum_rows(tot)
    later = carry[...]
    d_all = suf - lf + (tot_cum - tot) + later
    carry[...] = later + tot_cum[PP - 1:PP, :]

    qa = qall[...]
    hm = hmask[...]
    m_old = m_scr[...]
    l_old = l_scr[...]
    a_old = acc[...]
    for p in range(PP):
        kb = k_refs[p][...].astype(BF16)
        vb = v_refs[p][...].astype(BF16)
        st = _nt(qa, kb) + (hm + d_all[p:p + 1, :])
        m_new = jnp.maximum(m_old, jnp.max(st, axis=1, keepdims=True))
        alpha = jnp.exp(m_old - m_new)
        pm = jnp.exp(st - m_new)
        l_old = alpha * l_old + jnp.sum(pm, axis=1, keepdims=True)
        a_old = alpha * a_old + _nn(pm.astype(BF16), vb)
        m_old = m_new
    m_scr[...] = m_old
    l_scr[...] = l_old
    acc[...] = a_old

    @pl.when(s_id == n_steps - 1)
    def _():
        knb = kn_ref[...].reshape(R, dh).astype(BF16)
        vnb = vn_ref[...].reshape(R, dh).astype(BF16)
        r2 = lax.broadcasted_iota(jnp.int32, (R, R), 0)
        l2 = lax.broadcasted_iota(jnp.int32, (R, R), 1)
        ok = ((r2 % H) == (l2 % H)) & (l2 // H <= r2 // H)
        sn = jnp.where(ok, _nt(qa, knb) - crow[...], NEG)
        m_fin = jnp.maximum(m_old, jnp.max(sn, axis=1, keepdims=True))
        a2 = jnp.exp(m_old - m_fin)
        pn = jnp.exp(sn - m_fin)
        l_fin = a2 * l_old + jnp.sum(pn, axis=1, keepdims=True)
        out = (a2 * a_old + _nn(pn.astype(BF16), vnb)) / l_fin
        out = out * _sigmoid(og_ref[...].reshape(R, dh))
        o_ref[...] = out.reshape(T, H, dh).astype(o_ref.dtype)


def fox_sample(proj3, df_flat, fb_flat, cache_k, cache_v, cache_f, page_table, *, B, T, page_off, cols, H, dh, PP):
    R = T * H
    W = PAGE * H
    n_pages = page_table.shape[1]
    n_steps = n_pages // PP
    cq, ck, cv, co = cols

    def rowspec(c0):
        return pl.BlockSpec((T, H, dh), lambda b, s, pt: (b, c0 // H, 0))

    def page(p):
        return lambda b, s, pt: (page_off + pt[b, n_pages - 1 - (s * PP + p)], 0, 0)

    in_specs = [rowspec(cq), rowspec(ck), rowspec(cv), rowspec(co),
                pl.BlockSpec((None, 1, R), lambda b, s, pt: (b, 0, 0)),
                pl.BlockSpec((1, R), lambda b, s, pt: (0, 0))]
    in_specs += [pl.BlockSpec((None, W, dh), page(p)) for p in range(PP)]
    in_specs += [pl.BlockSpec((None, W, dh), page(p)) for p in range(PP)]
    in_specs += [pl.BlockSpec((None, 1, W), page(p)) for p in range(PP)]
    grid_spec = pltpu.PrefetchScalarGridSpec(
        num_scalar_prefetch=1,
        grid=(B, n_steps),
        in_specs=in_specs,
        out_specs=[pl.BlockSpec((T, H, dh), lambda b, s, pt: (b, 0, 0)),
                   pl.BlockSpec((None, 1, R), lambda b, s, pt: (b, 0, 0))],
        scratch_shapes=[
            pltpu.VMEM((R, dh), BF16),
            pltpu.VMEM((R, dh), F32),
            pltpu.VMEM((R, 1), F32),
            pltpu.VMEM((R, 1), F32),
            pltpu.VMEM((1, W), F32),
            pltpu.VMEM((1, R), F32),
            pltpu.VMEM((R, W), F32),
        ],
    )
    return pl.pallas_call(
        functools.partial(_fox_sample_kernel, PP=PP, T=T, H=H, dh=dh, n_steps=n_steps),
        grid_spec=grid_spec,
        out_shape=[jax.ShapeDtypeStruct((B * T, H, dh), BF16), jax.ShapeDtypeStruct((B, 1, R), F32)],
        compiler_params=_cparams(("parallel", "arbitrary")),
        name="fox_sample",
    )(page_table, proj3, proj3, proj3, proj3, df_flat, fb_flat,
      *([cache_k] * PP), *([cache_v] * PP), *([cache_f] * PP))


def _pad_lanes(w):
    return jnp.pad(w, ((0, 0), (0, LANES - w.shape[1])))


def _row_tile(m, cap):
    best = None
    for t in range(BF16_ROWS, min(m, cap) + 1, BF16_ROWS):
        if m % t == 0:
            best = t
    assert best is not None, (m, cap)
    return best


def _col_tile(n, cap):
    t = cap
    while n % t:
        t //= 2
    return t


def kernel(x_prompt, x_sample, state_mlstm_C, state_mlstm_n, state_mlstm_m, state_gla_S, state_hgrn_S, cache_fox_k, cache_fox_v, cache_fox_logf, page_table, norm_mix_pre, norm_mix_post, norm_ffn_pre, norm_ffn_post, ffn_w_gate, ffn_w_up, ffn_w_down, w_in_even, w_out_even, mlstm_b_i, mlstm_b_f, mlstm_norm, gla_w_gk2, gla_b_gk, gla_norm, w_in_odd, w_out_odd, hgrn_lb_logits, hgrn_norm, fox_b_f):
    bp, tp, D = x_prompt.shape
    bs, ts, _ = x_sample.shape
    mp, ms = bp * tp, bs * ts
    M = mp + ms
    depth = norm_mix_pre.shape[0]
    a_heads, a_dk, a_dv = state_mlstm_C.shape[2:]
    b_heads, b_dk, b_dv = state_gla_S.shape[2:]
    c_heads, c_dk, c_dv = state_hgrn_S.shape[2:]
    d_heads = fox_b_f.shape[1]
    half = D // 2
    d_dh = half // d_heads
    ff = ffn_w_gate.shape[3]
    n_phys = cache_fox_k.shape[1]

    tm = _row_tile(M, 1408)
    tr = _row_tile(M, 256)
    wd_all = ffn_w_down.astype(BF16)

    def ffn(h, layer, j):
        u = ffn_up(h, ffn_w_gate, ffn_w_up, (layer, j), tm=tm, tn=_col_tile(ff, 256))
        return matmul([(u, wd_all, (layer, j), 0)], out_dtype=F32, tm=tm, tn=_col_tile(D, 512), nk=2)

    groups = ((bp, tp, 0), (bs, ts, mp))
    outs = {k: ([], []) for k in ('c', 'n', 'm', 'g', 'h', 'k', 'v', 'f')}

    x = jnp.concatenate([x_prompt.reshape(mp, D), x_sample.reshape(ms, D)], axis=0)
    h = prenorm(x, norm_ffn_pre[0, 0], tr=tr)
    for layer in range(depth):
        y = ffn(h, layer, 0)
        x, h = resid_norm(x, y, norm_ffn_post[layer, 0], norm_mix_pre[layer], scale=0.5, tr=tr)
        if layer % 2 == 0:
            e = layer // 2
            w = w_in_even[e]
            n_a = a_heads * (2 * a_dk + 2 * a_dv)
            n_b = b_heads * (2 * b_dk + 2 * b_dv)
            o_b = n_a + 2 * a_heads
            w_main = jnp.concatenate([w[:, :n_a], w[:, o_b:o_b + n_b]], axis=1).astype(BF16)
            w_small = _pad_lanes(jnp.concatenate([w[:, n_a:o_b], w[:, o_b + n_b:]], axis=1)).astype(BF16)
            proj = matmul([(h, w_main, (), 0)], out_dtype=F32, tm=tm, tn=_col_tile(n_a + n_b, 512))
            small = matmul([(h, w_small, (), 0)], out_dtype=F32, tm=tm, tn=LANES)
            gate_bias = _pad_lanes(jnp.concatenate([mlstm_b_i[e], mlstm_b_f[e]])[None, :])
            a_cols = (0, a_heads * a_dk, 2 * a_heads * a_dk, 2 * a_heads * a_dk + a_heads * a_dv)
            w2 = gla_w_gk2[e]
            w2pad = jnp.zeros((LANES, b_heads * b_dk), F32).at[2 * a_heads:2 * a_heads + w2.shape[0]].set(w2)
            w2pad = w2pad.reshape(LANES, b_heads, b_dk).transpose(1, 0, 2).astype(BF16)
            b_cols = (n_a, n_a + b_heads * b_dk, n_a + 2 * b_heads * b_dk, n_a + 2 * b_heads * b_dk + b_heads * b_dv)
            ha, hb = [], []
            for gi, (B, T, r0) in enumerate(groups):
                if gi == 0:
                    c0, n0, m0, s0 = (jnp.zeros((B,) + s.shape[2:], F32)
                                      for s in (state_mlstm_C, state_mlstm_n, state_mlstm_m, state_gla_S))
                else:
                    c0, n0, m0, s0 = state_mlstm_C[e], state_mlstm_n[e], state_mlstm_m[e], state_gla_S[e]
                CH = min(T, 128)
                ha_g, c1, n1, m1 = mlstm_mixer(proj, small, gate_bias, mlstm_norm[e], c0, n0, m0,
                                               B=B, T=T, row_off=r0, cols=a_cols,
                                               H=a_heads, dk=a_dk, dv=a_dv, L=CH, HB=2)
                hb_g, s1 = gla_mixer(proj, small, w2pad, gla_b_gk[e], gla_norm[e], s0,
                                     B=B, T=T, row_off=r0, cols=b_cols,
                                     H=b_heads, dk=b_dk, dv=b_dv, CH=CH, SB=min(CH, 32), HB=2)
                ha.append(ha_g); hb.append(hb_g)
                outs['c'][gi].append(c1); outs['n'][gi].append(n1)
                outs['m'][gi].append(m1); outs['g'][gi].append(s1)
            wo = w_out_even.astype(BF16)
            mix = matmul([(jnp.concatenate(ha, axis=0), wo, (e,), 0), (jnp.concatenate(hb, axis=0), wo, (e,), 1)],
                         out_dtype=F32, tm=tm, tn=_col_tile(D, 512))
        else:
            o = layer // 2
            n_main = 4 * c_heads * c_dk + 4 * half
            w_small = _pad_lanes(w_in_odd[o][:, n_main:]).astype(BF16)
            proj = matmul([(h, w_in_odd.astype(BF16), (o,), 0)], out_dtype=F32, tm=tm,
                          tn=_col_tile(n_main, 512), n=n_main)
            small = matmul([(h, w_small, (), 0)], out_dtype=F32, tm=tm, tn=LANES)
            cw = c_heads * c_dk
            dq = 4 * cw
            d_cols = (dq, dq + half, dq + 2 * half, dq + 3 * half)
            hc, hd = [], []
            for gi, (B, T, r0) in enumerate(groups):
                s0 = jnp.zeros((B,) + state_hgrn_S.shape[2:], F32) if gi == 0 else state_hgrn_S[o]
                CH = min(T, 128)
                hc_g, s1 = hgrn_mixer(proj, hgrn_lb_logits, hgrn_norm[o], s0,
                                      B=B, T=T, row_off=r0, cols=(0, cw, 2 * cw, 3 * cw),
                                      H=c_heads, dk=c_dk, dv=c_dv, CH=CH, SB=min(CH, 32), HB=4, layer=layer)
                df = small[r0:r0 + B * T, :d_heads]
                if gi == 0:
                    lf_t, c_t = fox_cumsum(df.reshape(B, T, d_heads).transpose(0, 2, 1), fox_b_f[o])
                    logf = lf_t.transpose(0, 2, 1)
                    hd_g = fox_prompt(proj, c_t.reshape(B, d_heads, 1, T), B=B, T=T, row_off=r0,
                                      cols=d_cols, H=d_heads, dh=d_dh, TQ=min(T, 256))
                else:
                    n_cb = proj.shape[1] // d_dh
                    proj3 = proj[r0:r0 + B * T].reshape(B * T, n_cb, d_dh)
                    n_odd = cache_fox_k.shape[0]
                    hd3, lf = fox_sample(
                        proj3, df.reshape(B, 1, T * d_heads), jnp.tile(fox_b_f[o], T)[None, :],
                        cache_fox_k.reshape(n_odd * n_phys, PAGE * d_heads, d_dh),
                        cache_fox_v.reshape(n_odd * n_phys, PAGE * d_heads, d_dh),
                        cache_fox_logf.reshape(n_odd * n_phys, 1, PAGE * d_heads),
                        page_table, B=B, T=T, page_off=o * n_phys,
                        cols=tuple(c // d_dh for c in d_cols), H=d_heads, dh=d_dh, PP=8)
                    hd_g = hd3.reshape(B * T, half)
                    logf = lf.reshape(B, T, d_heads)
                hc.append(hc_g); hd.append(hd_g)
                outs['h'][gi].append(s1)
                k_new, v_new = split_heads(proj, rows=B * T, row_off=r0, ck=d_cols[1], cv=d_cols[2],
                                           H=d_heads, dh=d_dh, tr=min(B * T, 256))
                outs['k'][gi].append(k_new.reshape(B, T, d_heads, d_dh))
                outs['v'][gi].append(v_new.reshape(B, T, d_heads, d_dh))
                outs['f'][gi].append(logf)
            wo = w_out_odd.astype(BF16)
            mix = matmul([(jnp.concatenate(hc, axis=0), wo, (o,), 0), (jnp.concatenate(hd, axis=0), wo, (o,), 1)],
                         out_dtype=F32, tm=tm, tn=_col_tile(D, 512))
        x, h = resid_norm(x, mix, norm_mix_post[layer], norm_ffn_pre[layer, 1], scale=1.0, tr=tr)
        y = ffn(h, layer, 1)
        if layer + 1 < depth:
            x, h = resid_norm(x, y, norm_ffn_post[layer, 1], norm_ffn_pre[layer + 1, 0], scale=0.5, tr=tr)
        else:
            y_p, y_s = resid_split(x, y, norm_ffn_post[layer, 1], scale=0.5, rows_a=mp, tr=math.gcd(mp, ms))

    names = ('c', 'n', 'm', 'g', 'h', 'k', 'v', 'f')
    res_p = tuple(jnp.stack(outs[k][0]) for k in names)
    res_s = tuple(jnp.stack(outs[k][1]) for k in names)
    return (y_p.reshape(bp, tp, D), y_s.reshape(bs, ts, D)) + res_p + res_s
```

```python
import functools
import math

import jax
import jax.numpy as jnp
from jax import lax
from jax.experimental import pallas as pl
from jax.experimental.pallas import tpu as pltpu

F32 = jnp.float32
BF16 = jnp.bfloat16
EPS = 1e-6
NEG = -1e30
LANES = 128
BF16_ROWS = 16
VMEM_LIMIT = 56 * 1024 * 1024
VMEM_LIMIT_HIGH = 60 * 1024 * 1024

B_GATE_NORMALIZER = 16.0
PAGE = 128


def _cparams(sem):
    return pltpu.CompilerParams(dimension_semantics=sem, vmem_limit_bytes=VMEM_LIMIT)


def _log_sigmoid(x):
    return jnp.minimum(x, 0.0) - jnp.log1p(jnp.exp(-jnp.abs(x)))


def _sigmoid(x):
    return jax.nn.sigmoid(x)


def _nt(a, b):
    return lax.dot_general(a, b, (((1,), (1,)), ((), ())), preferred_element_type=F32)


def _tn(a, b):
    return lax.dot_general(a, b, (((0,), (0,)), ((), ())), preferred_element_type=F32)


def _nn(a, b):
    return jnp.dot(a, b, preferred_element_type=F32)


def _cumsum_rows(x):
    n = x.shape[0]
    row = lax.broadcasted_iota(jnp.int32, x.shape, 0)
    s = 1
    while s < n:
        x = x + jnp.where(row >= s, pltpu.roll(x, s, 0), 0.0)
        s *= 2
    return x


def _cumsum_lanes(x, step):
    n = x.shape[1]
    lane = lax.broadcasted_iota(jnp.int32, x.shape, 1)
    s = step
    while s < n:
        x = x + jnp.where(lane >= s, pltpu.roll(x, s, 1), 0.0)
        s *= 2
    return x


def _prenorm_kernel(x_ref, g_ref, o_ref):
    x = x_ref[...]
    r = lax.rsqrt(jnp.mean(x * x, axis=-1, keepdims=True) + EPS)
    o_ref[...] = (x * r * g_ref[...]).astype(o_ref.dtype)


def prenorm(x, g, *, tr):
    m, d = x.shape
    return pl.pallas_call(
        _prenorm_kernel,
        grid=(m // tr,),
        in_specs=[pl.BlockSpec((tr, d), lambda i: (i, 0)), pl.BlockSpec((1, d), lambda i: (0, 0))],
        out_specs=pl.BlockSpec((tr, d), lambda i: (i, 0)),
        out_shape=jax.ShapeDtypeStruct((m, d), BF16),
        compiler_params=_cparams(("parallel",)),
        name="prenorm",
    )(x, g.reshape(1, d))


def _resid_kernel(x_ref, y_ref, gp_ref, gn_ref, xo_ref, ho_ref, *, scale):
    y = y_ref[...]
    r = lax.rsqrt(jnp.mean(y * y, axis=-1, keepdims=True) + EPS)
    xn = x_ref[...] + scale * (y * r * gp_ref[...])
    xo_ref[...] = xn
    r2 = lax.rsqrt(jnp.mean(xn * xn, axis=-1, keepdims=True) + EPS)
    ho_ref[...] = (xn * r2 * gn_ref[...]).astype(ho_ref.dtype)


def resid_norm(x, y, g_post, g_next, *, scale, tr):
    m, d = x.shape
    row = pl.BlockSpec((tr, d), lambda i: (i, 0))
    vec = pl.BlockSpec((1, d), lambda i: (0, 0))
    return pl.pallas_call(
        functools.partial(_resid_kernel, scale=scale),
        grid=(m // tr,),
        in_specs=[row, row, vec, vec],
        out_specs=[row, row],
        out_shape=[jax.ShapeDtypeStruct((m, d), F32), jax.ShapeDtypeStruct((m, d), BF16)],
        compiler_params=_cparams(("parallel",)),
        name="resid_norm",
    )(x, y, g_post.reshape(1, d), g_next.reshape(1, d))


def _resid_split_kernel(x_ref, y_ref, gp_ref, xa_ref, xb_ref, *, scale, n_a):
    i = pl.program_id(0)
    y = y_ref[...]
    r = lax.rsqrt(jnp.mean(y * y, axis=-1, keepdims=True) + EPS)
    xn = x_ref[...] + scale * (y * r * gp_ref[...])

    @pl.when(i < n_a)
    def _():
        xa_ref[...] = xn

    @pl.when(i >= n_a)
    def _():
        xb_ref[...] = xn


def resid_split(x, y, g_post, *, scale, rows_a, tr):
    m, d = x.shape
    n_a = rows_a // tr
    row = pl.BlockSpec((tr, d), lambda i: (i, 0))
    return pl.pallas_call(
        functools.partial(_resid_split_kernel, scale=scale, n_a=n_a),
        grid=(m // tr,),
        in_specs=[row, row, pl.BlockSpec((1, d), lambda i: (0, 0))],
        out_specs=[pl.BlockSpec((tr, d), lambda i: (jnp.minimum(i, n_a - 1), 0)),
                   pl.BlockSpec((tr, d), lambda i: (jnp.maximum(i - n_a, 0), 0))],
        out_shape=[jax.ShapeDtypeStruct((rows_a, d), F32), jax.ShapeDtypeStruct((m - rows_a, d), F32)],
        compiler_params=_cparams(("arbitrary",)),
        name="resid_split",
    )(x, y, g_post.reshape(1, d))


def _split_heads_kernel(k_ref, v_ref, ko_ref, vo_ref, *, H, dh):
    for h in range(H):
        ko_ref[:, h, :] = k_ref[:, h * dh:(h + 1) * dh]
        vo_ref[:, h, :] = v_ref[:, h * dh:(h + 1) * dh]


def split_heads(proj, *, rows, row_off, ck, cv, H, dh, tr):
    w = H * dh
    out = jax.ShapeDtypeStruct((rows, H, dh), proj.dtype)
    ospec = pl.BlockSpec((tr, H, dh), lambda i: (i, 0, 0))
    return pl.pallas_call(
        functools.partial(_split_heads_kernel, H=H, dh=dh),
        grid=(rows // tr,),
        in_specs=[pl.BlockSpec((tr, w), lambda i: (row_off // tr + i, ck // w)),
                  pl.BlockSpec((tr, w), lambda i: (row_off // tr + i, cv // w))],
        out_specs=[ospec, ospec],
        out_shape=[out, out],
        compiler_params=_cparams(("parallel",)),
        name="split_heads",
    )(proj, proj)


def _mm_kernel(*refs, n_pairs, nk):
    o_ref = refs[-1]
    acc = _nn(refs[0][...], refs[1][...])
    for p in range(1, n_pairs):
        acc = acc + _nn(refs[2 * p][...], refs[2 * p + 1][...])
    if nk == 1:
        o_ref[...] = acc.astype(o_ref.dtype)
    else:
        k = pl.program_id(2)

        @pl.when(k == 0)
        def _():
            o_ref[...] = acc

        @pl.when(k > 0)
        def _():
            o_ref[...] += acc


def matmul(pairs, *, out_dtype, tm, tn, nk=1):
    m = pairs[0][0].shape[0]
    n = pairs[0][1].shape[-1]
    assert nk == 1 or out_dtype == F32
    in_specs, args = [], []
    for x, w, lead, kb0 in pairs:
        tk = x.shape[1] // nk
        in_specs.append(pl.BlockSpec((tm, tk), lambda i, j, k: (i, k)))
        in_specs.append(pl.BlockSpec((None,) * len(lead) + (tk, tn),
                                     lambda i, j, k, lead=lead, kb0=kb0, nk=nk: lead + (kb0 * nk + k, j)))
        args += [x, w]
    return pl.pallas_call(
        functools.partial(_mm_kernel, n_pairs=len(pairs), nk=nk),
        grid=(m // tm, n // tn, nk),
        in_specs=in_specs,
        out_specs=pl.BlockSpec((tm, tn), lambda i, j, k: (i, j)),
        out_shape=jax.ShapeDtypeStruct((m, n), out_dtype),
        compiler_params=_cparams(("parallel", "parallel", "arbitrary")),
        name="matmul",
    )(*args)


def _mm_nt_kernel(x_ref, wt_ref, o_ref, w_bf):
    @pl.when(pl.program_id(1) == 0)
    def _():
        w_bf[...] = wt_ref[...].astype(BF16)

    o_ref[...] = _nt(x_ref[...], w_bf[...]).astype(o_ref.dtype)


def matmul_nt(x, wt, lead, *, n, out_dtype, tm, tn):
    m, k = x.shape
    return pl.pallas_call(
        _mm_nt_kernel,
        grid=(n // tn, m // tm),
        in_specs=[pl.BlockSpec((tm, k), lambda j, i: (i, 0)),
                  pl.BlockSpec((None,) * len(lead) + (tn, k), lambda j, i: lead + (j, 0))],
        out_specs=pl.BlockSpec((tm, tn), lambda j, i: (i, j)),
        out_shape=jax.ShapeDtypeStruct((m, n), out_dtype),
        scratch_shapes=[pltpu.VMEM((tn, k), BF16)],
        compiler_params=_cparams(("parallel", "arbitrary")),
        name="matmul_nt",
    )(x, wt)


def _ffn_up_kernel(h_ref, wg_ref, wu_ref, wd_ref, o_ref, wdo_ref, wg_bf, wu_bf):
    @pl.when(pl.program_id(1) == 0)
    def _():
        wg_bf[...] = wg_ref[...].astype(BF16)
        wu_bf[...] = wu_ref[...].astype(BF16)
        wdo_ref[...] = wd_ref[...].astype(BF16)

    h = h_ref[...]
    g = _nn(h, wg_bf[...])
    u = _nn(h, wu_bf[...])
    o_ref[...] = (g * _sigmoid(g) * u).astype(o_ref.dtype)


def ffn_up(h, wg, wu, wd, lead, *, tm, tn):
    m, k = h.shape
    n = wg.shape[-1]
    d = wd.shape[-1]
    nl = (None,) * len(lead)
    wspec = pl.BlockSpec(nl + (k, tn), lambda j, i: lead + (0, j))
    return pl.pallas_call(
        _ffn_up_kernel,
        grid=(n // tn, m // tm),
        in_specs=[pl.BlockSpec((tm, k), lambda j, i: (i, 0)), wspec, wspec,
                  pl.BlockSpec(nl + (tn, d), lambda j, i: lead + (j, 0))],
        out_specs=[pl.BlockSpec((tm, tn), lambda j, i: (i, j)), pl.BlockSpec((tn, d), lambda j, i: (j, 0))],
        out_shape=[jax.ShapeDtypeStruct((m, n), BF16), jax.ShapeDtypeStruct((n, d), BF16)],
        scratch_shapes=[pltpu.VMEM((k, tn), BF16), pltpu.VMEM((k, tn), BF16)],
        compiler_params=pltpu.CompilerParams(dimension_semantics=("parallel", "arbitrary"),
                                             vmem_limit_bytes=VMEM_LIMIT_HIGH),
        name="ffn_up",
    )(h, wg, wu, wd)


def _mlstm_kernel(q_ref, k_ref, v_ref, og_ref, gt_ref, gb_ref, nrm_ref, c0_ref, n0_ref, m0_ref,
                  h_ref, c_ref, n_ref, m_ref, *, L, HB, dk, dv, n_heads):
    @pl.when(pl.program_id(2) == 0)
    def _():
        c_ref[...] = c0_ref[...]
        n_ref[...] = n0_ref[...]
        m_ref[...] = m0_ref[...]

    gates = gt_ref[...] + gb_ref[...]
    lane = lax.broadcasted_iota(jnp.int32, gates.shape, 1)
    r = lax.broadcasted_iota(jnp.int32, (L, L), 0)
    s = lax.broadcasted_iota(jnp.int32, (L, L), 1)
    tril = r >= s
    eye = r == s

    def col2row(col):
        return jnp.sum(jnp.where(eye, col, 0.0), axis=0, keepdims=True)

    for hb in range(HB):
        hh = pl.program_id(1) * HB + hb
        ks = slice(hb * dk, (hb + 1) * dk)
        vs = slice(hb * dv, (hb + 1) * dv)
        i_col = jnp.sum(jnp.where(lane == hh, gates, 0.0), axis=1, keepdims=True)
        f_col = jnp.sum(jnp.where(lane == hh + n_heads, gates, 0.0), axis=1, keepdims=True)
        logf = _log_sigmoid(f_col)
        logf_row = col2row(logf)
        b_col = jnp.sum(jnp.where(tril, logf_row, 0.0), axis=1, keepdims=True)
        b_row = col2row(b_col)
        i_row = col2row(i_col)
        m_prev = m_ref[hb]
        dmat = jnp.where(tril, b_col - b_row + i_row, -jnp.inf)
        from_state = b_col + m_prev
        m_t = jnp.maximum(from_state, jnp.max(dmat, axis=1, keepdims=True))

        q = q_ref[:, ks]
        k = k_ref[:, ks] * (dk ** -0.5)
        qb = q.astype(BF16)
        vb = v_ref[:, vs].astype(BF16)
        w = jnp.exp(dmat - m_t) * _nt(qb, k.astype(BF16))
        s_state = jnp.exp(from_state - m_t)
        c_old = c_ref[hb]
        n_old = n_ref[hb]
        num = s_state * _nn(qb, c_old.astype(BF16)) + _nn(w.astype(BF16), vb)
        den = s_state * jnp.sum(q * n_old, axis=1, keepdims=True) + jnp.sum(w, axis=1, keepdims=True)
        h = num / jnp.maximum(jnp.abs(den), jnp.exp(-m_t))

        m_new = m_t[L - 1:L, :]
        b_last = b_col[L - 1:L, :]
        w_end = jnp.exp(b_last - b_col + i_col - m_new)
        s_end = jnp.exp(b_last + m_prev - m_new)
        kw = k * w_end
        c_ref[hb] = s_end * c_old + _tn(kw.astype(BF16), vb)
        n_ref[hb] = s_end * n_old + jnp.sum(kw, axis=0, keepdims=True)
        m_ref[hb] = m_new

        hn = h * lax.rsqrt(jnp.mean(h * h, axis=1, keepdims=True) + EPS) * nrm_ref[hb]
        h_ref[:, vs] = (hn * _sigmoid(og_ref[:, vs])).astype(h_ref.dtype)


def mlstm_mixer(proj, small, gate_bias, norm, c0, n0, m0, *, B, T, row_off, cols, H, dk, dv, L, HB):
    nC = T // L
    rb = row_off // L
    cq, ck, cv, co = cols
    wk, wv = HB * dk, HB * dv

    def rows(b, h, c):
        return rb + b * nC + c

    in_specs = [
        pl.BlockSpec((L, wk), lambda b, h, c: (rows(b, h, c), cq // wk + h)),
        pl.BlockSpec((L, wk), lambda b, h, c: (rows(b, h, c), ck // wk + h)),
        pl.BlockSpec((L, wv), lambda b, h, c: (rows(b, h, c), cv // wv + h)),
        pl.BlockSpec((L, wv), lambda b, h, c: (rows(b, h, c), co // wv + h)),
        pl.BlockSpec((L, LANES), lambda b, h, c: (rows(b, h, c), 0)),
        pl.BlockSpec((1, LANES), lambda b, h, c: (0, 0)),
        pl.BlockSpec((HB, 1, dv), lambda b, h, c: (h, 0, 0)),
        pl.BlockSpec((None, HB, dk, dv), lambda b, h, c: (b, h, 0, 0)),
        pl.BlockSpec((None, HB, 1, dk), lambda b, h, c: (b, h, 0, 0)),
        pl.BlockSpec((None, HB, 1, 1), lambda b, h, c: (b, h, 0, 0)),
    ]
    out_specs = [
        pl.BlockSpec((L, wv), lambda b, h, c: (b * nC + c, h)),
        pl.BlockSpec((None, HB, dk, dv), lambda b, h, c: (b, h, 0, 0)),
        pl.BlockSpec((None, HB, 1, dk), lambda b, h, c: (b, h, 0, 0)),
        pl.BlockSpec((None, HB, 1, 1), lambda b, h, c: (b, h, 0, 0)),
    ]
    out_shape = [
        jax.ShapeDtypeStruct((B * T, H * dv), BF16),
        jax.ShapeDtypeStruct((B, H, dk, dv), F32),
        jax.ShapeDtypeStruct((B, H, 1, dk), F32),
        jax.ShapeDtypeStruct((B, H, 1, 1), F32),
    ]
    h, c1, n1, m1 = pl.pallas_call(
        functools.partial(_mlstm_kernel, L=L, HB=HB, dk=dk, dv=dv, n_heads=H),
        grid=(B, H // HB, nC),
        in_specs=in_specs, out_specs=out_specs, out_shape=out_shape,
        compiler_params=_cparams(("parallel", "parallel", "arbitrary")),
        name="mlstm",
    )(proj, proj, proj, proj, small, gate_bias, norm.reshape(H, 1, dv), c0,
      n0.reshape(B, H, 1, dk), m0.reshape(B, H, 1, 1))
    return h, c1, n1.reshape(B, H, dk), m1.reshape(B, H)


def _scan_chunk(q, k, v, g, st, *, SB):
    CH = q.shape[0]
    bc = _cumsum_rows(g)
    b_last = bc[CH - 1:CH, :]
    vb = v.astype(BF16)
    o = _nt((q * jnp.exp(bc)).astype(BF16), st.astype(BF16))
    row = lax.broadcasted_iota(jnp.int32, bc.shape, 0)
    srow = lax.broadcasted_iota(jnp.int32, (SB, CH), 0)
    scol = lax.broadcasted_iota(jnp.int32, (SB, CH), 1)
    blocks = []
    for i in range(CH // SB):
        lo, hi = i * SB, (i + 1) * SB
        ref = bc[lo - 1:lo, :] if i > 0 else jnp.zeros_like(b_last)
        qi = q[lo:hi] * jnp.exp(bc[lo:hi] - ref)
        ki = k * jnp.exp(jnp.where(row < hi, ref - bc, 0.0))
        a = _nt(qi.astype(BF16), ki.astype(BF16))
        blocks.append(jnp.where(scol <= srow + lo, a, 0.0))
    a_full = blocks[0] if len(blocks) == 1 else jnp.concatenate(blocks, axis=0)
    o = o + _nn(a_full.astype(BF16), vb)
    k_dec = k * jnp.exp(b_last - bc)
    st_new = st * jnp.exp(b_last) + _tn(vb, k_dec.astype(BF16))
    return o, st_new


def _head_out(o, nrm, gate_act):
    return o * lax.rsqrt(jnp.mean(o * o, axis=1, keepdims=True) + EPS) * nrm * gate_act


def _gla_kernel(q_ref, k_ref, v_ref, og_ref, gt_ref, w2_ref, bgk_ref, nrm_ref, s0_ref,
                o_ref, s_ref, st_scr, *, SB, HB, dk, dv, n_chunks):
    c = pl.program_id(2)

    @pl.when(c == 0)
    def _():
        for hb in range(HB):
            st_scr[hb] = s0_ref[hb].T

    gates = gt_ref[...].astype(BF16)
    for hb in range(HB):
        ks = slice(hb * dk, (hb + 1) * dk)
        vs = slice(hb * dv, (hb + 1) * dv)
        gk = _nn(gates, w2_ref[hb]) + bgk_ref[hb]
        g = _log_sigmoid(gk) / B_GATE_NORMALIZER
        o, st_new = _scan_chunk(q_ref[:, ks] * (dk ** -0.5), k_ref[:, ks], v_ref[:, vs], g, st_scr[hb], SB=SB)
        st_scr[hb] = st_new
        og = og_ref[:, vs]
        o_ref[:, vs] = _head_out(o, nrm_ref[hb], og * _sigmoid(og)).astype(o_ref.dtype)

    @pl.when(c == n_chunks - 1)
    def _():
        for hb in range(HB):
            s_ref[hb] = st_scr[hb].T


def gla_mixer(proj, small, w2pad, b_gk, norm, s0, *, B, T, row_off, cols, H, dk, dv, CH, SB, HB):
    nC = T // CH
    rb = row_off // CH
    cq, ck, cv, co = cols
    wk, wv = HB * dk, HB * dv

    def rows(b, h, c):
        return rb + b * nC + c

    in_specs = [
        pl.BlockSpec((CH, wk), lambda b, h, c: (rows(b, h, c), cq // wk + h)),
        pl.BlockSpec((CH, wk), lambda b, h, c: (rows(b, h, c), ck // wk + h)),
        pl.BlockSpec((CH, wv), lambda b, h, c: (rows(b, h, c), cv // wv + h)),
        pl.BlockSpec((CH, wv), lambda b, h, c: (rows(b, h, c), co // wv + h)),
        pl.BlockSpec((CH, LANES), lambda b, h, c: (rows(b, h, c), 0)),
        pl.BlockSpec((HB, LANES, dk), lambda b, h, c: (h, 0, 0)),
        pl.BlockSpec((HB, 1, dk), lambda b, h, c: (h, 0, 0)),
        pl.BlockSpec((HB, 1, dv), lambda b, h, c: (h, 0, 0)),
        pl.BlockSpec((None, HB, dk, dv), lambda b, h, c: (b, h, 0, 0)),
    ]
    out_specs = [
        pl.BlockSpec((CH, wv), lambda b, h, c: (b * nC + c, h)),
        pl.BlockSpec((None, HB, dk, dv), lambda b, h, c: (b, h, 0, 0)),
    ]
    out_shape = [jax.ShapeDtypeStruct((B * T, H * dv), BF16), jax.ShapeDtypeStruct((B, H, dk, dv), F32)]
    return pl.pallas_call(
        functools.partial(_gla_kernel, SB=SB, HB=HB, dk=dk, dv=dv, n_chunks=nC),
        grid=(B, H // HB, nC),
        in_specs=in_specs, out_specs=out_specs, out_shape=out_shape,
        scratch_shapes=[pltpu.VMEM((HB, dv, dk), F32)],
        compiler_params=_cparams(("parallel", "parallel", "arbitrary")),
        name="gla",
    )(proj, proj, proj, proj, small, w2pad, b_gk.reshape(H, 1, dk), norm.reshape(H, 1, dv), s0)


def _hgrn_kernel(q_ref, f_ref, i_ref, og_ref, lg_ref, nrm_ref, s0_ref, o_ref, s_ref, st_scr,
                 *, SB, HB, dk, layer, n_chunks):
    c = pl.program_id(2)
    lg = lg_ref[...]
    e = jnp.exp(lg - jnp.max(lg, axis=0, keepdims=True))
    sm = e / jnp.sum(e, axis=0, keepdims=True)
    lb_all = jnp.sum(sm[1:layer + 1], axis=0, keepdims=True) if layer > 0 else jnp.zeros_like(sm[0:1])
    @pl.when(c == 0)
    def _():
        for hb in range(HB):
            st_scr[hb] = s0_ref[hb].T

    for hb in range(HB):
        sl = slice(hb * dk, (hb + 1) * dk)
        lb = lb_all[:, sl]
        z = f_ref[:, sl]
        g = jnp.log(lb + (1.0 - lb) * _sigmoid(z))
        key = (1.0 - lb) * _sigmoid(-z)
        cq = q_ref[:, sl]
        o, st_new = _scan_chunk(cq * _sigmoid(cq), key, i_ref[:, sl], g, st_scr[hb], SB=SB)
        st_scr[hb] = st_new
        og = og_ref[:, sl]
        o_ref[:, sl] = _head_out(o, nrm_ref[hb], og * _sigmoid(og)).astype(o_ref.dtype)

    @pl.when(c == n_chunks - 1)
    def _():
        for hb in range(HB):
            s_ref[hb] = st_scr[hb].T


def hgrn_mixer(proj, lb_logits, norm, s0, *, B, T, row_off, cols, H, dk, dv, CH, SB, HB, layer):
    assert dk == dv
    nC = T // CH
    rb = row_off // CH
    cq, cf, ci, co = cols
    depth = lb_logits.shape[0]
    wb = HB * dk

    def col(c0):
        return pl.BlockSpec((CH, wb), lambda b, h, c: (rb + b * nC + c, c0 // wb + h))

    in_specs = [
        col(cq), col(cf), col(ci), col(co),
        pl.BlockSpec((depth, wb), lambda b, h, c: (0, h)),
        pl.BlockSpec((HB, 1, dv), lambda b, h, c: (h, 0, 0)),
        pl.BlockSpec((None, HB, dk, dv), lambda b, h, c: (b, h, 0, 0)),
    ]
    out_specs = [
        pl.BlockSpec((CH, wb), lambda b, h, c: (b * nC + c, h)),
        pl.BlockSpec((None, HB, dk, dv), lambda b, h, c: (b, h, 0, 0)),
    ]
    out_shape = [jax.ShapeDtypeStruct((B * T, H * dv), BF16), jax.ShapeDtypeStruct((B, H, dk, dv), F32)]
    return pl.pallas_call(
        functools.partial(_hgrn_kernel, SB=SB, HB=HB, dk=dk, layer=layer, n_chunks=nC),
        grid=(B, H // HB, nC),
        in_specs=in_specs, out_specs=out_specs, out_shape=out_shape,
        scratch_shapes=[pltpu.VMEM((HB, dv, dk), F32)],
        compiler_params=_cparams(("parallel", "parallel", "arbitrary")),
        name="hgrn",
    )(proj, proj, proj, proj, lb_logits, norm.reshape(H, 1, dv), s0)


def _fox_cumsum_kernel(x_ref, b_ref, lf_ref, c_ref):
    lf = _log_sigmoid(x_ref[...] + b_ref[...])
    lf_ref[...] = lf
    c_ref[...] = _cumsum_lanes(lf, 1)


def fox_cumsum(df_t, bias):
    B, H, T = df_t.shape
    spec = pl.BlockSpec((None, H, T), lambda b: (b, 0, 0))
    return pl.pallas_call(
        _fox_cumsum_kernel,
        grid=(B,),
        in_specs=[spec, pl.BlockSpec((H, 1), lambda b: (0, 0))],
        out_specs=[spec, spec],
        out_shape=[jax.ShapeDtypeStruct((B, H, T), F32)] * 2,
        compiler_params=_cparams(("parallel",)),
        name="fox_cumsum",
    )(df_t, bias.reshape(H, 1))


def _fox_prompt_kernel(q_ref, k_ref, v_ref, og_ref, cr_ref, o_ref, *, TQ, T, dh):
    kb = k_ref[...].astype(BF16)
    vb = v_ref[...].astype(BF16)
    c_row = cr_ref[...]
    for i in range(T // TQ):
        lo, hi = i * TQ, (i + 1) * TQ
        q = (q_ref[lo:hi, :] * (dh ** -0.5)).astype(BF16)
        s = _nt(q, kb[:hi]) - c_row[:, :hi]
        qpos = lo + lax.broadcasted_iota(jnp.int32, (TQ, hi), 0)
        kpos = lax.broadcasted_iota(jnp.int32, (TQ, hi), 1)
        s = jnp.where(kpos <= qpos, s, NEG)
        p = jnp.exp(s - jnp.max(s, axis=1, keepdims=True))
        l = jnp.sum(p, axis=1, keepdims=True)
        o = _nn(p.astype(BF16), vb[:hi]) / l
        o_ref[lo:hi, :] = (o * _sigmoid(og_ref[lo:hi, :])).astype(o_ref.dtype)


def fox_prompt(proj, c_row, *, B, T, row_off, cols, H, dh, TQ):
    cq, ck, cv, co = cols
    rb = row_off // T

    def col(c0):
        return pl.BlockSpec((T, dh), lambda b, h: (rb + b, c0 // dh + h))

    in_specs = [col(cq), col(ck), col(cv), col(co),
                pl.BlockSpec((None, None, 1, T), lambda b, h: (b, h, 0, 0))]
    return pl.pallas_call(
        functools.partial(_fox_prompt_kernel, TQ=TQ, T=T, dh=dh),
        grid=(B, H),
        in_specs=in_specs,
        out_specs=pl.BlockSpec((T, dh), lambda b, h: (b, h)),
        out_shape=jax.ShapeDtypeStruct((B * T, H * dh), BF16),
        compiler_params=_cparams(("parallel", "parallel")),
        name="fox_prompt",
    )(proj, proj, proj, proj, c_row)


def _fox_sample_kernel(pt_ref, q_ref, kn_ref, vn_ref, og_ref, df_ref, fb_ref, *rest, PP, T, H, dh, n_steps):
    k_refs = rest[0:PP]
    v_refs = rest[PP:2 * PP]
    f_refs = rest[2 * PP:3 * PP]
    o_ref, lf_ref = rest[3 * PP], rest[3 * PP + 1]
    qall, acc, m_scr, l_scr, carry, crow, hmask = rest[3 * PP + 2:]
    s_id = pl.program_id(1)
    R = T * H
    W = PAGE * H

    @pl.when(s_id == 0)
    def _():
        qall[...] = (q_ref[...].reshape(R, dh) * (dh ** -0.5)).astype(BF16)
        acc[...] = jnp.zeros_like(acc)
        m_scr[...] = jnp.full_like(m_scr, NEG)
        l_scr[...] = jnp.zeros_like(l_scr)
        carry[...] = jnp.zeros_like(carry)
        lf_new = _log_sigmoid(df_ref[...] + fb_ref[...])
        lf_ref[...] = lf_new
        crow[...] = _cumsum_lanes(lf_new, H)
        rr = lax.broadcasted_iota(jnp.int32, (R, W), 0)
        ll = lax.broadcasted_iota(jnp.int32, (R, W), 1)
        hmask[...] = jnp.where((rr % H) == (ll % H), 0.0, NEG)

    lf = jnp.concatenate([f_refs[p][...] for p in range(PP)], axis=0)
    lane = lax.broadcasted_iota(jnp.int32, (PP, W), 1)
    suf = lf
    tot = lf
    s = H
    while s < W:
        suf = suf + jnp.where(lane + s < W, pltpu.roll(suf, W - s, 1), 0.0)
        tot = tot + pltpu.roll(tot, s, 1)
        s *= 2
    tot_cum = _cumsum_rows(tot)
    later = carry[...]
    d_all = suf - lf + (tot_cum - tot) + later
    carry[...] = later + tot_cum[PP - 1:PP, :]

    qa = qall[...]
    hm = hmask[...]
    m_old = m_scr[...]
    l_old = l_scr[...]
    a_old = acc[...]
    for p in range(PP):
        kb = k_refs[p][...].astype(BF16)
        vb = v_refs[p][...].astype(BF16)
        st = _nt(qa, kb) + (hm + d_all[p:p + 1, :])
        m_new = jnp.maximum(m_old, jnp.max(st, axis=1, keepdims=True))
        alpha = jnp.exp(m_old - m_new)
        pm = jnp.exp(st - m_new)
        l_old = alpha * l_old + jnp.sum(pm, axis=1, keepdims=True)
        a_old = alpha * a_old + _nn(pm.astype(BF16), vb)
        m_old = m_new
    m_scr[...] = m_old
    l_scr[...] = l_old
    acc[...] = a_old

    @pl.when(s_id == n_steps - 1)
    def _():
        knb = kn_ref[...].reshape(R, dh).astype(BF16)
        vnb = vn_ref[...].reshape(R, dh).astype(BF16)
        r2 = lax.broadcasted_iota(jnp.int32, (R, R), 0)
        l2 = lax.broadcasted_iota(jnp.int32, (R, R), 1)
        ok = ((r2 % H) == (l2 % H)) & (l2 // H <= r2 // H)
        sn = jnp.where(ok, _nt(qa, knb) - crow[...], NEG)
        m_fin = jnp.maximum(m_old, jnp.max(sn, axis=1, keepdims=True))
        a2 = jnp.exp(m_old - m_fin)
        pn = jnp.exp(sn - m_fin)
        l_fin = a2 * l_old + jnp.sum(pn, axis=1, keepdims=True)
        out = (a2 * a_old + _nn(pn.astype(BF16), vnb)) / l_fin
        out = out * _sigmoid(og_ref[...].reshape(R, dh))
        o_ref[...] = out.reshape(T, H, dh).astype(o_ref.dtype)


def fox_sample(proj3, df_flat, fb_flat, cache_k, cache_v, cache_f, page_table, *, B, T, page_off, cols, H, dh, PP):
    R = T * H
    W = PAGE * H
    n_pages = page_table.shape[1]
    n_steps = n_pages // PP
    cq, ck, cv, co = cols

    def rowspec(c0):
        return pl.BlockSpec((T, H, dh), lambda b, s, pt: (b, c0 // H, 0))

    def page(p):
        return lambda b, s, pt: (page_off + pt[b, n_pages - 1 - (s * PP + p)], 0, 0)

    in_specs = [rowspec(cq), rowspec(ck), rowspec(cv), rowspec(co),
                pl.BlockSpec((None, 1, R), lambda b, s, pt: (b, 0, 0)),
                pl.BlockSpec((1, R), lambda b, s, pt: (0, 0))]
    in_specs += [pl.BlockSpec((None, W, dh), page(p)) for p in range(PP)]
    in_specs += [pl.BlockSpec((None, W, dh), page(p)) for p in range(PP)]
    in_specs += [pl.BlockSpec((None, 1, W), page(p)) for p in range(PP)]
    grid_spec = pltpu.PrefetchScalarGridSpec(
        num_scalar_prefetch=1,
        grid=(B, n_steps),
        in_specs=in_specs,
        out_specs=[pl.BlockSpec((T, H, dh), lambda b, s, pt: (b, 0, 0)),
                   pl.BlockSpec((None, 1, R), lambda b, s, pt: (b, 0, 0))],
        scratch_shapes=[
            pltpu.VMEM((R, dh), BF16),
            pltpu.VMEM((R, dh), F32),
            pltpu.VMEM((R, 1), F32),
            pltpu.VMEM((R, 1), F32),
            pltpu.VMEM((1, W), F32),
            pltpu.VMEM((1, R), F32),
            pltpu.VMEM((R, W), F32),
        ],
    )
    return pl.pallas_call(
        functools.partial(_fox_sample_kernel, PP=PP, T=T, H=H, dh=dh, n_steps=n_steps),
        grid_spec=grid_spec,
        out_shape=[jax.ShapeDtypeStruct((B * T, H, dh), BF16), jax.ShapeDtypeStruct((B, 1, R), F32)],
        compiler_params=_cparams(("parallel", "arbitrary")),
        name="fox_sample",
    )(page_table, proj3, proj3, proj3, proj3, df_flat, fb_flat,
      *([cache_k] * PP), *([cache_v] * PP), *([cache_f] * PP))


def _pad_lanes(w):
    return jnp.pad(w, ((0, 0), (0, LANES - w.shape[1])))


def _row_tile(m, cap):
    best = None
    for t in range(BF16_ROWS, min(m, cap) + 1, BF16_ROWS):
        if m % t == 0:
            best = t
    assert best is not None, (m, cap)
    return best


def _col_tile(n, cap):
    t = cap
    while n % t:
        t //= 2
    return t


def kernel(x_prompt, x_sample, state_mlstm_C, state_mlstm_n, state_mlstm_m, state_gla_S, state_hgrn_S, cache_fox_k, cache_fox_v, cache_fox_logf, page_table, norm_mix_pre, norm_mix_post, norm_ffn_pre, norm_ffn_post, ffn_w_gate, ffn_w_up, ffn_w_down, w_in_even, w_out_even, mlstm_b_i, mlstm_b_f, mlstm_norm, gla_w_gk2, gla_b_gk, gla_norm, w_in_odd, w_out_odd, hgrn_lb_logits, hgrn_norm, fox_b_f):
    bp, tp, D = x_prompt.shape
    bs, ts, _ = x_sample.shape
    mp, ms = bp * tp, bs * ts
    M = mp + ms
    depth = norm_mix_pre.shape[0]
    a_heads, a_dk, a_dv = state_mlstm_C.shape[2:]
    b_heads, b_dk, b_dv = state_gla_S.shape[2:]
    c_heads, c_dk, c_dv = state_hgrn_S.shape[2:]
    d_heads = fox_b_f.shape[1]
    half = D // 2
    d_dh = half // d_heads
    ff = ffn_w_gate.shape[3]
    n_phys = cache_fox_k.shape[1]

    tm = _row_tile(M, 1408)
    tr = _row_tile(M, 256)
    def ffn(h, layer, j):
        u, wd = ffn_up(h, ffn_w_gate, ffn_w_up, ffn_w_down, (layer, j), tm=tm, tn=_col_tile(ff, 256))
        return matmul([(u, wd, (), 0)], out_dtype=F32, tm=tm, tn=_col_tile(D, 512), nk=2)

    groups = ((bp, tp, 0), (bs, ts, mp))
    outs = {k: ([], []) for k in ('c', 'n', 'm', 'g', 'h', 'k', 'v', 'f')}

    x = jnp.concatenate([x_prompt.reshape(mp, D), x_sample.reshape(ms, D)], axis=0)
    h = prenorm(x, norm_ffn_pre[0, 0], tr=tr)
    for layer in range(depth):
        y = ffn(h, layer, 0)
        x, h = resid_norm(x, y, norm_ffn_post[layer, 0], norm_mix_pre[layer], scale=0.5, tr=tr)
        if layer % 2 == 0:
            e = layer // 2
            wt = jnp.swapaxes(w_in_even, 1, 2)
            n_a = a_heads * (2 * a_dk + 2 * a_dv)
            n_b = b_heads * (2 * b_dk + 2 * b_dv)
            o_b = n_a + 2 * a_heads
            wt_small = jnp.concatenate([wt[e, n_a:o_b], wt[e, o_b + n_b:]], axis=0)
            wt_small = jnp.pad(wt_small, ((0, LANES - wt_small.shape[0]), (0, 0)))
            proj_a = matmul_nt(h, wt, (e,), n=n_a, out_dtype=F32, tm=tm, tn=_col_tile(n_a, 512))
            proj_b = matmul_nt(h, wt[e, o_b:o_b + n_b], (), n=n_b, out_dtype=F32, tm=tm, tn=_col_tile(n_b, 512))
            small = matmul_nt(h, wt_small, (), n=LANES, out_dtype=F32, tm=tm, tn=LANES)
            gate_bias = _pad_lanes(jnp.concatenate([mlstm_b_i[e], mlstm_b_f[e]])[None, :])
            a_cols = (0, a_heads * a_dk, 2 * a_heads * a_dk, 2 * a_heads * a_dk + a_heads * a_dv)
            w2 = gla_w_gk2[e]
            w2pad = jnp.zeros((LANES, b_heads * b_dk), F32).at[2 * a_heads:2 * a_heads + w2.shape[0]].set(w2)
            w2pad = w2pad.reshape(LANES, b_heads, b_dk).transpose(1, 0, 2).astype(BF16)
            b_cols = (0, b_heads * b_dk, 2 * b_heads * b_dk, 2 * b_heads * b_dk + b_heads * b_dv)
            ha, hb = [], []
            for gi, (B, T, r0) in enumerate(groups):
                if gi == 0:
                    c0, n0, m0, s0 = (jnp.zeros((B,) + s.shape[2:], F32)
                                      for s in (state_mlstm_C, state_mlstm_n, state_mlstm_m, state_gla_S))
                else:
                    c0, n0, m0, s0 = state_mlstm_C[e], state_mlstm_n[e], state_mlstm_m[e], state_gla_S[e]
                CH = min(T, 128)
                ha_g, c1, n1, m1 = mlstm_mixer(proj_a, small, gate_bias, mlstm_norm[e], c0, n0, m0,
                                               B=B, T=T, row_off=r0, cols=a_cols,
                                               H=a_heads, dk=a_dk, dv=a_dv, L=CH, HB=2)
                hb_g, s1 = gla_mixer(proj_b, small, w2pad, gla_b_gk[e], gla_norm[e], s0,
                                     B=B, T=T, row_off=r0, cols=b_cols,
                                     H=b_heads, dk=b_dk, dv=b_dv, CH=CH, SB=min(CH, 32), HB=2)
                ha.append(ha_g); hb.append(hb_g)
                outs['c'][gi].append(c1); outs['n'][gi].append(n1)
                outs['m'][gi].append(m1); outs['g'][gi].append(s1)
            wo = w_out_even.astype(BF16)
            mix = matmul([(jnp.concatenate(ha, axis=0), wo, (e,), 0), (jnp.concatenate(hb, axis=0), wo, (e,), 1)],
                         out_dtype=F32, tm=tm, tn=_col_tile(D, 512))
        else:
            o = layer // 2
            n_main = 4 * c_heads * c_dk + 4 * half
            wt = jnp.swapaxes(w_in_odd, 1, 2)
            wt_small = wt[o, n_main:]
            wt_small = jnp.pad(wt_small, ((0, LANES - wt_small.shape[0]), (0, 0)))
            proj = matmul_nt(h, wt, (o,), n=n_main, out_dtype=F32, tm=tm, tn=_col_tile(n_main, 512))
            small = matmul_nt(h, wt_small, (), n=LANES, out_dtype=F32, tm=tm, tn=LANES)
            cw = c_heads * c_dk
            dq = 4 * cw
            d_cols = (dq, dq + half, dq + 2 * half, dq + 3 * half)
            hc, hd = [], []
            for gi, (B, T, r0) in enumerate(groups):
                s0 = jnp.zeros((B,) + state_hgrn_S.shape[2:], F32) if gi == 0 else state_hgrn_S[o]
                CH = min(T, 128)
                hc_g, s1 = hgrn_mixer(proj, hgrn_lb_logits, hgrn_norm[o], s0,
                                      B=B, T=T, row_off=r0, cols=(0, cw, 2 * cw, 3 * cw),
                                      H=c_heads, dk=c_dk, dv=c_dv, CH=CH, SB=min(CH, 32), HB=8, layer=layer)
                df = small[r0:r0 + B * T, :d_heads]
                if gi == 0:
                    lf_t, c_t = fox_cumsum(df.reshape(B, T, d_heads).transpose(0, 2, 1), fox_b_f[o])
                    logf = lf_t.transpose(0, 2, 1)
                    hd_g = fox_prompt(proj, c_t.reshape(B, d_heads, 1, T), B=B, T=T, row_off=r0,
                                      cols=d_cols, H=d_heads, dh=d_dh, TQ=min(T, 256))
                else:
                    n_cb = proj.shape[1] // d_dh
                    proj3 = proj[r0:r0 + B * T].reshape(B * T, n_cb, d_dh)
                    n_odd = cache_fox_k.shape[0]
                    hd3, lf = fox_sample(
                        proj3, df.reshape(B, 1, T * d_heads), jnp.tile(fox_b_f[o], T)[None, :],
                        cache_fox_k.reshape(n_odd * n_phys, PAGE * d_heads, d_dh),
                        cache_fox_v.reshape(n_odd * n_phys, PAGE * d_heads, d_dh),
                        cache_fox_logf.reshape(n_odd * n_phys, 1, PAGE * d_heads),
                        page_table, B=B, T=T, page_off=o * n_phys,
                        cols=tuple(c // d_dh for c in d_cols), H=d_heads, dh=d_dh, PP=8)
                    hd_g = hd3.reshape(B * T, half)
                    logf = lf.reshape(B, T, d_heads)
                hc.append(hc_g); hd.append(hd_g)
                outs['h'][gi].append(s1)
                k_new, v_new = split_heads(proj, rows=B * T, row_off=r0, ck=d_cols[1], cv=d_cols[2],
                                           H=d_heads, dh=d_dh, tr=min(B * T, 256))
                outs['k'][gi].append(k_new.reshape(B, T, d_heads, d_dh))
                outs['v'][gi].append(v_new.reshape(B, T, d_heads, d_dh))
                outs['f'][gi].append(logf)
            wo = w_out_odd.astype(BF16)
            mix = matmul([(jnp.concatenate(hc, axis=0), wo, (o,), 0), (jnp.concatenate(hd, axis=0), wo, (o,), 1)],
                         out_dtype=F32, tm=tm, tn=_col_tile(D, 512))
        x, h = resid_norm(x, mix, norm_mix_post[layer], norm_ffn_pre[layer, 1], scale=1.0, tr=tr)
        y = ffn(h, layer, 1)
        if layer + 1 < depth:
            x, h = resid_norm(x, y, norm_ffn_post[layer, 1], norm_ffn_pre[layer + 1, 0], scale=0.5, tr=tr)
        else:
            y_p, y_s = resid_split(x, y, norm_ffn_post[layer, 1], scale=0.5, rows_a=mp, tr=math.gcd(mp, ms))

    names = ('c', 'n', 'm', 'g', 'h', 'k', 'v', 'f')
    res_p = tuple(jnp.stack(outs[k][0]) for k in names)
    res_s = tuple(jnp.stack(outs[k][1]) for k in names)
    return (y_p.reshape(bp, tp, D), y_s.reshape(bs, ts, D)) + res_p + res_s
```

```python
import functools
import math

import jax
import jax.numpy as jnp
from jax import lax
from jax.experimental import pallas as pl
from jax.experimental.pallas import tpu as pltpu

F32 = jnp.float32
BF16 = jnp.bfloat16
EPS = 1e-6
NEG = -1e30
LANES = 128
BF16_ROWS = 16
VMEM_LIMIT = 56 * 1024 * 1024

B_GATE_NORMALIZER = 16.0
PAGE = 128


def _cparams(sem):
    return pltpu.CompilerParams(dimension_semantics=sem, vmem_limit_bytes=VMEM_LIMIT)


def _log_sigmoid(x):
    return jnp.minimum(x, 0.0) - jnp.log1p(jnp.exp(-jnp.abs(x)))


def _sigmoid(x):
    return jax.nn.sigmoid(x)


def _nt(a, b):
    return lax.dot_general(a, b, (((1,), (1,)), ((), ())), preferred_element_type=F32)


def _tn(a, b):
    return lax.dot_general(a, b, (((0,), (0,)), ((), ())), preferred_element_type=F32)


def _nn(a, b):
    return jnp.dot(a, b, preferred_element_type=F32)


def _cumsum_rows(x):
    n = x.shape[0]
    row = lax.broadcasted_iota(jnp.int32, x.shape, 0)
    s = 1
    while s < n:
        x = x + jnp.where(row >= s, pltpu.roll(x, s, 0), 0.0)
        s *= 2
    return x


def _cumsum_lanes(x, step):
    n = x.shape[1]
    lane = lax.broadcasted_iota(jnp.int32, x.shape, 1)
    s = step
    while s < n:
        x = x + jnp.where(lane >= s, pltpu.roll(x, s, 1), 0.0)
        s *= 2
    return x


def _prenorm_join_kernel(xa_ref, xb_ref, g_ref, xo_ref, ho_ref, *, n_a):
    x = jnp.where(pl.program_id(0) < n_a, xa_ref[...], xb_ref[...])
    xo_ref[...] = x
    r = lax.rsqrt(jnp.mean(x * x, axis=-1, keepdims=True) + EPS)
    ho_ref[...] = (x * r * g_ref[...]).astype(ho_ref.dtype)


def prenorm_join(xa, xb, g, *, tr):
    (ma, d), mb = xa.shape, xb.shape[0]
    n_a = ma // tr
    row = pl.BlockSpec((tr, d), lambda i: (i, 0))
    return pl.pallas_call(
        functools.partial(_prenorm_join_kernel, n_a=n_a),
        grid=((ma + mb) // tr,),
        in_specs=[pl.BlockSpec((tr, d), lambda i: (jnp.minimum(i, n_a - 1), 0)),
                  pl.BlockSpec((tr, d), lambda i: (jnp.maximum(i - n_a, 0), 0)),
                  pl.BlockSpec((1, d), lambda i: (0, 0))],
        out_specs=[row, row],
        out_shape=[jax.ShapeDtypeStruct((ma + mb, d), F32), jax.ShapeDtypeStruct((ma + mb, d), BF16)],
        compiler_params=_cparams(("parallel",)),
        name="prenorm_join",
    )(xa, xb, g.reshape(1, d))


def _resid_kernel(x_ref, y_ref, gp_ref, gn_ref, xo_ref, ho_ref, *, scale):
    y = y_ref[...].astype(F32)
    r = lax.rsqrt(jnp.mean(y * y, axis=-1, keepdims=True) + EPS)
    xn = x_ref[...] + scale * (y * r * gp_ref[...])
    xo_ref[...] = xn
    r2 = lax.rsqrt(jnp.mean(xn * xn, axis=-1, keepdims=True) + EPS)
    ho_ref[...] = (xn * r2 * gn_ref[...]).astype(ho_ref.dtype)


def resid_norm(x, y, g_post, g_next, *, scale, tr):
    m, d = x.shape
    row = pl.BlockSpec((tr, d), lambda i: (i, 0))
    vec = pl.BlockSpec((1, d), lambda i: (0, 0))
    return pl.pallas_call(
        functools.partial(_resid_kernel, scale=scale),
        grid=(m // tr,),
        in_specs=[row, row, vec, vec],
        out_specs=[row, row],
        out_shape=[jax.ShapeDtypeStruct((m, d), F32), jax.ShapeDtypeStruct((m, d), BF16)],
        compiler_params=_cparams(("parallel",)),
        name="resid_norm",
    )(x, y, g_post.reshape(1, d), g_next.reshape(1, d))


def _resid_split_kernel(x_ref, y_ref, gp_ref, xa_ref, xb_ref, *, scale, n_a):
    i = pl.program_id(0)
    y = y_ref[...].astype(F32)
    r = lax.rsqrt(jnp.mean(y * y, axis=-1, keepdims=True) + EPS)
    xn = x_ref[...] + scale * (y * r * gp_ref[...])

    @pl.when(i < n_a)
    def _():
        xa_ref[...] = xn

    @pl.when(i >= n_a)
    def _():
        xb_ref[...] = xn


def resid_split(x, y, g_post, *, scale, rows_a, tr):
    m, d = x.shape
    n_a = rows_a // tr
    row = pl.BlockSpec((tr, d), lambda i: (i, 0))
    return pl.pallas_call(
        functools.partial(_resid_split_kernel, scale=scale, n_a=n_a),
        grid=(m // tr,),
        in_specs=[row, row, pl.BlockSpec((1, d), lambda i: (0, 0))],
        out_specs=[pl.BlockSpec((tr, d), lambda i: (jnp.minimum(i, n_a - 1), 0)),
                   pl.BlockSpec((tr, d), lambda i: (jnp.maximum(i - n_a, 0), 0))],
        out_shape=[jax.ShapeDtypeStruct((rows_a, d), F32), jax.ShapeDtypeStruct((m - rows_a, d), F32)],
        compiler_params=_cparams(("arbitrary",)),
        name="resid_split",
    )(x, y, g_post.reshape(1, d))


def _split_heads_kernel(k_ref, v_ref, ko_ref, vo_ref, *, H, dh):
    for h in range(H):
        ko_ref[:, h, :] = k_ref[:, h * dh:(h + 1) * dh]
        vo_ref[:, h, :] = v_ref[:, h * dh:(h + 1) * dh]


def split_heads(proj, *, rows, row_off, ck, cv, H, dh, tr):
    w = H * dh
    out = jax.ShapeDtypeStruct((rows, H, dh), proj.dtype)
    ospec = pl.BlockSpec((tr, H, dh), lambda i: (i, 0, 0))
    return pl.pallas_call(
        functools.partial(_split_heads_kernel, H=H, dh=dh),
        grid=(rows // tr,),
        in_specs=[pl.BlockSpec((tr, w), lambda i: (row_off // tr + i, ck // w)),
                  pl.BlockSpec((tr, w), lambda i: (row_off // tr + i, cv // w))],
        out_specs=[ospec, ospec],
        out_shape=[out, out],
        compiler_params=_cparams(("parallel",)),
        name="split_heads",
    )(proj, proj)


def _mm_kernel(*refs, n_pairs, nk):
    o_ref = refs[2 * n_pairs]
    acc = _nn(refs[0][...], refs[1][...])
    for p in range(1, n_pairs):
        acc = acc + _nn(refs[2 * p][...], refs[2 * p + 1][...])
    if nk == 1:
        o_ref[...] = acc.astype(o_ref.dtype)
    else:
        acc_ref = refs[2 * n_pairs + 1]
        k = pl.program_id(2)

        @pl.when(k == 0)
        def _():
            acc_ref[...] = acc

        @pl.when((k > 0) & (k < nk - 1))
        def _():
            acc_ref[...] += acc

        @pl.when(k == nk - 1)
        def _():
            o_ref[...] = (acc_ref[...] + acc).astype(o_ref.dtype)


def matmul(pairs, *, out_dtype, tm, tn, nk=1):
    m = pairs[0][0].shape[0]
    n = pairs[0][1].shape[-1]
    in_specs, args = [], []
    for x, w, lead, kb0 in pairs:
        tk = x.shape[1] // nk
        in_specs.append(pl.BlockSpec((tm, tk), lambda i, j, k: (i, k)))
        in_specs.append(pl.BlockSpec((None,) * len(lead) + (tk, tn),
                                     lambda i, j, k, lead=lead, kb0=kb0, nk=nk: lead + (kb0 * nk + k, j)))
        args += [x, w]
    return pl.pallas_call(
        functools.partial(_mm_kernel, n_pairs=len(pairs), nk=nk),
        grid=(m // tm, n // tn, nk),
        in_specs=in_specs,
        out_specs=pl.BlockSpec((tm, tn), lambda i, j, k: (i, j)),
        out_shape=jax.ShapeDtypeStruct((m, n), out_dtype),
        scratch_shapes=[pltpu.VMEM((tm, tn), F32)] if nk > 1 else [],
        compiler_params=_cparams(("parallel", "parallel", "arbitrary")),
        name="matmul",
    )(*args)


def _mm_nt_kernel(x_ref, wt_ref, o_ref, w_bf):
    @pl.when(pl.program_id(1) == 0)
    def _():
        w_bf[...] = wt_ref[...].astype(BF16)

    o_ref[...] = _nt(x_ref[...], w_bf[...]).astype(o_ref.dtype)


def matmul_nt(x, wt, lead, *, n, out_dtype, tm, tn):
    m, k = x.shape
    return pl.pallas_call(
        _mm_nt_kernel,
        grid=(n // tn, m // tm),
        in_specs=[pl.BlockSpec((tm, k), lambda j, i: (i, 0)),
                  pl.BlockSpec((None,) * len(lead) + (tn, k), lambda j, i: lead + (j, 0))],
        out_specs=pl.BlockSpec((tm, tn), lambda j, i: (i, j)),
        out_shape=jax.ShapeDtypeStruct((m, n), out_dtype),
        scratch_shapes=[pltpu.VMEM((tn, k), BF16)],
        compiler_params=_cparams(("parallel", "arbitrary")),
        name="matmul_nt",
    )(x, wt)


def _ffn_up_kernel(h_ref, wg_ref, wu_ref, o_ref, wg_bf, wu_bf):
    @pl.when(pl.program_id(1) == 0)
    def _():
        wg_bf[...] = wg_ref[...].astype(BF16)
        wu_bf[...] = wu_ref[...].astype(BF16)

    h = h_ref[...]
    g = _nn(h, wg_bf[...])
    u = _nn(h, wu_bf[...])
    o_ref[...] = (g * _sigmoid(g) * u).astype(o_ref.dtype)


def ffn_up(h, wg, wu, lead, *, tm, tn):
    m, k = h.shape
    n = wg.shape[-1]
    wspec = pl.BlockSpec((None,) * len(lead) + (k, tn), lambda j, i: lead + (0, j))
    return pl.pallas_call(
        _ffn_up_kernel,
        grid=(n // tn, m // tm),
        in_specs=[pl.BlockSpec((tm, k), lambda j, i: (i, 0)), wspec, wspec],
        out_specs=pl.BlockSpec((tm, tn), lambda j, i: (i, j)),
        out_shape=jax.ShapeDtypeStruct((m, n), BF16),
        scratch_shapes=[pltpu.VMEM((k, tn), BF16), pltpu.VMEM((k, tn), BF16)],
        compiler_params=_cparams(("parallel", "arbitrary")),
        name="ffn_up",
    )(h, wg, wu)


def _mlstm_kernel(q_ref, k_ref, v_ref, og_ref, gt_ref, gb_ref, nrm_ref, c0_ref, n0_ref, m0_ref,
                  h_ref, c_ref, n_ref, m_ref, *, L, HB, dk, dv, n_heads):
    @pl.when(pl.program_id(2) == 0)
    def _():
        c_ref[...] = c0_ref[...]
        n_ref[...] = n0_ref[...]
        m_ref[...] = m0_ref[...]

    gates = gt_ref[...] + gb_ref[...]
    lane = lax.broadcasted_iota(jnp.int32, gates.shape, 1)
    r = lax.broadcasted_iota(jnp.int32, (L, L), 0)
    s = lax.broadcasted_iota(jnp.int32, (L, L), 1)
    tril = r >= s
    eye = r == s

    def col2row(col):
        return jnp.sum(jnp.where(eye, col, 0.0), axis=0, keepdims=True)

    for hb in range(HB):
        hh = pl.program_id(1) * HB + hb
        ks = slice(hb * dk, (hb + 1) * dk)
        vs = slice(hb * dv, (hb + 1) * dv)
        i_col = jnp.sum(jnp.where(lane == hh, gates, 0.0), axis=1, keepdims=True)
        f_col = jnp.sum(jnp.where(lane == hh + n_heads, gates, 0.0), axis=1, keepdims=True)
        logf = _log_sigmoid(f_col)
        logf_row = col2row(logf)
        b_col = jnp.sum(jnp.where(tril, logf_row, 0.0), axis=1, keepdims=True)
        b_row = col2row(b_col)
        i_row = col2row(i_col)
        m_prev = m_ref[hb]
        dmat = jnp.where(tril, b_col - b_row + i_row, -jnp.inf)
        from_state = b_col + m_prev
        m_t = jnp.maximum(from_state, jnp.max(dmat, axis=1, keepdims=True))

        q = q_ref[:, ks]
        k = k_ref[:, ks] * (dk ** -0.5)
        qb = q.astype(BF16)
        vb = v_ref[:, vs].astype(BF16)
        w = jnp.exp(dmat - m_t) * _nt(qb, k.astype(BF16))
        s_state = jnp.exp(from_state - m_t)
        c_old = c_ref[hb]
        n_old = n_ref[hb]
        num = s_state * _nn(qb, c_old.astype(BF16)) + _nn(w.astype(BF16), vb)
        den = s_state * jnp.sum(q * n_old, axis=1, keepdims=True) + jnp.sum(w, axis=1, keepdims=True)
        h = num / jnp.maximum(jnp.abs(den), jnp.exp(-m_t))

        m_new = m_t[L - 1:L, :]
        b_last = b_col[L - 1:L, :]
        w_end = jnp.exp(b_last - b_col + i_col - m_new)
        s_end = jnp.exp(b_last + m_prev - m_new)
        kw = k * w_end
        c_ref[hb] = s_end * c_old + _tn(kw.astype(BF16), vb)
        n_ref[hb] = s_end * n_old + jnp.sum(kw, axis=0, keepdims=True)
        m_ref[hb] = m_new

        hn = h * lax.rsqrt(jnp.mean(h * h, axis=1, keepdims=True) + EPS) * nrm_ref[hb]
        h_ref[:, vs] = (hn * _sigmoid(og_ref[:, vs])).astype(h_ref.dtype)


def mlstm_mixer(proj, small, gate_bias, norm, c0, n0, m0, *, B, T, row_off, cols, H, dk, dv, L, HB):
    nC = T // L
    rb = row_off // L
    cq, ck, cv, co = cols
    wk, wv = HB * dk, HB * dv

    def rows(b, h, c):
        return rb + b * nC + c

    in_specs = [
        pl.BlockSpec((L, wk), lambda b, h, c: (rows(b, h, c), cq // wk + h)),
        pl.BlockSpec((L, wk), lambda b, h, c: (rows(b, h, c), ck // wk + h)),
        pl.BlockSpec((L, wv), lambda b, h, c: (rows(b, h, c), cv // wv + h)),
        pl.BlockSpec((L, wv), lambda b, h, c: (rows(b, h, c), co // wv + h)),
        pl.BlockSpec((L, LANES), lambda b, h, c: (rows(b, h, c), 0)),
        pl.BlockSpec((1, LANES), lambda b, h, c: (0, 0)),
        pl.BlockSpec((HB, 1, dv), lambda b, h, c: (h, 0, 0)),
        pl.BlockSpec((None, HB, dk, dv), lambda b, h, c: (b, h, 0, 0)),
        pl.BlockSpec((None, HB, 1, dk), lambda b, h, c: (b, h, 0, 0)),
        pl.BlockSpec((None, HB, 1, 1), lambda b, h, c: (b, h, 0, 0)),
    ]
    out_specs = [
        pl.BlockSpec((L, wv), lambda b, h, c: (b * nC + c, h)),
        pl.BlockSpec((None, HB, dk, dv), lambda b, h, c: (b, h, 0, 0)),
        pl.BlockSpec((None, HB, 1, dk), lambda b, h, c: (b, h, 0, 0)),
        pl.BlockSpec((None, HB, 1, 1), lambda b, h, c: (b, h, 0, 0)),
    ]
    out_shape = [
        jax.ShapeDtypeStruct((B * T, H * dv), BF16),
        jax.ShapeDtypeStruct((B, H, dk, dv), F32),
        jax.ShapeDtypeStruct((B, H, 1, dk), F32),
        jax.ShapeDtypeStruct((B, H, 1, 1), F32),
    ]
    h, c1, n1, m1 = pl.pallas_call(
        functools.partial(_mlstm_kernel, L=L, HB=HB, dk=dk, dv=dv, n_heads=H),
        grid=(B, H // HB, nC),
        in_specs=in_specs, out_specs=out_specs, out_shape=out_shape,
        compiler_params=_cparams(("parallel", "parallel", "arbitrary")),
        name="mlstm",
    )(proj, proj, proj, proj, small, gate_bias, norm.reshape(H, 1, dv), c0,
      n0.reshape(B, H, 1, dk), m0.reshape(B, H, 1, 1))
    return h, c1, n1.reshape(B, H, dk), m1.reshape(B, H)


def _scan_chunk(q, k, v, g, st, *, SB):
    CH = q.shape[0]
    bc = _cumsum_rows(g)
    b_last = bc[CH - 1:CH, :]
    vb = v.astype(BF16)
    o = _nt((q * jnp.exp(bc)).astype(BF16), st.astype(BF16))
    row = lax.broadcasted_iota(jnp.int32, bc.shape, 0)
    srow = lax.broadcasted_iota(jnp.int32, (SB, CH), 0)
    scol = lax.broadcasted_iota(jnp.int32, (SB, CH), 1)
    blocks = []
    for i in range(CH // SB):
        lo, hi = i * SB, (i + 1) * SB
        ref = bc[lo - 1:lo, :] if i > 0 else jnp.zeros_like(b_last)
        qi = q[lo:hi] * jnp.exp(bc[lo:hi] - ref)
        ki = k * jnp.exp(jnp.where(row < hi, ref - bc, 0.0))
        a = _nt(qi.astype(BF16), ki.astype(BF16))
        blocks.append(jnp.where(scol <= srow + lo, a, 0.0))
    a_full = blocks[0] if len(blocks) == 1 else jnp.concatenate(blocks, axis=0)
    o = o + _nn(a_full.astype(BF16), vb)
    k_dec = k * jnp.exp(b_last - bc)
    st_new = st * jnp.exp(b_last) + _tn(vb, k_dec.astype(BF16))
    return o, st_new


def _head_out(o, nrm, gate_act):
    return o * lax.rsqrt(jnp.mean(o * o, axis=1, keepdims=True) + EPS) * nrm * gate_act


def _gla_kernel(q_ref, k_ref, v_ref, og_ref, gt_ref, w2_ref, bgk_ref, nrm_ref, s0_ref,
                o_ref, s_ref, st_scr, *, SB, HB, dk, dv, n_chunks):
    c = pl.program_id(2)

    @pl.when(c == 0)
    def _():
        for hb in range(HB):
            st_scr[hb] = s0_ref[hb].T

    gates = gt_ref[...].astype(BF16)
    for hb in range(HB):
        ks = slice(hb * dk, (hb + 1) * dk)
        vs = slice(hb * dv, (hb + 1) * dv)
        gk = _nn(gates, w2_ref[hb]) + bgk_ref[hb]
        g = _log_sigmoid(gk) / B_GATE_NORMALIZER
        o, st_new = _scan_chunk(q_ref[:, ks] * (dk ** -0.5), k_ref[:, ks], v_ref[:, vs], g, st_scr[hb], SB=SB)
        st_scr[hb] = st_new
        og = og_ref[:, vs]
        o_ref[:, vs] = _head_out(o, nrm_ref[hb], og * _sigmoid(og)).astype(o_ref.dtype)

    @pl.when(c == n_chunks - 1)
    def _():
        for hb in range(HB):
            s_ref[hb] = st_scr[hb].T


def gla_mixer(proj, small, w2pad, b_gk, norm, s0, *, B, T, row_off, cols, H, dk, dv, CH, SB, HB):
    nC = T // CH
    rb = row_off // CH
    cq, ck, cv, co = cols
    wk, wv = HB * dk, HB * dv

    def rows(b, h, c):
        return rb + b * nC + c

    in_specs = [
        pl.BlockSpec((CH, wk), lambda b, h, c: (rows(b, h, c), cq // wk + h)),
        pl.BlockSpec((CH, wk), lambda b, h, c: (rows(b, h, c), ck // wk + h)),
        pl.BlockSpec((CH, wv), lambda b, h, c: (rows(b, h, c), cv // wv + h)),
        pl.BlockSpec((CH, wv), lambda b, h, c: (rows(b, h, c), co // wv + h)),
        pl.BlockSpec((CH, LANES), lambda b, h, c: (rows(b, h, c), 0)),
        pl.BlockSpec((HB, LANES, dk), lambda b, h, c: (h, 0, 0)),
        pl.BlockSpec((HB, 1, dk), lambda b, h, c: (h, 0, 0)),
        pl.BlockSpec((HB, 1, dv), lambda b, h, c: (h, 0, 0)),
        pl.BlockSpec((None, HB, dk, dv), lambda b, h, c: (b, h, 0, 0)),
    ]
    out_specs = [
        pl.BlockSpec((CH, wv), lambda b, h, c: (b * nC + c, h)),
        pl.BlockSpec((None, HB, dk, dv), lambda b, h, c: (b, h, 0, 0)),
    ]
    out_shape = [jax.ShapeDtypeStruct((B * T, H * dv), BF16), jax.ShapeDtypeStruct((B, H, dk, dv), F32)]
    return pl.pallas_call(
        functools.partial(_gla_kernel, SB=SB, HB=HB, dk=dk, dv=dv, n_chunks=nC),
        grid=(B, H // HB, nC),
        in_specs=in_specs, out_specs=out_specs, out_shape=out_shape,
        scratch_shapes=[pltpu.VMEM((HB, dv, dk), F32)],
        compiler_params=_cparams(("parallel", "parallel", "arbitrary")),
        name="gla",
    )(proj, proj, proj, proj, small, w2pad, b_gk.reshape(H, 1, dk), norm.reshape(H, 1, dv), s0)


def _hgrn_kernel(q_ref, f_ref, i_ref, og_ref, lg_ref, nrm_ref, s0_ref, o_ref, s_ref, st_scr,
                 *, SB, HB, dk, layer, n_chunks):
    c = pl.program_id(2)
    lg = lg_ref[...]
    e = jnp.exp(lg - jnp.max(lg, axis=0, keepdims=True))
    sm = e / jnp.sum(e, axis=0, keepdims=True)
    lb_all = jnp.sum(sm[1:layer + 1], axis=0, keepdims=True) if layer > 0 else jnp.zeros_like(sm[0:1])
    @pl.when(c == 0)
    def _():
        for hb in range(HB):
            st_scr[hb] = s0_ref[hb].T

    for hb in range(HB):
        sl = slice(hb * dk, (hb + 1) * dk)
        lb = lb_all[:, sl]
        z = f_ref[:, sl]
        g = jnp.log(lb + (1.0 - lb) * _sigmoid(z))
        key = (1.0 - lb) * _sigmoid(-z)
        cq = q_ref[:, sl]
        o, st_new = _scan_chunk(cq * _sigmoid(cq), key, i_ref[:, sl], g, st_scr[hb], SB=SB)
        st_scr[hb] = st_new
        og = og_ref[:, sl]
        o_ref[:, sl] = _head_out(o, nrm_ref[hb], og * _sigmoid(og)).astype(o_ref.dtype)

    @pl.when(c == n_chunks - 1)
    def _():
        for hb in range(HB):
            s_ref[hb] = st_scr[hb].T


def hgrn_mixer(proj, lb_logits, norm, s0, *, B, T, row_off, cols, H, dk, dv, CH, SB, HB, layer):
    assert dk == dv
    nC = T // CH
    rb = row_off // CH
    cq, cf, ci, co = cols
    depth = lb_logits.shape[0]
    wb = HB * dk

    def col(c0):
        return pl.BlockSpec((CH, wb), lambda b, h, c: (rb + b * nC + c, c0 // wb + h))

    in_specs = [
        col(cq), col(cf), col(ci), col(co),
        pl.BlockSpec((depth, wb), lambda b, h, c: (0, h)),
        pl.BlockSpec((HB, 1, dv), lambda b, h, c: (h, 0, 0)),
        pl.BlockSpec((None, HB, dk, dv), lambda b, h, c: (b, h, 0, 0)),
    ]
    out_specs = [
        pl.BlockSpec((CH, wb), lambda b, h, c: (b * nC + c, h)),
        pl.BlockSpec((None, HB, dk, dv), lambda b, h, c: (b, h, 0, 0)),
    ]
    out_shape = [jax.ShapeDtypeStruct((B * T, H * dv), BF16), jax.ShapeDtypeStruct((B, H, dk, dv), F32)]
    return pl.pallas_call(
        functools.partial(_hgrn_kernel, SB=SB, HB=HB, dk=dk, layer=layer, n_chunks=nC),
        grid=(B, H // HB, nC),
        in_specs=in_specs, out_specs=out_specs, out_shape=out_shape,
        scratch_shapes=[pltpu.VMEM((HB, dv, dk), F32)],
        compiler_params=_cparams(("parallel", "parallel", "arbitrary")),
        name="hgrn",
    )(proj, proj, proj, proj, lb_logits, norm.reshape(H, 1, dv), s0)


def _fox_cumsum_kernel(x_ref, b_ref, lf_ref, c_ref):
    lf = _log_sigmoid(x_ref[...] + b_ref[...])
    lf_ref[...] = lf
    c_ref[...] = _cumsum_lanes(lf, 1)


def fox_cumsum(df_t, bias):
    B, H, T = df_t.shape
    spec = pl.BlockSpec((None, H, T), lambda b: (b, 0, 0))
    return pl.pallas_call(
        _fox_cumsum_kernel,
        grid=(B,),
        in_specs=[spec, pl.BlockSpec((H, 1), lambda b: (0, 0))],
        out_specs=[spec, spec],
        out_shape=[jax.ShapeDtypeStruct((B, H, T), F32)] * 2,
        compiler_params=_cparams(("parallel",)),
        name="fox_cumsum",
    )(df_t, bias.reshape(H, 1))


def _fox_prompt_kernel(q_ref, k_ref, v_ref, og_ref, cr_ref, o_ref, *, TQ, T, dh):
    kb = k_ref[...].astype(BF16)
    vb = v_ref[...].astype(BF16)
    c_row = cr_ref[...]
    for i in range(T // TQ):
        lo, hi = i * TQ, (i + 1) * TQ
        q = (q_ref[lo:hi, :] * (dh ** -0.5)).astype(BF16)
        s = _nt(q, kb[:hi]) - c_row[:, :hi]
        qpos = lo + lax.broadcasted_iota(jnp.int32, (TQ, hi), 0)
        kpos = lax.broadcasted_iota(jnp.int32, (TQ, hi), 1)
        s = jnp.where(kpos <= qpos, s, NEG)
        p = jnp.exp(s - jnp.max(s, axis=1, keepdims=True))
        l = jnp.sum(p, axis=1, keepdims=True)
        o = _nn(p.astype(BF16), vb[:hi]) / l
        o_ref[lo:hi, :] = (o * _sigmoid(og_ref[lo:hi, :])).astype(o_ref.dtype)


def fox_prompt(proj, c_row, *, B, T, row_off, cols, H, dh, TQ):
    cq, ck, cv, co = cols
    rb = row_off // T

    def col(c0):
        return pl.BlockSpec((T, dh), lambda b, h: (rb + b, c0 // dh + h))

    in_specs = [col(cq), col(ck), col(cv), col(co),
                pl.BlockSpec((None, None, 1, T), lambda b, h: (b, h, 0, 0))]
    return pl.pallas_call(
        functools.partial(_fox_prompt_kernel, TQ=TQ, T=T, dh=dh),
        grid=(B, H),
        in_specs=in_specs,
        out_specs=pl.BlockSpec((T, dh), lambda b, h: (b, h)),
        out_shape=jax.ShapeDtypeStruct((B * T, H * dh), BF16),
        compiler_params=_cparams(("parallel", "parallel")),
        name="fox_prompt",
    )(proj, proj, proj, proj, c_row)


def _fox_sample_kernel(pt_ref, q_ref, kn_ref, vn_ref, og_ref, df_ref, fb_ref, *rest, PP, T, H, dh, n_steps):
    k_refs = rest[0:PP]
    v_refs = rest[PP:2 * PP]
    f_refs = rest[2 * PP:3 * PP]
    o_ref, lf_ref = rest[3 * PP], rest[3 * PP + 1]
    qall, acc, m_scr, l_scr, carry, crow, hmask = rest[3 * PP + 2:]
    s_id = pl.program_id(1)
    R = T * H
    W = PAGE * H

    @pl.when(s_id == 0)
    def _():
        qall[...] = (q_ref[...].reshape(R, dh) * (dh ** -0.5)).astype(BF16)
        acc[...] = jnp.zeros_like(acc)
        m_scr[...] = jnp.full_like(m_scr, NEG)
        l_scr[...] = jnp.zeros_like(l_scr)
        carry[...] = jnp.zeros_like(carry)
        lf_new = _log_sigmoid(df_ref[...] + fb_ref[...])
        lf_ref[...] = lf_new
        crow[...] = _cumsum_lanes(lf_new, H)
        rr = lax.broadcasted_iota(jnp.int32, (R, W), 0)
        ll = lax.broadcasted_iota(jnp.int32, (R, W), 1)
        hmask[...] = jnp.where((rr % H) == (ll % H), 0.0, NEG)

    lf = jnp.concatenate([f_refs[p][...] for p in range(PP)], axis=0)
    lane = lax.broadcasted_iota(jnp.int32, (PP, W), 1)
    suf = lf
    tot = lf
    s = H
    while s < W:
        suf = suf + jnp.where(lane + s < W, pltpu.roll(suf, W - s, 1), 0.0)
        tot = tot + pltpu.roll(tot, s, 1)
        s *= 2
    tot_cum = _cumsum_rows(tot)
    later = carry[...]
    d_all = suf - lf + (tot_cum - tot) + later
    carry[...] = later + tot_cum[PP - 1:PP, :]

    qa = qall[...]
    hm = hmask[...]
    m_pg, l_pg, a_pg = [], [], []
    for p in range(PP):
        kb = k_refs[p][...].astype(BF16)
        vb = v_refs[p][...].astype(BF16)
        st = _nt(qa, kb) + (hm + d_all[p:p + 1, :])
        m_p = jnp.max(st, axis=1, keepdims=True)
        pm = jnp.exp(st - m_p)
        m_pg.append(m_p)
        l_pg.append(jnp.sum(pm, axis=1, keepdims=True))
        a_pg.append(_nn(pm.astype(BF16), vb))
    m_prev = m_scr[...]
    m_old = m_prev
    for m_p in m_pg:
        m_old = jnp.maximum(m_old, m_p)
    alpha = jnp.exp(m_prev - m_old)
    l_old = alpha * l_scr[...]
    a_old = alpha * acc[...]
    for m_p, l_p, a_p in zip(m_pg, l_pg, a_pg):
        w_p = jnp.exp(m_p - m_old)
        l_old = l_old + w_p * l_p
        a_old = a_old + w_p * a_p
    m_scr[...] = m_old
    l_scr[...] = l_old
    acc[...] = a_old

    @pl.when(s_id == n_steps - 1)
    def _():
        knb = kn_ref[...].reshape(R, dh).astype(BF16)
        vnb = vn_ref[...].reshape(R, dh).astype(BF16)
        r2 = lax.broadcasted_iota(jnp.int32, (R, R), 0)
        l2 = lax.broadcasted_iota(jnp.int32, (R, R), 1)
        ok = ((r2 % H) == (l2 % H)) & (l2 // H <= r2 // H)
        sn = jnp.where(ok, _nt(qa, knb) - crow[...], NEG)
        m_fin = jnp.maximum(m_old, jnp.max(sn, axis=1, keepdims=True))
        a2 = jnp.exp(m_old - m_fin)
        pn = jnp.exp(sn - m_fin)
        l_fin = a2 * l_old + jnp.sum(pn, axis=1, keepdims=True)
        out = (a2 * a_old + _nn(pn.astype(BF16), vnb)) / l_fin
        out = out * _sigmoid(og_ref[...].reshape(R, dh))
        o_ref[...] = out.reshape(T, H, dh).astype(o_ref.dtype)


def fox_sample(proj3, df_flat, fb_flat, cache_k, cache_v, cache_f, page_table, *, B, T, page_off, cols, H, dh, PP):
    R = T * H
    W = PAGE * H
    n_pages = page_table.shape[1]
    n_steps = n_pages // PP
    cq, ck, cv, co = cols

    def rowspec(c0):
        return pl.BlockSpec((T, H, dh), lambda b, s, pt: (b, c0 // H, 0))

    def page(p):
        return lambda b, s, pt: (page_off + pt[b, n_pages - 1 - (s * PP + p)], 0, 0)

    in_specs = [rowspec(cq), rowspec(ck), rowspec(cv), rowspec(co),
                pl.BlockSpec((None, 1, R), lambda b, s, pt: (b, 0, 0)),
                pl.BlockSpec((1, R), lambda b, s, pt: (0, 0))]
    in_specs += [pl.BlockSpec((None, W, dh), page(p)) for p in range(PP)]
    in_specs += [pl.BlockSpec((None, W, dh), page(p)) for p in range(PP)]
    in_specs += [pl.BlockSpec((None, 1, W), page(p)) for p in range(PP)]
    grid_spec = pltpu.PrefetchScalarGridSpec(
        num_scalar_prefetch=1,
        grid=(B, n_steps),
        in_specs=in_specs,
        out_specs=[pl.BlockSpec((T, H, dh), lambda b, s, pt: (b, 0, 0)),
                   pl.BlockSpec((None, 1, R), lambda b, s, pt: (b, 0, 0))],
        scratch_shapes=[
            pltpu.VMEM((R, dh), BF16),
            pltpu.VMEM((R, dh), F32),
            pltpu.VMEM((R, 1), F32),
            pltpu.VMEM((R, 1), F32),
            pltpu.VMEM((1, W), F32),
            pltpu.VMEM((1, R), F32),
            pltpu.VMEM((R, W), F32),
        ],
    )
    return pl.pallas_call(
        functools.partial(_fox_sample_kernel, PP=PP, T=T, H=H, dh=dh, n_steps=n_steps),
        grid_spec=grid_spec,
        out_shape=[jax.ShapeDtypeStruct((B * T, H, dh), BF16), jax.ShapeDtypeStruct((B, 1, R), F32)],
        compiler_params=_cparams(("parallel", "arbitrary")),
        name="fox_sample",
    )(page_table, proj3, proj3, proj3, proj3, df_flat, fb_flat,
      *([cache_k] * PP), *([cache_v] * PP), *([cache_f] * PP))


def _pad_lanes(w):
    return jnp.pad(w, ((0, 0), (0, LANES - w.shape[1])))


def _row_tile(m, cap):
    best = None
    for t in range(BF16_ROWS, min(m, cap) + 1, BF16_ROWS):
        if m % t == 0:
            best = t
    assert best is not None, (m, cap)
    return best


def _col_tile(n, cap):
    t = cap
    while n % t:
        t //= 2
    return t


def kernel(x_prompt, x_sample, state_mlstm_C, state_mlstm_n, state_mlstm_m, state_gla_S, state_hgrn_S, cache_fox_k, cache_fox_v, cache_fox_logf, page_table, norm_mix_pre, norm_mix_post, norm_ffn_pre, norm_ffn_post, ffn_w_gate, ffn_w_up, ffn_w_down, w_in_even, w_out_even, mlstm_b_i, mlstm_b_f, mlstm_norm, gla_w_gk2, gla_b_gk, gla_norm, w_in_odd, w_out_odd, hgrn_lb_logits, hgrn_norm, fox_b_f):
    bp, tp, D = x_prompt.shape
    bs, ts, _ = x_sample.shape
    mp, ms = bp * tp, bs * ts
    M = mp + ms
    depth = norm_mix_pre.shape[0]
    a_heads, a_dk, a_dv = state_mlstm_C.shape[2:]
    b_heads, b_dk, b_dv = state_gla_S.shape[2:]
    c_heads, c_dk, c_dv = state_hgrn_S.shape[2:]
    d_heads = fox_b_f.shape[1]
    half = D // 2
    d_dh = half // d_heads
    ff = ffn_w_gate.shape[3]
    n_phys = cache_fox_k.shape[1]

    tm = _row_tile(M, 1408)
    tr = _row_tile(M, 256)
    wd_all = ffn_w_down.astype(BF16)

    def ffn(h, layer, j):
        u = ffn_up(h, ffn_w_gate, ffn_w_up, (layer, j), tm=tm, tn=_col_tile(ff, 256))
        return matmul([(u, wd_all, (layer, j), 0)], out_dtype=BF16, tm=tm, tn=_col_tile(D, 512), nk=2)

    groups = ((bp, tp, 0), (bs, ts, mp))
    outs = {k: ([], []) for k in ('c', 'n', 'm', 'g', 'h', 'k', 'v', 'f')}

    x, h = prenorm_join(x_prompt.reshape(mp, D), x_sample.reshape(ms, D), norm_ffn_pre[0, 0], tr=math.gcd(mp, ms))
    for layer in range(depth):
        y = ffn(h, layer, 0)
        x, h = resid_norm(x, y, norm_ffn_post[layer, 0], norm_mix_pre[layer], scale=0.5, tr=tr)
        if layer % 2 == 0:
            e = layer // 2
            wt = jnp.swapaxes(w_in_even, 1, 2)
            n_a = a_heads * (2 * a_dk + 2 * a_dv)
            n_b = b_heads * (2 * b_dk + 2 * b_dv)
            o_b = n_a + 2 * a_heads
            wt_small = jnp.concatenate([wt[e, n_a:o_b], wt[e, o_b + n_b:]], axis=0)
            wt_small = jnp.pad(wt_small, ((0, LANES - wt_small.shape[0]), (0, 0)))
            proj_a = matmul_nt(h, wt, (e,), n=n_a, out_dtype=F32, tm=tm, tn=_col_tile(n_a, 512))
            proj_b = matmul_nt(h, wt[e, o_b:o_b + n_b], (), n=n_b, out_dtype=F32, tm=tm, tn=_col_tile(n_b, 512))
            small = matmul_nt(h, wt_small, (), n=LANES, out_dtype=F32, tm=tm, tn=LANES)
            gate_bias = _pad_lanes(jnp.concatenate([mlstm_b_i[e], mlstm_b_f[e]])[None, :])
            a_cols = (0, a_heads * a_dk, 2 * a_heads * a_dk, 2 * a_heads * a_dk + a_heads * a_dv)
            w2 = gla_w_gk2[e]
            w2pad = jnp.zeros((LANES, b_heads * b_dk), F32).at[2 * a_heads:2 * a_heads + w2.shape[0]].set(w2)
            w2pad = w2pad.reshape(LANES, b_heads, b_dk).transpose(1, 0, 2).astype(BF16)
            b_cols = (0, b_heads * b_dk, 2 * b_heads * b_dk, 2 * b_heads * b_dk + b_heads * b_dv)
            ha, hb = [], []
            for gi, (B, T, r0) in enumerate(groups):
                if gi == 0:
                    c0, n0, m0, s0 = (jnp.zeros((B,) + s.shape[2:], F32)
                                      for s in (state_mlstm_C, state_mlstm_n, state_mlstm_m, state_gla_S))
                else:
                    c0, n0, m0, s0 = state_mlstm_C[e], state_mlstm_n[e], state_mlstm_m[e], state_gla_S[e]
                CH = min(T, 128)
                ha_g, c1, n1, m1 = mlstm_mixer(proj_a, small, gate_bias, mlstm_norm[e], c0, n0, m0,
                                               B=B, T=T, row_off=r0, cols=a_cols,
                                               H=a_heads, dk=a_dk, dv=a_dv, L=CH, HB=4)
                hb_g, s1 = gla_mixer(proj_b, small, w2pad, gla_b_gk[e], gla_norm[e], s0,
                                     B=B, T=T, row_off=r0, cols=b_cols,
                                     H=b_heads, dk=b_dk, dv=b_dv, CH=CH, SB=min(CH, 32), HB=4)
                ha.append(ha_g); hb.append(hb_g)
                outs['c'][gi].append(c1); outs['n'][gi].append(n1)
                outs['m'][gi].append(m1); outs['g'][gi].append(s1)
            wo = w_out_even.astype(BF16)
            mix = matmul([(jnp.concatenate(ha, axis=0), wo, (e,), 0), (jnp.concatenate(hb, axis=0), wo, (e,), 1)],
                         out_dtype=BF16, tm=tm, tn=_col_tile(D, 512))
        else:
            o = layer // 2
            n_main = 4 * c_heads * c_dk + 4 * half
            wt = jnp.swapaxes(w_in_odd, 1, 2)
            wt_small = wt[o, n_main:]
            wt_small = jnp.pad(wt_small, ((0, LANES - wt_small.shape[0]), (0, 0)))
            proj = matmul_nt(h, wt, (o,), n=n_main, out_dtype=F32, tm=tm, tn=_col_tile(n_main, 512))
            small = matmul_nt(h, wt_small, (), n=LANES, out_dtype=F32, tm=tm, tn=LANES)
            cw = c_heads * c_dk
            dq = 4 * cw
            d_cols = (dq, dq + half, dq + 2 * half, dq + 3 * half)
            hc, hd = [], []
            for gi, (B, T, r0) in enumerate(groups):
                s0 = jnp.zeros((B,) + state_hgrn_S.shape[2:], F32) if gi == 0 else state_hgrn_S[o]
                CH = min(T, 128)
                hc_g, s1 = hgrn_mixer(proj, hgrn_lb_logits, hgrn_norm[o], s0,
                                      B=B, T=T, row_off=r0, cols=(0, cw, 2 * cw, 3 * cw),
                                      H=c_heads, dk=c_dk, dv=c_dv, CH=CH, SB=min(CH, 32), HB=8, layer=layer)
                df = small[r0:r0 + B * T, :d_heads]
                if gi == 0:
                    lf_t, c_t = fox_cumsum(df.reshape(B, T, d_heads).transpose(0, 2, 1), fox_b_f[o])
                    logf = lf_t.transpose(0, 2, 1)
                    hd_g = fox_prompt(proj, c_t.reshape(B, d_heads, 1, T), B=B, T=T, row_off=r0,
                                      cols=d_cols, H=d_heads, dh=d_dh, TQ=min(T, 256))
                else:
                    n_cb = proj.shape[1] // d_dh
                    proj3 = proj[r0:r0 + B * T].reshape(B * T, n_cb, d_dh)
                    n_odd = cache_fox_k.shape[0]
                    hd3, lf = fox_sample(
                        proj3, df.reshape(B, 1, T * d_heads), jnp.tile(fox_b_f[o], T)[None, :],
                        cache_fox_k.reshape(n_odd * n_phys, PAGE * d_heads, d_dh),
                        cache_fox_v.reshape(n_odd * n_phys, PAGE * d_heads, d_dh),
                        cache_fox_logf.reshape(n_odd * n_phys, 1, PAGE * d_heads),
                        page_table, B=B, T=T, page_off=o * n_phys,
                        cols=tuple(c // d_dh for c in d_cols), H=d_heads, dh=d_dh, PP=8)
                    hd_g = hd3.reshape(B * T, half)
                    logf = lf.reshape(B, T, d_heads)
                hc.append(hc_g); hd.append(hd_g)
                outs['h'][gi].append(s1)
                k_new, v_new = split_heads(proj, rows=B * T, row_off=r0, ck=d_cols[1], cv=d_cols[2],
                                           H=d_heads, dh=d_dh, tr=min(B * T, 256))
                outs['k'][gi].append(k_new.reshape(B, T, d_heads, d_dh))
                outs['v'][gi].append(v_new.reshape(B, T, d_heads, d_dh))
                outs['f'][gi].append(logf)
            wo = w_out_odd.astype(BF16)
            mix = matmul([(jnp.concatenate(hc, axis=0), wo, (o,), 0), (jnp.concatenate(hd, axis=0), wo, (o,), 1)],
                         out_dtype=BF16, tm=tm, tn=_col_tile(D, 512))
        x, h = resid_norm(x, mix, norm_mix_post[layer], norm_ffn_pre[layer, 1], scale=1.0, tr=tr)
        y = ffn(h, layer, 1)
        if layer + 1 < depth:
            x, h = resid_norm(x, y, norm_ffn_post[layer, 1], norm_ffn_pre[layer + 1, 0], scale=0.5, tr=tr)
        else:
            y_p, y_s = resid_split(x, y, norm_ffn_post[layer, 1], scale=0.5, rows_a=mp, tr=math.gcd(mp, ms))

    names = ('c', 'n', 'm', 'g', 'h', 'k', 'v', 'f')
    res_p = tuple(jnp.stack(outs[k][0]) for k in names)
    res_s = tuple(jnp.stack(outs[k][1]) for k in names)
    return (y_p.reshape(bp, tp, D), y_s.reshape(bs, ts, D)) + res_p + res_s
```

```python
import functools
import math

import jax
import jax.numpy as jnp
from jax import lax
from jax.experimental import pallas as pl
from jax.experimental.pallas import tpu as pltpu

F32 = jnp.float32
BF16 = jnp.bfloat16
EPS = 1e-6
NEG = -1e30
LANES = 128
BF16_ROWS = 16
VMEM_LIMIT = 56 * 1024 * 1024

B_GATE_NORMALIZER = 16.0

MM_ROWS_CAP = 1408
NORM_ROWS_CAP = 256
FFN_COLS = 256
PROJ_COLS = 512
SCAN_CHUNK = 128
SCAN_SUB = 32
FOX_Q_BLOCK = 256
FOX_PAGES_PER_STEP = 8
MLSTM_HEADS_PER_STEP = 4
GLA_HEADS_PER_STEP = 4
HGRN_HEADS_PER_STEP = 8
SPLIT_ROWS = 256
CAST_ROWS = 128


def _cparams(sem):
    return pltpu.CompilerParams(dimension_semantics=sem, vmem_limit_bytes=VMEM_LIMIT)


def _log_sigmoid(x):
    return jnp.minimum(x, 0.0) - jnp.log1p(jnp.exp(-jnp.abs(x)))


def _sigmoid(x):
    return jax.nn.sigmoid(x)


def _nt(a, b):
    return lax.dot_general(a, b, (((1,), (1,)), ((), ())), preferred_element_type=F32)


def _tn(a, b):
    return lax.dot_general(a, b, (((0,), (0,)), ((), ())), preferred_element_type=F32)


def _nn(a, b):
    return jnp.dot(a, b, preferred_element_type=F32)


def _cumsum_rows(x):
    n = x.shape[0]
    row = lax.broadcasted_iota(jnp.int32, x.shape, 0)
    s = 1
    while s < n:
        x = x + jnp.where(row >= s, pltpu.roll(x, s, 0), 0.0)
        s *= 2
    return x


def _cumsum_lanes(x, step):
    n = x.shape[1]
    lane = lax.broadcasted_iota(jnp.int32, x.shape, 1)
    s = step
    while s < n:
        x = x + jnp.where(lane >= s, pltpu.roll(x, s, 1), 0.0)
        s *= 2
    return x


def _prenorm_join_kernel(xa_ref, xb_ref, g_ref, xo_ref, ho_ref, *, n_a):
    x = jnp.where(pl.program_id(0) < n_a, xa_ref[...], xb_ref[...])
    xo_ref[...] = x
    r = lax.rsqrt(jnp.mean(x * x, axis=-1, keepdims=True) + EPS)
    ho_ref[...] = (x * r * g_ref[...]).astype(ho_ref.dtype)


def prenorm_join(xa, xb, g, *, tr):
    (ma, d), mb = xa.shape, xb.shape[0]
    n_a = ma // tr
    row = pl.BlockSpec((tr, d), lambda i: (i, 0))
    return pl.pallas_call(
        functools.partial(_prenorm_join_kernel, n_a=n_a),
        grid=((ma + mb) // tr,),
        in_specs=[pl.BlockSpec((tr, d), lambda i: (jnp.minimum(i, n_a - 1), 0)),
                  pl.BlockSpec((tr, d), lambda i: (jnp.maximum(i - n_a, 0), 0)),
                  pl.BlockSpec((1, d), lambda i: (0, 0))],
        out_specs=[row, row],
        out_shape=[jax.ShapeDtypeStruct((ma + mb, d), F32), jax.ShapeDtypeStruct((ma + mb, d), BF16)],
        compiler_params=_cparams(("parallel",)),
        name="prenorm_join",
    )(xa, xb, g.reshape(1, d))


def _resid_kernel(x_ref, y_ref, gp_ref, gn_ref, xo_ref, ho_ref, *, scale):
    y = y_ref[...].astype(F32)
    r = lax.rsqrt(jnp.mean(y * y, axis=-1, keepdims=True) + EPS)
    xn = x_ref[...] + scale * (y * r * gp_ref[...])
    xo_ref[...] = xn
    r2 = lax.rsqrt(jnp.mean(xn * xn, axis=-1, keepdims=True) + EPS)
    ho_ref[...] = (xn * r2 * gn_ref[...]).astype(ho_ref.dtype)


def resid_norm(x, y, g_post, g_next, *, scale, tr):
    m, d = x.shape
    row = pl.BlockSpec((tr, d), lambda i: (i, 0))
    vec = pl.BlockSpec((1, d), lambda i: (0, 0))
    return pl.pallas_call(
        functools.partial(_resid_kernel, scale=scale),
        grid=(m // tr,),
        in_specs=[row, row, vec, vec],
        out_specs=[row, row],
        out_shape=[jax.ShapeDtypeStruct((m, d), F32), jax.ShapeDtypeStruct((m, d), BF16)],
        compiler_params=_cparams(("parallel",)),
        name="resid_norm",
    )(x, y, g_post.reshape(1, d), g_next.reshape(1, d))


def _resid_split_kernel(x_ref, y_ref, gp_ref, xa_ref, xb_ref, *, scale, n_a):
    i = pl.program_id(0)
    y = y_ref[...].astype(F32)
    r = lax.rsqrt(jnp.mean(y * y, axis=-1, keepdims=True) + EPS)
    xn = x_ref[...] + scale * (y * r * gp_ref[...])

    @pl.when(i < n_a)
    def _():
        xa_ref[...] = xn

    @pl.when(i >= n_a)
    def _():
        xb_ref[...] = xn


def resid_split(x, y, g_post, *, scale, rows_a, tr):
    m, d = x.shape
    n_a = rows_a // tr
    row = pl.BlockSpec((tr, d), lambda i: (i, 0))
    return pl.pallas_call(
        functools.partial(_resid_split_kernel, scale=scale, n_a=n_a),
        grid=(m // tr,),
        in_specs=[row, row, pl.BlockSpec((1, d), lambda i: (0, 0))],
        out_specs=[pl.BlockSpec((tr, d), lambda i: (jnp.minimum(i, n_a - 1), 0)),
                   pl.BlockSpec((tr, d), lambda i: (jnp.maximum(i - n_a, 0), 0))],
        out_shape=[jax.ShapeDtypeStruct((rows_a, d), F32), jax.ShapeDtypeStruct((m - rows_a, d), F32)],
        compiler_params=_cparams(("arbitrary",)),
        name="resid_split",
    )(x, y, g_post.reshape(1, d))


def _split_heads_kernel(k_ref, v_ref, ko_ref, vo_ref, *, H, dh):
    for h in range(H):
        ko_ref[:, h, :] = k_ref[:, h * dh:(h + 1) * dh]
        vo_ref[:, h, :] = v_ref[:, h * dh:(h + 1) * dh]


def split_heads(proj, *, rows, row_off, ck, cv, H, dh, tr):
    w = H * dh
    out = jax.ShapeDtypeStruct((rows, H, dh), proj.dtype)
    ospec = pl.BlockSpec((tr, H, dh), lambda i: (i, 0, 0))
    return pl.pallas_call(
        functools.partial(_split_heads_kernel, H=H, dh=dh),
        grid=(rows // tr,),
        in_specs=[pl.BlockSpec((tr, w), lambda i: (row_off // tr + i, ck // w)),
                  pl.BlockSpec((tr, w), lambda i: (row_off // tr + i, cv // w))],
        out_specs=[ospec, ospec],
        out_shape=[out, out],
        compiler_params=_cparams(("parallel",)),
        name="split_heads",
    )(proj, proj)


def _mm_kernel(*refs, n_pairs, nk, n_ride):
    refs = list(refs)
    pair_refs = [refs.pop(0) for _ in range(2 * n_pairs)]
    ride_in = refs.pop(0) if n_ride else None
    o_ref = refs.pop(0)
    ride_out = refs.pop(0) if n_ride else None
    if n_ride:
        step = (pl.program_id(0) * pl.num_programs(1) + pl.program_id(1)) * nk + pl.program_id(2)

        @pl.when(step < n_ride)
        def _():
            ride_out[...] = ride_in[...].astype(ride_out.dtype)

    acc = _nn(pair_refs[0][...], pair_refs[1][...])
    for p in range(1, n_pairs):
        acc = acc + _nn(pair_refs[2 * p][...], pair_refs[2 * p + 1][...])
    if nk == 1:
        o_ref[...] = acc.astype(o_ref.dtype)
    else:
        acc_ref = refs.pop(0)
        k = pl.program_id(2)

        @pl.when(k == 0)
        def _():
            acc_ref[...] = acc

        @pl.when((k > 0) & (k < nk - 1))
        def _():
            acc_ref[...] += acc

        @pl.when(k == nk - 1)
        def _():
            o_ref[...] = (acc_ref[...] + acc).astype(o_ref.dtype)


def matmul(pairs, *, out_dtype, tm, tn, nk=1, ride=None):
    m = pairs[0][0].shape[0]
    n = pairs[0][1].shape[-1]
    grid = (m // tm, n // tn, nk)
    in_specs, args = [], []
    for x, w, lead, kb0 in pairs:
        tk = x.shape[1] // nk
        in_specs.append(pl.BlockSpec((tm, tk), lambda i, j, k: (i, k)))
        in_specs.append(pl.BlockSpec((None,) * len(lead) + (tk, tn),
                                     lambda i, j, k, lead=lead, kb0=kb0, nk=nk: lead + (kb0 * nk + k, j)))
        args += [x, w]
    out_specs = [pl.BlockSpec((tm, tn), lambda i, j, k: (i, j))]
    out_shape = [jax.ShapeDtypeStruct((m, n), out_dtype)]
    n_ride = 0
    if ride is not None:
        a, lead, rows = ride
        r, c = a.shape[-2:]
        n_ride = r // rows
        assert n_ride * rows == r and n_ride <= grid[0] * grid[1] * grid[2]

        def blk(i, j, k):
            return jnp.minimum((i * grid[1] + j) * nk + k, n_ride - 1)

        in_specs.append(pl.BlockSpec((None,) * len(lead) + (rows, c), lambda i, j, k: lead + (blk(i, j, k), 0)))
        args.append(a)
        out_specs.append(pl.BlockSpec((rows, c), lambda i, j, k: (blk(i, j, k), 0)))
        out_shape.append(jax.ShapeDtypeStruct((r, c), BF16))
    res = pl.pallas_call(
        functools.partial(_mm_kernel, n_pairs=len(pairs), nk=nk, n_ride=n_ride),
        grid=grid,
        in_specs=in_specs,
        out_specs=out_specs,
        out_shape=out_shape,
        scratch_shapes=[pltpu.VMEM((tm, tn), F32)] if nk > 1 else [],
        compiler_params=_cparams(("arbitrary", "arbitrary", "arbitrary") if n_ride else
                                 ("parallel", "parallel", "arbitrary")),
        name="matmul",
    )(*args)
    return res if n_ride else res[0]


def _mm_nt_kernel(x_ref, wt_ref, o_ref, w_bf):
    @pl.when(pl.program_id(1) == 0)
    def _():
        w_bf[...] = wt_ref[...].astype(BF16)

    o_ref[...] = _nt(x_ref[...], w_bf[...]).astype(o_ref.dtype)


def matmul_nt(x, wt, lead, *, n, out_dtype, tm, tn):
    m, k = x.shape
    return pl.pallas_call(
        _mm_nt_kernel,
        grid=(n // tn, m // tm),
        in_specs=[pl.BlockSpec((tm, k), lambda j, i: (i, 0)),
                  pl.BlockSpec((None,) * len(lead) + (tn, k), lambda j, i: lead + (j, 0))],
        out_specs=pl.BlockSpec((tm, tn), lambda j, i: (i, j)),
        out_shape=jax.ShapeDtypeStruct((m, n), out_dtype),
        scratch_shapes=[pltpu.VMEM((tn, k), BF16)],
        compiler_params=_cparams(("parallel", "arbitrary")),
        name="matmul_nt",
    )(x, wt)


def _ffn_up_kernel(h_ref, wg_ref, wu_ref, o_ref, wg_bf, wu_bf):
    @pl.when(pl.program_id(1) == 0)
    def _():
        wg_bf[...] = wg_ref[...].astype(BF16)
        wu_bf[...] = wu_ref[...].astype(BF16)

    h = h_ref[...]
    g = _nn(h, wg_bf[...])
    u = _nn(h, wu_bf[...])
    o_ref[...] = (g * _sigmoid(g) * u).astype(o_ref.dtype)


def ffn_up(h, wg, wu, lead, *, tm, tn):
    m, k = h.shape
    n = wg.shape[-1]
    wspec = pl.BlockSpec((None,) * len(lead) + (k, tn), lambda j, i: lead + (0, j))
    return pl.pallas_call(
        _ffn_up_kernel,
        grid=(n // tn, m // tm),
        in_specs=[pl.BlockSpec((tm, k), lambda j, i: (i, 0)), wspec, wspec],
        out_specs=pl.BlockSpec((tm, tn), lambda j, i: (i, j)),
        out_shape=jax.ShapeDtypeStruct((m, n), BF16),
        scratch_shapes=[pltpu.VMEM((k, tn), BF16), pltpu.VMEM((k, tn), BF16)],
        compiler_params=_cparams(("parallel", "arbitrary")),
        name="ffn_up",
    )(h, wg, wu)


def _mlstm_kernel(q_ref, k_ref, v_ref, og_ref, gt_ref, gb_ref, nrm_ref, c0_ref, n0_ref, m0_ref,
                  h_ref, c_ref, n_ref, m_ref, *, L, HB, dk, dv, n_heads):
    @pl.when(pl.program_id(2) == 0)
    def _():
        c_ref[...] = c0_ref[...]
        n_ref[...] = n0_ref[...]
        m_ref[...] = m0_ref[...]

    gates = gt_ref[...] + gb_ref[...]
    lane = lax.broadcasted_iota(jnp.int32, gates.shape, 1)
    r = lax.broadcasted_iota(jnp.int32, (L, L), 0)
    s = lax.broadcasted_iota(jnp.int32, (L, L), 1)
    tril = r >= s
    eye = r == s

    def col2row(col):
        return jnp.sum(jnp.where(eye, col, 0.0), axis=0, keepdims=True)

    for hb in range(HB):
        hh = pl.program_id(1) * HB + hb
        ks = slice(hb * dk, (hb + 1) * dk)
        vs = slice(hb * dv, (hb + 1) * dv)
        i_col = jnp.sum(jnp.where(lane == hh, gates, 0.0), axis=1, keepdims=True)
        f_col = jnp.sum(jnp.where(lane == hh + n_heads, gates, 0.0), axis=1, keepdims=True)
        logf = _log_sigmoid(f_col)
        logf_row = col2row(logf)
        b_col = jnp.sum(jnp.where(tril, logf_row, 0.0), axis=1, keepdims=True)
        b_row = col2row(b_col)
        i_row = col2row(i_col)
        m_prev = m_ref[hb]
        dmat = jnp.where(tril, b_col - b_row + i_row, -jnp.inf)
        from_state = b_col + m_prev
        m_t = jnp.maximum(from_state, jnp.max(dmat, axis=1, keepdims=True))

        q = q_ref[:, ks]
        k = k_ref[:, ks] * (dk ** -0.5)
        qb = q.astype(BF16)
        vb = v_ref[:, vs].astype(BF16)
        w = jnp.exp(dmat - m_t) * _nt(qb, k.astype(BF16))
        s_state = jnp.exp(from_state - m_t)
        c_old = c_ref[hb]
        n_old = n_ref[hb]
        num = s_state * _nn(qb, c_old.astype(BF16)) + _nn(w.astype(BF16), vb)
        den = s_state * jnp.sum(q * n_old, axis=1, keepdims=True) + jnp.sum(w, axis=1, keepdims=True)
        h = num / jnp.maximum(jnp.abs(den), jnp.exp(-m_t))

        m_new = m_t[L - 1:L, :]
        b_last = b_col[L - 1:L, :]
        w_end = jnp.exp(b_last - b_col + i_col - m_new)
        s_end = jnp.exp(b_last + m_prev - m_new)
        kw = k * w_end
        c_ref[hb] = s_end * c_old + _tn(kw.astype(BF16), vb)
        n_ref[hb] = s_end * n_old + jnp.sum(kw, axis=0, keepdims=True)
        m_ref[hb] = m_new

        hn = h * lax.rsqrt(jnp.mean(h * h, axis=1, keepdims=True) + EPS) * nrm_ref[hb]
        h_ref[:, vs] = (hn * _sigmoid(og_ref[:, vs])).astype(h_ref.dtype)


def mlstm_mixer(proj, small, gate_bias, norm, c0, n0, m0, *, B, T, row_off, cols, H, dk, dv, L, HB):
    nC = T // L
    rb = row_off // L
    cq, ck, cv, co = cols
    wk, wv = HB * dk, HB * dv

    def rows(b, h, c):
        return rb + b * nC + c

    in_specs = [
        pl.BlockSpec((L, wk), lambda b, h, c: (rows(b, h, c), cq // wk + h)),
        pl.BlockSpec((L, wk), lambda b, h, c: (rows(b, h, c), ck // wk + h)),
        pl.BlockSpec((L, wv), lambda b, h, c: (rows(b, h, c), cv // wv + h)),
        pl.BlockSpec((L, wv), lambda b, h, c: (rows(b, h, c), co // wv + h)),
        pl.BlockSpec((L, LANES), lambda b, h, c: (rows(b, h, c), 0)),
        pl.BlockSpec((1, LANES), lambda b, h, c: (0, 0)),
        pl.BlockSpec((HB, 1, dv), lambda b, h, c: (h, 0, 0)),
        pl.BlockSpec((None, HB, dk, dv), lambda b, h, c: (b, h, 0, 0)),
        pl.BlockSpec((None, HB, 1, dk), lambda b, h, c: (b, h, 0, 0)),
        pl.BlockSpec((None, HB, 1, 1), lambda b, h, c: (b, h, 0, 0)),
    ]
    out_specs = [
        pl.BlockSpec((L, wv), lambda b, h, c: (b * nC + c, h)),
        pl.BlockSpec((None, HB, dk, dv), lambda b, h, c: (b, h, 0, 0)),
        pl.BlockSpec((None, HB, 1, dk), lambda b, h, c: (b, h, 0, 0)),
        pl.BlockSpec((None, HB, 1, 1), lambda b, h, c: (b, h, 0, 0)),
    ]
    out_shape = [
        jax.ShapeDtypeStruct((B * T, H * dv), BF16),
        jax.ShapeDtypeStruct((B, H, dk, dv), F32),
        jax.ShapeDtypeStruct((B, H, 1, dk), F32),
        jax.ShapeDtypeStruct((B, H, 1, 1), F32),
    ]
    h, c1, n1, m1 = pl.pallas_call(
        functools.partial(_mlstm_kernel, L=L, HB=HB, dk=dk, dv=dv, n_heads=H),
        grid=(B, H // HB, nC),
        in_specs=in_specs, out_specs=out_specs, out_shape=out_shape,
        compiler_params=_cparams(("parallel", "parallel", "arbitrary")),
        name="mlstm",
    )(proj, proj, proj, proj, small, gate_bias, norm.reshape(H, 1, dv), c0,
      n0.reshape(B, H, 1, dk), m0.reshape(B, H, 1, 1))
    return h, c1, n1.reshape(B, H, dk), m1.reshape(B, H)


def _scan_chunk(q, k, v, g, st, *, SB):
    CH = q.shape[0]
    bc = _cumsum_rows(g)
    b_last = bc[CH - 1:CH, :]
    vb = v.astype(BF16)
    o = _nt((q * jnp.exp(bc)).astype(BF16), st.astype(BF16))
    row = lax.broadcasted_iota(jnp.int32, bc.shape, 0)
    srow = lax.broadcasted_iota(jnp.int32, (SB, CH), 0)
    scol = lax.broadcasted_iota(jnp.int32, (SB, CH), 1)
    blocks = []
    for i in range(CH // SB):
        lo, hi = i * SB, (i + 1) * SB
        ref = bc[lo - 1:lo, :] if i > 0 else jnp.zeros_like(b_last)
        qi = q[lo:hi] * jnp.exp(bc[lo:hi] - ref)
        ki = k * jnp.exp(jnp.where(row < hi, ref - bc, 0.0))
        a = _nt(qi.astype(BF16), ki.astype(BF16))
        blocks.append(jnp.where(scol <= srow + lo, a, 0.0))
    a_full = blocks[0] if len(blocks) == 1 else jnp.concatenate(blocks, axis=0)
    o = o + _nn(a_full.astype(BF16), vb)
    k_dec = k * jnp.exp(b_last - bc)
    st_new = st * jnp.exp(b_last) + _tn(vb, k_dec.astype(BF16))
    return o, st_new


def _head_out(o, nrm, gate_act):
    return o * lax.rsqrt(jnp.mean(o * o, axis=1, keepdims=True) + EPS) * nrm * gate_act


def _gla_kernel(q_ref, k_ref, v_ref, og_ref, gt_ref, w2_ref, bgk_ref, nrm_ref, s0_ref,
                o_ref, s_ref, st_scr, *, SB, HB, dk, dv, n_chunks):
    c = pl.program_id(2)

    @pl.when(c == 0)
    def _():
        for hb in range(HB):
            st_scr[hb] = s0_ref[hb].T

    gates = gt_ref[...].astype(BF16)
    for hb in range(HB):
        ks = slice(hb * dk, (hb + 1) * dk)
        vs = slice(hb * dv, (hb + 1) * dv)
        gk = _nn(gates, w2_ref[hb]) + bgk_ref[hb]
        g = _log_sigmoid(gk) / B_GATE_NORMALIZER
        o, st_new = _scan_chunk(q_ref[:, ks] * (dk ** -0.5), k_ref[:, ks], v_ref[:, vs], g, st_scr[hb], SB=SB)
        st_scr[hb] = st_new
        og = og_ref[:, vs]
        o_ref[:, vs] = _head_out(o, nrm_ref[hb], og * _sigmoid(og)).astype(o_ref.dtype)

    @pl.when(c == n_chunks - 1)
    def _():
        for hb in range(HB):
            s_ref[hb] = st_scr[hb].T


def gla_mixer(proj, small, w2pad, b_gk, norm, s0, *, B, T, row_off, cols, H, dk, dv, CH, SB, HB):
    nC = T // CH
    rb = row_off // CH
    cq, ck, cv, co = cols
    wk, wv = HB * dk, HB * dv

    def rows(b, h, c):
        return rb + b * nC + c

    in_specs = [
        pl.BlockSpec((CH, wk), lambda b, h, c: (rows(b, h, c), cq // wk + h)),
        pl.BlockSpec((CH, wk), lambda b, h, c: (rows(b, h, c), ck // wk + h)),
        pl.BlockSpec((CH, wv), lambda b, h, c: (rows(b, h, c), cv // wv + h)),
        pl.BlockSpec((CH, wv), lambda b, h, c: (rows(b, h, c), co // wv + h)),
        pl.BlockSpec((CH, LANES), lambda b, h, c: (rows(b, h, c), 0)),
        pl.BlockSpec((HB, LANES, dk), lambda b, h, c: (h, 0, 0)),
        pl.BlockSpec((HB, 1, dk), lambda b, h, c: (h, 0, 0)),
        pl.BlockSpec((HB, 1, dv), lambda b, h, c: (h, 0, 0)),
        pl.BlockSpec((None, HB, dk, dv), lambda b, h, c: (b, h, 0, 0)),
    ]
    out_specs = [
        pl.BlockSpec((CH, wv), lambda b, h, c: (b * nC + c, h)),
        pl.BlockSpec((None, HB, dk, dv), lambda b, h, c: (b, h, 0, 0)),
    ]
    out_shape = [jax.ShapeDtypeStruct((B * T, H * dv), BF16), jax.ShapeDtypeStruct((B, H, dk, dv), F32)]
    return pl.pallas_call(
        functools.partial(_gla_kernel, SB=SB, HB=HB, dk=dk, dv=dv, n_chunks=nC),
        grid=(B, H // HB, nC),
        in_specs=in_specs, out_specs=out_specs, out_shape=out_shape,
        scratch_shapes=[pltpu.VMEM((HB, dv, dk), F32)],
        compiler_params=_cparams(("parallel", "parallel", "arbitrary")),
        name="gla",
    )(proj, proj, proj, proj, small, w2pad, b_gk.reshape(H, 1, dk), norm.reshape(H, 1, dv), s0)


def _hgrn_kernel(q_ref, f_ref, i_ref, og_ref, lg_ref, nrm_ref, s0_ref, o_ref, s_ref, st_scr,
                 *, SB, HB, dk, layer, n_chunks):
    c = pl.program_id(2)
    lg = lg_ref[...]
    e = jnp.exp(lg - jnp.max(lg, axis=0, keepdims=True))
    sm = e / jnp.sum(e, axis=0, keepdims=True)
    lb_all = jnp.sum(sm[1:layer + 1], axis=0, keepdims=True) if layer > 0 else jnp.zeros_like(sm[0:1])
    @pl.when(c == 0)
    def _():
        for hb in range(HB):
            st_scr[hb] = s0_ref[hb].T

    for hb in range(HB):
        sl = slice(hb * dk, (hb + 1) * dk)
        lb = lb_all[:, sl]
        z = f_ref[:, sl]
        g = jnp.log(lb + (1.0 - lb) * _sigmoid(z))
        key = (1.0 - lb) * _sigmoid(-z)
        cq = q_ref[:, sl]
        o, st_new = _scan_chunk(cq * _sigmoid(cq), key, i_ref[:, sl], g, st_scr[hb], SB=SB)
        st_scr[hb] = st_new
        og = og_ref[:, sl]
        o_ref[:, sl] = _head_out(o, nrm_ref[hb], og * _sigmoid(og)).astype(o_ref.dtype)

    @pl.when(c == n_chunks - 1)
    def _():
        for hb in range(HB):
            s_ref[hb] = st_scr[hb].T


def hgrn_mixer(proj, lb_logits, norm, s0, *, B, T, row_off, cols, H, dk, dv, CH, SB, HB, layer):
    assert dk == dv
    nC = T // CH
    rb = row_off // CH
    cq, cf, ci, co = cols
    depth = lb_logits.shape[0]
    wb = HB * dk

    def col(c0):
        return pl.BlockSpec((CH, wb), lambda b, h, c: (rb + b * nC + c, c0 // wb + h))

    in_specs = [
        col(cq), col(cf), col(ci), col(co),
        pl.BlockSpec((depth, wb), lambda b, h, c: (0, h)),
        pl.BlockSpec((HB, 1, dv), lambda b, h, c: (h, 0, 0)),
        pl.BlockSpec((None, HB, dk, dv), lambda b, h, c: (b, h, 0, 0)),
    ]
    out_specs = [
        pl.BlockSpec((CH, wb), lambda b, h, c: (b * nC + c, h)),
        pl.BlockSpec((None, HB, dk, dv), lambda b, h, c: (b, h, 0, 0)),
    ]
    out_shape = [jax.ShapeDtypeStruct((B * T, H * dv), BF16), jax.ShapeDtypeStruct((B, H, dk, dv), F32)]
    return pl.pallas_call(
        functools.partial(_hgrn_kernel, SB=SB, HB=HB, dk=dk, layer=layer, n_chunks=nC),
        grid=(B, H // HB, nC),
        in_specs=in_specs, out_specs=out_specs, out_shape=out_shape,
        scratch_shapes=[pltpu.VMEM((HB, dv, dk), F32)],
        compiler_params=_cparams(("parallel", "parallel", "arbitrary")),
        name="hgrn",
    )(proj, proj, proj, proj, lb_logits, norm.reshape(H, 1, dv), s0)


def _fox_cumsum_kernel(x_ref, b_ref, lf_ref, c_ref):
    lf = _log_sigmoid(x_ref[...] + b_ref[...])
    lf_ref[...] = lf
    c_ref[...] = _cumsum_lanes(lf, 1)


def fox_cumsum(df_t, bias):
    B, H, T = df_t.shape
    spec = pl.BlockSpec((None, H, T), lambda b: (b, 0, 0))
    return pl.pallas_call(
        _fox_cumsum_kernel,
        grid=(B,),
        in_specs=[spec, pl.BlockSpec((H, 1), lambda b: (0, 0))],
        out_specs=[spec, spec],
        out_shape=[jax.ShapeDtypeStruct((B, H, T), F32)] * 2,
        compiler_params=_cparams(("parallel",)),
        name="fox_cumsum",
    )(df_t, bias.reshape(H, 1))


def _fox_prompt_kernel(q_ref, k_ref, v_ref, og_ref, cr_ref, o_ref, *, TQ, T, dh):
    kb = k_ref[...].astype(BF16)
    vb = v_ref[...].astype(BF16)
    c_row = cr_ref[...]
    for i in range(T // TQ):
        lo, hi = i * TQ, (i + 1) * TQ
        q = (q_ref[lo:hi, :] * (dh ** -0.5)).astype(BF16)
        s = _nt(q, kb[:hi]) - c_row[:, :hi]
        qpos = lo + lax.broadcasted_iota(jnp.int32, (TQ, hi), 0)
        kpos = lax.broadcasted_iota(jnp.int32, (TQ, hi), 1)
        s = jnp.where(kpos <= qpos, s, NEG)
        p = jnp.exp(s - jnp.max(s, axis=1, keepdims=True))
        l = jnp.sum(p, axis=1, keepdims=True)
        o = _nn(p.astype(BF16), vb[:hi]) / l
        o_ref[lo:hi, :] = (o * _sigmoid(og_ref[lo:hi, :])).astype(o_ref.dtype)


def fox_prompt(proj, c_row, *, B, T, row_off, cols, H, dh, TQ):
    cq, ck, cv, co = cols
    rb = row_off // T

    def col(c0):
        return pl.BlockSpec((T, dh), lambda b, h: (rb + b, c0 // dh + h))

    in_specs = [col(cq), col(ck), col(cv), col(co),
                pl.BlockSpec((None, None, 1, T), lambda b, h: (b, h, 0, 0))]
    return pl.pallas_call(
        functools.partial(_fox_prompt_kernel, TQ=TQ, T=T, dh=dh),
        grid=(B, H),
        in_specs=in_specs,
        out_specs=pl.BlockSpec((T, dh), lambda b, h: (b, h)),
        out_shape=jax.ShapeDtypeStruct((B * T, H * dh), BF16),
        compiler_params=_cparams(("parallel", "parallel")),
        name="fox_prompt",
    )(proj, proj, proj, proj, c_row)


def _fox_sample_kernel(pt_ref, q_ref, kn_ref, vn_ref, og_ref, df_ref, fb_ref, *rest, PP, T, H, dh, n_steps):
    k_refs = rest[0:PP]
    v_refs = rest[PP:2 * PP]
    f_refs = rest[2 * PP:3 * PP]
    o_ref, lf_ref = rest[3 * PP], rest[3 * PP + 1]
    qall, acc, m_scr, l_scr, carry, crow, hmask = rest[3 * PP + 2:]
    s_id = pl.program_id(1)
    R = T * H
    W = k_refs[0].shape[0]

    @pl.when(s_id == 0)
    def _():
        qall[...] = (q_ref[...].reshape(R, dh) * (dh ** -0.5)).astype(BF16)
        acc[...] = jnp.zeros_like(acc)
        m_scr[...] = jnp.full_like(m_scr, NEG)
        l_scr[...] = jnp.zeros_like(l_scr)
        carry[...] = jnp.zeros_like(carry)
        lf_new = _log_sigmoid(df_ref[...] + fb_ref[...])
        lf_ref[...] = lf_new
        crow[...] = _cumsum_lanes(lf_new, H)
        rr = lax.broadcasted_iota(jnp.int32, (R, W), 0)
        ll = lax.broadcasted_iota(jnp.int32, (R, W), 1)
        hmask[...] = jnp.where((rr % H) == (ll % H), 0.0, NEG)

    lf = jnp.concatenate([f_refs[p][...] for p in range(PP)], axis=0)
    lane = lax.broadcasted_iota(jnp.int32, (PP, W), 1)
    suf = lf
    tot = lf
    s = H
    while s < W:
        suf = suf + jnp.where(lane + s < W, pltpu.roll(suf, W - s, 1), 0.0)
        tot = tot + pltpu.roll(tot, s, 1)
        s *= 2
    tot_cum = _cumsum_rows(tot)
    later = carry[...]
    d_all = suf - lf + (tot_cum - tot) + later
    carry[...] = later + tot_cum[PP - 1:PP, :]

    qa = qall[...]
    hm = hmask[...]
    m_pg, l_pg, a_pg = [], [], []
    for p in range(PP):
        kb = k_refs[p][...].astype(BF16)
        vb = v_refs[p][...].astype(BF16)
        st = _nt(qa, kb) + (hm + d_all[p:p + 1, :])
        m_p = jnp.max(st, axis=1, keepdims=True)
        pm = jnp.exp(st - m_p)
        m_pg.append(m_p)
        l_pg.append(jnp.sum(pm, axis=1, keepdims=True))
        a_pg.append(_nn(pm.astype(BF16), vb))
    m_prev = m_scr[...]
    m_old = m_prev
    for m_p in m_pg:
        m_old = jnp.maximum(m_old, m_p)
    alpha = jnp.exp(m_prev - m_old)
    l_old = alpha * l_scr[...]
    a_old = alpha * acc[...]
    for m_p, l_p, a_p in zip(m_pg, l_pg, a_pg):
        w_p = jnp.exp(m_p - m_old)
        l_old = l_old + w_p * l_p
        a_old = a_old + w_p * a_p
    m_scr[...] = m_old
    l_scr[...] = l_old
    acc[...] = a_old

    @pl.when(s_id == n_steps - 1)
    def _():
        knb = kn_ref[...].reshape(R, dh).astype(BF16)
        vnb = vn_ref[...].reshape(R, dh).astype(BF16)
        r2 = lax.broadcasted_iota(jnp.int32, (R, R), 0)
        l2 = lax.broadcasted_iota(jnp.int32, (R, R), 1)
        ok = ((r2 % H) == (l2 % H)) & (l2 // H <= r2 // H)
        sn = jnp.where(ok, _nt(qa, knb) - crow[...], NEG)
        m_fin = jnp.maximum(m_old, jnp.max(sn, axis=1, keepdims=True))
        a2 = jnp.exp(m_old - m_fin)
        pn = jnp.exp(sn - m_fin)
        l_fin = a2 * l_old + jnp.sum(pn, axis=1, keepdims=True)
        out = (a2 * a_old + _nn(pn.astype(BF16), vnb)) / l_fin
        out = out * _sigmoid(og_ref[...].reshape(R, dh))
        o_ref[...] = out.reshape(T, H, dh).astype(o_ref.dtype)


def fox_sample(proj3, df_flat, fb_flat, cache_k, cache_v, cache_f, page_table, *, B, T, page_off, cols, H, dh, PP):
    R = T * H
    W = cache_k.shape[1]
    n_pages = page_table.shape[1]
    n_steps = n_pages // PP
    cq, ck, cv, co = cols

    def rowspec(c0):
        return pl.BlockSpec((T, H, dh), lambda b, s, pt: (b, c0 // H, 0))

    def page(p):
        return lambda b, s, pt: (page_off + pt[b, n_pages - 1 - (s * PP + p)], 0, 0)

    in_specs = [rowspec(cq), rowspec(ck), rowspec(cv), rowspec(co),
                pl.BlockSpec((None, 1, R), lambda b, s, pt: (b, 0, 0)),
                pl.BlockSpec((1, R), lambda b, s, pt: (0, 0))]
    in_specs += [pl.BlockSpec((None, W, dh), page(p)) for p in range(PP)]
    in_specs += [pl.BlockSpec((None, W, dh), page(p)) for p in range(PP)]
    in_specs += [pl.BlockSpec((None, 1, W), page(p)) for p in range(PP)]
    grid_spec = pltpu.PrefetchScalarGridSpec(
        num_scalar_prefetch=1,
        grid=(B, n_steps),
        in_specs=in_specs,
        out_specs=[pl.BlockSpec((T, H, dh), lambda b, s, pt: (b, 0, 0)),
                   pl.BlockSpec((None, 1, R), lambda b, s, pt: (b, 0, 0))],
        scratch_shapes=[
            pltpu.VMEM((R, dh), BF16),
            pltpu.VMEM((R, dh), F32),
            pltpu.VMEM((R, 1), F32),
            pltpu.VMEM((R, 1), F32),
            pltpu.VMEM((1, W), F32),
            pltpu.VMEM((1, R), F32),
            pltpu.VMEM((R, W), F32),
        ],
    )
    return pl.pallas_call(
        functools.partial(_fox_sample_kernel, PP=PP, T=T, H=H, dh=dh, n_steps=n_steps),
        grid_spec=grid_spec,
        out_shape=[jax.ShapeDtypeStruct((B * T, H, dh), BF16), jax.ShapeDtypeStruct((B, 1, R), F32)],
        compiler_params=_cparams(("parallel", "arbitrary")),
        name="fox_sample",
    )(page_table, proj3, proj3, proj3, proj3, df_flat, fb_flat,
      *([cache_k] * PP), *([cache_v] * PP), *([cache_f] * PP))


def _pad_lanes(w):
    return jnp.pad(w, ((0, 0), (0, LANES - w.shape[1])))


def _row_tile(m, cap):
    best = None
    for t in range(BF16_ROWS, min(m, cap) + 1, BF16_ROWS):
        if m % t == 0:
            best = t
    assert best is not None, (m, cap)
    return best


def _col_tile(n, cap):
    t = cap
    while n % t:
        t //= 2
    return t


def kernel(x_prompt, x_sample, state_mlstm_C, state_mlstm_n, state_mlstm_m, state_gla_S, state_hgrn_S, cache_fox_k, cache_fox_v, cache_fox_logf, page_table, norm_mix_pre, norm_mix_post, norm_ffn_pre, norm_ffn_post, ffn_w_gate, ffn_w_up, ffn_w_down, w_in_even, w_out_even, mlstm_b_i, mlstm_b_f, mlstm_norm, gla_w_gk2, gla_b_gk, gla_norm, w_in_odd, w_out_odd, hgrn_lb_logits, hgrn_norm, fox_b_f):
    bp, tp, D = x_prompt.shape
    bs, ts, _ = x_sample.shape
    mp, ms = bp * tp, bs * ts
    M = mp + ms
    depth = norm_mix_pre.shape[0]
    a_heads, a_dk, a_dv = state_mlstm_C.shape[2:]
    b_heads, b_dk, b_dv = state_gla_S.shape[2:]
    c_heads, c_dk, c_dv = state_hgrn_S.shape[2:]
    d_heads = fox_b_f.shape[1]
    half = D // 2
    d_dh = half // d_heads
    ff = ffn_w_gate.shape[3]
    n_phys, page = cache_fox_k.shape[1:3]

    tm = _row_tile(M, MM_ROWS_CAP)
    tr = _row_tile(M, NORM_ROWS_CAP)
    wd_bf = [ffn_w_down[0, 0].astype(BF16)]

    def ffn(h, layer, j):
        u = ffn_up(h, ffn_w_gate, ffn_w_up, (layer, j), tm=tm, tn=_col_tile(ff, FFN_COLS))
        nxt = (layer, 1) if j == 0 else (layer + 1, 0)
        ride = (ffn_w_down, nxt, CAST_ROWS) if nxt[0] < depth else None
        res = matmul([(u, wd_bf[-1], (), 0)], out_dtype=BF16, tm=tm, tn=_col_tile(D, PROJ_COLS), nk=2, ride=ride)
        if ride is None:
            return res
        wd_bf.append(res[1])
        return res[0]

    groups = ((bp, tp, 0), (bs, ts, mp))
    outs = {k: ([], []) for k in ('c', 'n', 'm', 'g', 'h', 'k', 'v', 'f')}

    x, h = prenorm_join(x_prompt.reshape(mp, D), x_sample.reshape(ms, D), norm_ffn_pre[0, 0], tr=math.gcd(mp, ms))
    for layer in range(depth):
        y = ffn(h, layer, 0)
        x, h = resid_norm(x, y, norm_ffn_post[layer, 0], norm_mix_pre[layer], scale=0.5, tr=tr)
        if layer % 2 == 0:
            e = layer // 2
            wt = jnp.swapaxes(w_in_even, 1, 2)
            n_a = a_heads * (2 * a_dk + 2 * a_dv)
            n_b = b_heads * (2 * b_dk + 2 * b_dv)
            o_b = n_a + 2 * a_heads
            wt_small = jnp.concatenate([wt[e, n_a:o_b], wt[e, o_b + n_b:]], axis=0)
            wt_small = jnp.pad(wt_small, ((0, LANES - wt_small.shape[0]), (0, 0)))
            proj_a = matmul_nt(h, wt, (e,), n=n_a, out_dtype=F32, tm=tm, tn=_col_tile(n_a, PROJ_COLS))
            proj_b = matmul_nt(h, wt[e, o_b:o_b + n_b], (), n=n_b, out_dtype=F32, tm=tm, tn=_col_tile(n_b, PROJ_COLS))
            small = matmul_nt(h, wt_small, (), n=LANES, out_dtype=F32, tm=tm, tn=LANES)
            gate_bias = _pad_lanes(jnp.concatenate([mlstm_b_i[e], mlstm_b_f[e]])[None, :])
            a_cols = (0, a_heads * a_dk, 2 * a_heads * a_dk, 2 * a_heads * a_dk + a_heads * a_dv)
            w2 = gla_w_gk2[e]
            w2pad = jnp.zeros((LANES, b_heads * b_dk), F32).at[2 * a_heads:2 * a_heads + w2.shape[0]].set(w2)
            w2pad = w2pad.reshape(LANES, b_heads, b_dk).transpose(1, 0, 2).astype(BF16)
            b_cols = (0, b_heads * b_dk, 2 * b_heads * b_dk, 2 * b_heads * b_dk + b_heads * b_dv)
            ha, hb = [], []
            for gi, (B, T, r0) in enumerate(groups):
                if gi == 0:
                    c0, n0, m0, s0 = (jnp.zeros((B,) + s.shape[2:], F32)
                                      for s in (state_mlstm_C, state_mlstm_n, state_mlstm_m, state_gla_S))
                else:
                    c0, n0, m0, s0 = state_mlstm_C[e], state_mlstm_n[e], state_mlstm_m[e], state_gla_S[e]
                CH = min(T, SCAN_CHUNK)
                ha_g, c1, n1, m1 = mlstm_mixer(proj_a, small, gate_bias, mlstm_norm[e], c0, n0, m0,
                                               B=B, T=T, row_off=r0, cols=a_cols,
                                               H=a_heads, dk=a_dk, dv=a_dv, L=CH,
                                               HB=math.gcd(a_heads, MLSTM_HEADS_PER_STEP))
                hb_g, s1 = gla_mixer(proj_b, small, w2pad, gla_b_gk[e], gla_norm[e], s0,
                                     B=B, T=T, row_off=r0, cols=b_cols,
                                     H=b_heads, dk=b_dk, dv=b_dv, CH=CH, SB=min(CH, SCAN_SUB),
                                     HB=math.gcd(b_heads, GLA_HEADS_PER_STEP))
                ha.append(ha_g); hb.append(hb_g)
                outs['c'][gi].append(c1); outs['n'][gi].append(n1)
                outs['m'][gi].append(m1); outs['g'][gi].append(s1)
            wo = w_out_even.astype(BF16)
            mix = matmul([(jnp.concatenate(ha, axis=0), wo, (e,), 0), (jnp.concatenate(hb, axis=0), wo, (e,), 1)],
                         out_dtype=BF16, tm=tm, tn=_col_tile(D, PROJ_COLS))
        else:
            o = layer // 2
            n_main = 4 * c_heads * c_dk + 4 * half
            wt = jnp.swapaxes(w_in_odd, 1, 2)
            wt_small = wt[o, n_main:]
            wt_small = jnp.pad(wt_small, ((0, LANES - wt_small.shape[0]), (0, 0)))
            proj = matmul_nt(h, wt, (o,), n=n_main, out_dtype=F32, tm=tm, tn=_col_tile(n_main, PROJ_COLS))
            small = matmul_nt(h, wt_small, (), n=LANES, out_dtype=F32, tm=tm, tn=LANES)
            cw = c_heads * c_dk
            dq = 4 * cw
            d_cols = (dq, dq + half, dq + 2 * half, dq + 3 * half)
            hc, hd = [], []
            for gi, (B, T, r0) in enumerate(groups):
                s0 = jnp.zeros((B,) + state_hgrn_S.shape[2:], F32) if gi == 0 else state_hgrn_S[o]
                CH = min(T, SCAN_CHUNK)
                hc_g, s1 = hgrn_mixer(proj, hgrn_lb_logits, hgrn_norm[o], s0,
                                      B=B, T=T, row_off=r0, cols=(0, cw, 2 * cw, 3 * cw),
                                      H=c_heads, dk=c_dk, dv=c_dv, CH=CH, SB=min(CH, SCAN_SUB),
                                      HB=math.gcd(c_heads, HGRN_HEADS_PER_STEP), layer=layer)
                df = small[r0:r0 + B * T, :d_heads]
                if gi == 0:
                    lf_t, c_t = fox_cumsum(df.reshape(B, T, d_heads).transpose(0, 2, 1), fox_b_f[o])
                    logf = lf_t.transpose(0, 2, 1)
                    hd_g = fox_prompt(proj, c_t.reshape(B, d_heads, 1, T), B=B, T=T, row_off=r0,
                                      cols=d_cols, H=d_heads, dh=d_dh, TQ=min(T, FOX_Q_BLOCK))
                else:
                    n_cb = proj.shape[1] // d_dh
                    proj3 = proj[r0:r0 + B * T].reshape(B * T, n_cb, d_dh)
                    n_odd = cache_fox_k.shape[0]
                    hd3, lf = fox_sample(
                        proj3, df.reshape(B, 1, T * d_heads), jnp.tile(fox_b_f[o], T)[None, :],
                        cache_fox_k.reshape(n_odd * n_phys, page * d_heads, d_dh),
                        cache_fox_v.reshape(n_odd * n_phys, page * d_heads, d_dh),
                        cache_fox_logf.reshape(n_odd * n_phys, 1, page * d_heads),
                        page_table, B=B, T=T, page_off=o * n_phys,
                        cols=tuple(c // d_dh for c in d_cols), H=d_heads, dh=d_dh, PP=FOX_PAGES_PER_STEP)
                    hd_g = hd3.reshape(B * T, half)
                    logf = lf.reshape(B, T, d_heads)
                hc.append(hc_g); hd.append(hd_g)
                outs['h'][gi].append(s1)
                k_new, v_new = split_heads(proj, rows=B * T, row_off=r0, ck=d_cols[1], cv=d_cols[2],
                                           H=d_heads, dh=d_dh, tr=min(B * T, SPLIT_ROWS))
                outs['k'][gi].append(k_new.reshape(B, T, d_heads, d_dh))
                outs['v'][gi].append(v_new.reshape(B, T, d_heads, d_dh))
                outs['f'][gi].append(logf)
            wo = w_out_odd.astype(BF16)
            mix = matmul([(jnp.concatenate(hc, axis=0), wo, (o,), 0), (jnp.concatenate(hd, axis=0), wo, (o,), 1)],
                         out_dtype=BF16, tm=tm, tn=_col_tile(D, PROJ_COLS))
        x, h = resid_norm(x, mix, norm_mix_post[layer], norm_ffn_pre[layer, 1], scale=1.0, tr=tr)
        y = ffn(h, layer, 1)
        if layer + 1 < depth:
            x, h = resid_norm(x, y, norm_ffn_post[layer, 1], norm_ffn_pre[layer + 1, 0], scale=0.5, tr=tr)
        else:
            y_p, y_s = resid_split(x, y, norm_ffn_post[layer, 1], scale=0.5, rows_a=mp, tr=math.gcd(mp, ms))

    names = ('c', 'n', 'm', 'g', 'h', 'k', 'v', 'f')
    res_p = tuple(jnp.stack(outs[k][0]) for k in names)
    res_s = tuple(jnp.stack(outs[k][1]) for k in names)
    return (y_p.reshape(bp, tp, D), y_s.reshape(bs, ts, D)) + res_p + res_s
```

```python
import functools
import math

import jax
import jax.numpy as jnp
from jax import lax
from jax.experimental import pallas as pl
from jax.experimental.pallas import tpu as pltpu

F32 = jnp.float32
BF16 = jnp.bfloat16
EPS = 1e-6
NEG = -1e30
LANES = 128
BF16_ROWS = 16
VMEM_LIMIT = 56 * 1024 * 1024

B_GATE_NORMALIZER = 16.0

MM_ROWS_CAP = 1408
NORM_ROWS_CAP = 256
FFN_COLS = 256
PROJ_COLS = 512
SCAN_CHUNK = 128
SCAN_SUB = 32
FOX_Q_BLOCK = 256
FOX_PAGES_PER_STEP = 8
MLSTM_HEADS_PER_STEP = 4
GLA_HEADS_PER_STEP = 4
HGRN_HEADS_PER_STEP = 8
SPLIT_ROWS = 256
CAST_ROWS = 128


def _cparams(sem):
    return pltpu.CompilerParams(dimension_semantics=sem, vmem_limit_bytes=VMEM_LIMIT)


def _log_sigmoid(x):
    return jnp.minimum(x, 0.0) - jnp.log1p(jnp.exp(-jnp.abs(x)))


def _sigmoid(x):
    return jax.nn.sigmoid(x)


def _nt(a, b):
    return lax.dot_general(a, b, (((1,), (1,)), ((), ())), preferred_element_type=F32)


def _tn(a, b):
    return lax.dot_general(a, b, (((0,), (0,)), ((), ())), preferred_element_type=F32)


def _nn(a, b):
    return jnp.dot(a, b, preferred_element_type=F32)


def _cumsum_rows(x):
    n = x.shape[0]
    row = lax.broadcasted_iota(jnp.int32, x.shape, 0)
    s = 1
    while s < n:
        x = x + jnp.where(row >= s, pltpu.roll(x, s, 0), 0.0)
        s *= 2
    return x


def _cumsum_lanes(x, step):
    n = x.shape[1]
    lane = lax.broadcasted_iota(jnp.int32, x.shape, 1)
    s = step
    while s < n:
        x = x + jnp.where(lane >= s, pltpu.roll(x, s, 1), 0.0)
        s *= 2
    return x


def _prenorm_join_kernel(xa_ref, xb_ref, g_ref, xo_ref, ho_ref, *, n_a):
    x = jnp.where(pl.program_id(0) < n_a, xa_ref[...], xb_ref[...])
    xo_ref[...] = x
    r = lax.rsqrt(jnp.mean(x * x, axis=-1, keepdims=True) + EPS)
    ho_ref[...] = (x * r * g_ref[...]).astype(ho_ref.dtype)


def prenorm_join(xa, xb, g, *, tr):
    (ma, d), mb = xa.shape, xb.shape[0]
    n_a = ma // tr
    row = pl.BlockSpec((tr, d), lambda i: (i, 0))
    return pl.pallas_call(
        functools.partial(_prenorm_join_kernel, n_a=n_a),
        grid=((ma + mb) // tr,),
        in_specs=[pl.BlockSpec((tr, d), lambda i: (jnp.minimum(i, n_a - 1), 0)),
                  pl.BlockSpec((tr, d), lambda i: (jnp.maximum(i - n_a, 0), 0)),
                  pl.BlockSpec((1, d), lambda i: (0, 0))],
        out_specs=[row, row],
        out_shape=[jax.ShapeDtypeStruct((ma + mb, d), F32), jax.ShapeDtypeStruct((ma + mb, d), BF16)],
        compiler_params=_cparams(("parallel",)),
        name="prenorm_join",
    )(xa, xb, g.reshape(1, d))


def _resid_kernel(x_ref, y_ref, gp_ref, gn_ref, xo_ref, ho_ref, *, scale):
    y = y_ref[...].astype(F32)
    r = lax.rsqrt(jnp.mean(y * y, axis=-1, keepdims=True) + EPS)
    xn = x_ref[...] + scale * (y * r * gp_ref[...])
    xo_ref[...] = xn
    r2 = lax.rsqrt(jnp.mean(xn * xn, axis=-1, keepdims=True) + EPS)
    ho_ref[...] = (xn * r2 * gn_ref[...]).astype(ho_ref.dtype)


def resid_norm(x, y, g_post, g_next, *, scale, tr):
    m, d = x.shape
    row = pl.BlockSpec((tr, d), lambda i: (i, 0))
    vec = pl.BlockSpec((1, d), lambda i: (0, 0))
    return pl.pallas_call(
        functools.partial(_resid_kernel, scale=scale),
        grid=(m // tr,),
        in_specs=[row, row, vec, vec],
        out_specs=[row, row],
        out_shape=[jax.ShapeDtypeStruct((m, d), F32), jax.ShapeDtypeStruct((m, d), BF16)],
        compiler_params=_cparams(("parallel",)),
        name="resid_norm",
    )(x, y, g_post.reshape(1, d), g_next.reshape(1, d))


def _resid_split_kernel(x_ref, y_ref, gp_ref, xa_ref, xb_ref, *, scale, n_a):
    i = pl.program_id(0)
    y = y_ref[...].astype(F32)
    r = lax.rsqrt(jnp.mean(y * y, axis=-1, keepdims=True) + EPS)
    xn = x_ref[...] + scale * (y * r * gp_ref[...])

    @pl.when(i < n_a)
    def _():
        xa_ref[...] = xn

    @pl.when(i >= n_a)
    def _():
        xb_ref[...] = xn


def resid_split(x, y, g_post, *, scale, rows_a, tr):
    m, d = x.shape
    n_a = rows_a // tr
    row = pl.BlockSpec((tr, d), lambda i: (i, 0))
    return pl.pallas_call(
        functools.partial(_resid_split_kernel, scale=scale, n_a=n_a),
        grid=(m // tr,),
        in_specs=[row, row, pl.BlockSpec((1, d), lambda i: (0, 0))],
        out_specs=[pl.BlockSpec((tr, d), lambda i: (jnp.minimum(i, n_a - 1), 0)),
                   pl.BlockSpec((tr, d), lambda i: (jnp.maximum(i - n_a, 0), 0))],
        out_shape=[jax.ShapeDtypeStruct((rows_a, d), F32), jax.ShapeDtypeStruct((m - rows_a, d), F32)],
        compiler_params=_cparams(("arbitrary",)),
        name="resid_split",
    )(x, y, g_post.reshape(1, d))


def _split_heads_kernel(k_ref, v_ref, ko_ref, vo_ref, *, H, dh):
    for h in range(H):
        ko_ref[:, h, :] = k_ref[:, h * dh:(h + 1) * dh]
        vo_ref[:, h, :] = v_ref[:, h * dh:(h + 1) * dh]


def split_heads(proj, *, rows, row_off, ck, cv, H, dh, tr):
    w = H * dh
    out = jax.ShapeDtypeStruct((rows, H, dh), proj.dtype)
    ospec = pl.BlockSpec((tr, H, dh), lambda i: (i, 0, 0))
    return pl.pallas_call(
        functools.partial(_split_heads_kernel, H=H, dh=dh),
        grid=(rows // tr,),
        in_specs=[pl.BlockSpec((tr, w), lambda i: (row_off // tr + i, ck // w)),
                  pl.BlockSpec((tr, w), lambda i: (row_off // tr + i, cv // w))],
        out_specs=[ospec, ospec],
        out_shape=[out, out],
        compiler_params=_cparams(("parallel",)),
        name="split_heads",
    )(proj, proj)


def _mm_kernel(*refs, n_pairs, nk, n_ride):
    refs = list(refs)
    pair_refs = [refs.pop(0) for _ in range(2 * n_pairs)]
    ride_in = refs.pop(0) if n_ride else None
    o_ref = refs.pop(0)
    ride_out = refs.pop(0) if n_ride else None
    if n_ride:
        step = (pl.program_id(0) * pl.num_programs(1) + pl.program_id(1)) * nk + pl.program_id(2)

        @pl.when(step < n_ride)
        def _():
            ride_out[...] = ride_in[...].astype(ride_out.dtype)

    acc = _nn(pair_refs[0][...], pair_refs[1][...])
    for p in range(1, n_pairs):
        acc = acc + _nn(pair_refs[2 * p][...], pair_refs[2 * p + 1][...])
    if nk == 1:
        o_ref[...] = acc.astype(o_ref.dtype)
    else:
        acc_ref = refs.pop(0)
        k = pl.program_id(2)

        @pl.when(k == 0)
        def _():
            acc_ref[...] = acc

        @pl.when((k > 0) & (k < nk - 1))
        def _():
            acc_ref[...] += acc

        @pl.when(k == nk - 1)
        def _():
            o_ref[...] = (acc_ref[...] + acc).astype(o_ref.dtype)


def matmul(pairs, *, out_dtype, tm, tn, nk=1, ride=None):
    m = pairs[0][0].shape[0]
    n = pairs[0][1].shape[-1]
    grid = (m // tm, n // tn, nk)
    in_specs, args = [], []
    for x, w, lead, kb0 in pairs:
        tk = x.shape[1] // nk
        in_specs.append(pl.BlockSpec((tm, tk), lambda i, j, k: (i, k)))
        in_specs.append(pl.BlockSpec((None,) * len(lead) + (tk, tn),
                                     lambda i, j, k, lead=lead, kb0=kb0, nk=nk: lead + (kb0 * nk + k, j)))
        args += [x, w]
    out_specs = [pl.BlockSpec((tm, tn), lambda i, j, k: (i, j))]
    out_shape = [jax.ShapeDtypeStruct((m, n), out_dtype)]
    n_ride = 0
    if ride is not None:
        a, lead, rows = ride
        r, c = a.shape[-2:]
        n_ride = r // rows
        assert n_ride * rows == r and n_ride <= grid[0] * grid[1] * grid[2]

        def blk(i, j, k):
            return jnp.minimum((i * grid[1] + j) * nk + k, n_ride - 1)

        in_specs.append(pl.BlockSpec((None,) * len(lead) + (rows, c), lambda i, j, k: lead + (blk(i, j, k), 0)))
        args.append(a)
        out_specs.append(pl.BlockSpec((rows, c), lambda i, j, k: (blk(i, j, k), 0)))
        out_shape.append(jax.ShapeDtypeStruct((r, c), BF16))
    res = pl.pallas_call(
        functools.partial(_mm_kernel, n_pairs=len(pairs), nk=nk, n_ride=n_ride),
        grid=grid,
        in_specs=in_specs,
        out_specs=out_specs,
        out_shape=out_shape,
        scratch_shapes=[pltpu.VMEM((tm, tn), F32)] if nk > 1 else [],
        compiler_params=_cparams(("arbitrary", "arbitrary", "arbitrary") if n_ride else
                                 ("parallel", "parallel", "arbitrary")),
        name="matmul",
    )(*args)
    return res if n_ride else res[0]


def _mm_nt_kernel(x_ref, wt_ref, o_ref, w_bf):
    @pl.when(pl.program_id(1) == 0)
    def _():
        w_bf[...] = wt_ref[...].astype(BF16)

    o_ref[...] = _nt(x_ref[...], w_bf[...]).astype(o_ref.dtype)


def _snake(j, i, n_i):
    return jnp.where(j % 2 == 0, i, n_i - 1 - i)


def matmul_nt(x, wt, lead, *, n, out_dtype, tm, tn):
    m, k = x.shape
    n_i = m // tm
    return pl.pallas_call(
        _mm_nt_kernel,
        grid=(n // tn, n_i),
        in_specs=[pl.BlockSpec((tm, k), lambda j, i: (_snake(j, i, n_i), 0)),
                  pl.BlockSpec((None,) * len(lead) + (tn, k), lambda j, i: lead + (j, 0))],
        out_specs=pl.BlockSpec((tm, tn), lambda j, i: (_snake(j, i, n_i), j)),
        out_shape=jax.ShapeDtypeStruct((m, n), out_dtype),
        scratch_shapes=[pltpu.VMEM((tn, k), BF16)],
        compiler_params=_cparams(("parallel", "arbitrary")),
        name="matmul_nt",
    )(x, wt)


def _ffn_up_kernel(h_ref, wg_ref, wu_ref, o_ref, wg_bf, wu_bf):
    @pl.when(pl.program_id(1) == 0)
    def _():
        wg_bf[...] = wg_ref[...].astype(BF16)
        wu_bf[...] = wu_ref[...].astype(BF16)

    h = h_ref[...]
    g = _nn(h, wg_bf[...])
    u = _nn(h, wu_bf[...])
    o_ref[...] = (g * _sigmoid(g) * u).astype(o_ref.dtype)


def ffn_up(h, wg, wu, lead, *, tm, tn):
    m, k = h.shape
    n = wg.shape[-1]
    n_i = m // tm
    wspec = pl.BlockSpec((None,) * len(lead) + (k, tn), lambda j, i: lead + (0, j))
    return pl.pallas_call(
        _ffn_up_kernel,
        grid=(n // tn, n_i),
        in_specs=[pl.BlockSpec((tm, k), lambda j, i: (_snake(j, i, n_i), 0)), wspec, wspec],
        out_specs=pl.BlockSpec((tm, tn), lambda j, i: (_snake(j, i, n_i), j)),
        out_shape=jax.ShapeDtypeStruct((m, n), BF16),
        scratch_shapes=[pltpu.VMEM((k, tn), BF16), pltpu.VMEM((k, tn), BF16)],
        compiler_params=_cparams(("parallel", "arbitrary")),
        name="ffn_up",
    )(h, wg, wu)


def _mlstm_kernel(q_ref, k_ref, v_ref, og_ref, gt_ref, gb_ref, nrm_ref, c0_ref, n0_ref, m0_ref,
                  h_ref, c_ref, n_ref, m_ref, *, L, HB, dk, dv, n_heads):
    @pl.when(pl.program_id(2) == 0)
    def _():
        c_ref[...] = c0_ref[...]
        n_ref[...] = n0_ref[...]
        m_ref[...] = m0_ref[...]

    gates = gt_ref[...] + gb_ref[...]
    lane = lax.broadcasted_iota(jnp.int32, gates.shape, 1)
    r = lax.broadcasted_iota(jnp.int32, (L, L), 0)
    s = lax.broadcasted_iota(jnp.int32, (L, L), 1)
    tril = r >= s
    eye = r == s

    def col2row(col):
        return jnp.sum(jnp.where(eye, col, 0.0), axis=0, keepdims=True)

    for hb in range(HB):
        hh = pl.program_id(1) * HB + hb
        ks = slice(hb * dk, (hb + 1) * dk)
        vs = slice(hb * dv, (hb + 1) * dv)
        i_col = jnp.sum(jnp.where(lane == hh, gates, 0.0), axis=1, keepdims=True)
        f_col = jnp.sum(jnp.where(lane == hh + n_heads, gates, 0.0), axis=1, keepdims=True)
        logf = _log_sigmoid(f_col)
        logf_row = col2row(logf)
        b_col = jnp.sum(jnp.where(tril, logf_row, 0.0), axis=1, keepdims=True)
        b_row = col2row(b_col)
        i_row = col2row(i_col)
        m_prev = m_ref[hb]
        dmat = jnp.where(tril, b_col - b_row + i_row, -jnp.inf)
        from_state = b_col + m_prev
        m_t = jnp.maximum(from_state, jnp.max(dmat, axis=1, keepdims=True))

        q = q_ref[:, ks]
        k = k_ref[:, ks] * (dk ** -0.5)
        qb = q.astype(BF16)
        vb = v_ref[:, vs].astype(BF16)
        w = jnp.exp(dmat - m_t) * _nt(qb, k.astype(BF16))
        s_state = jnp.exp(from_state - m_t)
        c_old = c_ref[hb]
        n_old = n_ref[hb]
        num = s_state * _nn(qb, c_old.astype(BF16)) + _nn(w.astype(BF16), vb)
        den = s_state * jnp.sum(q * n_old, axis=1, keepdims=True) + jnp.sum(w, axis=1, keepdims=True)
        h = num / jnp.maximum(jnp.abs(den), jnp.exp(-m_t))

        m_new = m_t[L - 1:L, :]
        b_last = b_col[L - 1:L, :]
        w_end = jnp.exp(b_last - b_col + i_col - m_new)
        s_end = jnp.exp(b_last + m_prev - m_new)
        kw = k * w_end
        c_ref[hb] = s_end * c_old + _tn(kw.astype(BF16), vb)
        n_ref[hb] = s_end * n_old + jnp.sum(kw, axis=0, keepdims=True)
        m_ref[hb] = m_new

        hn = h * lax.rsqrt(jnp.mean(h * h, axis=1, keepdims=True) + EPS) * nrm_ref[hb]
        h_ref[:, vs] = (hn * _sigmoid(og_ref[:, vs])).astype(h_ref.dtype)


def mlstm_mixer(proj, small, gate_bias, norm, c0, n0, m0, *, B, T, row_off, cols, H, dk, dv, L, HB):
    nC = T // L
    rb = row_off // L
    cq, ck, cv, co = cols
    wk, wv = HB * dk, HB * dv

    def rows(b, h, c):
        return rb + b * nC + c

    in_specs = [
        pl.BlockSpec((L, wk), lambda b, h, c: (rows(b, h, c), cq // wk + h)),
        pl.BlockSpec((L, wk), lambda b, h, c: (rows(b, h, c), ck // wk + h)),
        pl.BlockSpec((L, wv), lambda b, h, c: (rows(b, h, c), cv // wv + h)),
        pl.BlockSpec((L, wv), lambda b, h, c: (rows(b, h, c), co // wv + h)),
        pl.BlockSpec((L, LANES), lambda b, h, c: (rows(b, h, c), 0)),
        pl.BlockSpec((1, LANES), lambda b, h, c: (0, 0)),
        pl.BlockSpec((HB, 1, dv), lambda b, h, c: (h, 0, 0)),
        pl.BlockSpec((None, HB, dk, dv), lambda b, h, c: (b, h, 0, 0)),
        pl.BlockSpec((None, HB, 1, dk), lambda b, h, c: (b, h, 0, 0)),
        pl.BlockSpec((None, HB, 1, 1), lambda b, h, c: (b, h, 0, 0)),
    ]
    out_specs = [
        pl.BlockSpec((L, wv), lambda b, h, c: (b * nC + c, h)),
        pl.BlockSpec((None, HB, dk, dv), lambda b, h, c: (b, h, 0, 0)),
        pl.BlockSpec((None, HB, 1, dk), lambda b, h, c: (b, h, 0, 0)),
        pl.BlockSpec((None, HB, 1, 1), lambda b, h, c: (b, h, 0, 0)),
    ]
    out_shape = [
        jax.ShapeDtypeStruct((B * T, H * dv), BF16),
        jax.ShapeDtypeStruct((B, H, dk, dv), F32),
        jax.ShapeDtypeStruct((B, H, 1, dk), F32),
        jax.ShapeDtypeStruct((B, H, 1, 1), F32),
    ]
    h, c1, n1, m1 = pl.pallas_call(
        functools.partial(_mlstm_kernel, L=L, HB=HB, dk=dk, dv=dv, n_heads=H),
        grid=(B, H // HB, nC),
        in_specs=in_specs, out_specs=out_specs, out_shape=out_shape,
        compiler_params=_cparams(("parallel", "parallel", "arbitrary")),
        name="mlstm",
    )(proj, proj, proj, proj, small, gate_bias, norm.reshape(H, 1, dv), c0,
      n0.reshape(B, H, 1, dk), m0.reshape(B, H, 1, 1))
    return h, c1, n1.reshape(B, H, dk), m1.reshape(B, H)


def _scan_chunk(q, k, v, g, st, *, SB):
    CH = q.shape[0]
    bc = _cumsum_rows(g)
    b_last = bc[CH - 1:CH, :]
    vb = v.astype(BF16)
    o = _nt((q * jnp.exp(bc)).astype(BF16), st.astype(BF16))
    row = lax.broadcasted_iota(jnp.int32, bc.shape, 0)
    srow = lax.broadcasted_iota(jnp.int32, (SB, CH), 0)
    scol = lax.broadcasted_iota(jnp.int32, (SB, CH), 1)
    blocks = []
    for i in range(CH // SB):
        lo, hi = i * SB, (i + 1) * SB
        ref = bc[lo - 1:lo, :] if i > 0 else jnp.zeros_like(b_last)
        qi = q[lo:hi] * jnp.exp(bc[lo:hi] - ref)
        ki = k * jnp.exp(jnp.where(row < hi, ref - bc, 0.0))
        a = _nt(qi.astype(BF16), ki.astype(BF16))
        blocks.append(jnp.where(scol <= srow + lo, a, 0.0))
    a_full = blocks[0] if len(blocks) == 1 else jnp.concatenate(blocks, axis=0)
    o = o + _nn(a_full.astype(BF16), vb)
    k_dec = k * jnp.exp(b_last - bc)
    st_new = st * jnp.exp(b_last) + _tn(vb, k_dec.astype(BF16))
    return o, st_new


def _head_out(o, nrm, gate_act):
    return o * lax.rsqrt(jnp.mean(o * o, axis=1, keepdims=True) + EPS) * nrm * gate_act


def _gla_kernel(q_ref, k_ref, v_ref, og_ref, gt_ref, w2_ref, bgk_ref, nrm_ref, s0_ref,
                o_ref, s_ref, st_scr, *, SB, HB, dk, dv, n_chunks):
    c = pl.program_id(2)

    @pl.when(c == 0)
    def _():
        for hb in range(HB):
            st_scr[hb] = s0_ref[hb].T

    gates = gt_ref[...].astype(BF16)
    for hb in range(HB):
        ks = slice(hb * dk, (hb + 1) * dk)
        vs = slice(hb * dv, (hb + 1) * dv)
        gk = _nn(gates, w2_ref[hb]) + bgk_ref[hb]
        g = _log_sigmoid(gk) / B_GATE_NORMALIZER
        o, st_new = _scan_chunk(q_ref[:, ks] * (dk ** -0.5), k_ref[:, ks], v_ref[:, vs], g, st_scr[hb], SB=SB)
        st_scr[hb] = st_new
        og = og_ref[:, vs]
        o_ref[:, vs] = _head_out(o, nrm_ref[hb], og * _sigmoid(og)).astype(o_ref.dtype)

    @pl.when(c == n_chunks - 1)
    def _():
        for hb in range(HB):
            s_ref[hb] = st_scr[hb].T


def gla_mixer(proj, small, w2pad, b_gk, norm, s0, *, B, T, row_off, cols, H, dk, dv, CH, SB, HB):
    nC = T // CH
    rb = row_off // CH
    cq, ck, cv, co = cols
    wk, wv = HB * dk, HB * dv

    def rows(b, h, c):
        return rb + b * nC + c

    in_specs = [
        pl.BlockSpec((CH, wk), lambda b, h, c: (rows(b, h, c), cq // wk + h)),
        pl.BlockSpec((CH, wk), lambda b, h, c: (rows(b, h, c), ck // wk + h)),
        pl.BlockSpec((CH, wv), lambda b, h, c: (rows(b, h, c), cv // wv + h)),
        pl.BlockSpec((CH, wv), lambda b, h, c: (rows(b, h, c), co // wv + h)),
        pl.BlockSpec((CH, LANES), lambda b, h, c: (rows(b, h, c), 0)),
        pl.BlockSpec((HB, LANES, dk), lambda b, h, c: (h, 0, 0)),
        pl.BlockSpec((HB, 1, dk), lambda b, h, c: (h, 0, 0)),
        pl.BlockSpec((HB, 1, dv), lambda b, h, c: (h, 0, 0)),
        pl.BlockSpec((None, HB, dk, dv), lambda b, h, c: (b, h, 0, 0)),
    ]
    out_specs = [
        pl.BlockSpec((CH, wv), lambda b, h, c: (b * nC + c, h)),
        pl.BlockSpec((None, HB, dk, dv), lambda b, h, c: (b, h, 0, 0)),
    ]
    out_shape = [jax.ShapeDtypeStruct((B * T, H * dv), BF16), jax.ShapeDtypeStruct((B, H, dk, dv), F32)]
    return pl.pallas_call(
        functools.partial(_gla_kernel, SB=SB, HB=HB, dk=dk, dv=dv, n_chunks=nC),
        grid=(B, H // HB, nC),
        in_specs=in_specs, out_specs=out_specs, out_shape=out_shape,
        scratch_shapes=[pltpu.VMEM((HB, dv, dk), F32)],
        compiler_params=_cparams(("parallel", "parallel", "arbitrary")),
        name="gla",
    )(proj, proj, proj, proj, small, w2pad, b_gk.reshape(H, 1, dk), norm.reshape(H, 1, dv), s0)


def _hgrn_kernel(q_ref, f_ref, i_ref, og_ref, lg_ref, nrm_ref, s0_ref, o_ref, s_ref, st_scr,
                 *, SB, HB, dk, layer, n_chunks):
    c = pl.program_id(2)
    lg = lg_ref[...]
    e = jnp.exp(lg - jnp.max(lg, axis=0, keepdims=True))
    sm = e / jnp.sum(e, axis=0, keepdims=True)
    lb_all = jnp.sum(sm[1:layer + 1], axis=0, keepdims=True) if layer > 0 else jnp.zeros_like(sm[0:1])
    @pl.when(c == 0)
    def _():
        for hb in range(HB):
            st_scr[hb] = s0_ref[hb].T

    for hb in range(HB):
        sl = slice(hb * dk, (hb + 1) * dk)
        lb = lb_all[:, sl]
        z = f_ref[:, sl]
        g = jnp.log(lb + (1.0 - lb) * _sigmoid(z))
        key = (1.0 - lb) * _sigmoid(-z)
        cq = q_ref[:, sl]
        o, st_new = _scan_chunk(cq * _sigmoid(cq), key, i_ref[:, sl], g, st_scr[hb], SB=SB)
        st_scr[hb] = st_new
        og = og_ref[:, sl]
        o_ref[:, sl] = _head_out(o, nrm_ref[hb], og * _sigmoid(og)).astype(o_ref.dtype)

    @pl.when(c == n_chunks - 1)
    def _():
        for hb in range(HB):
            s_ref[hb] = st_scr[hb].T


def hgrn_mixer(proj, lb_logits, norm, s0, *, B, T, row_off, cols, H, dk, dv, CH, SB, HB, layer):
    assert dk == dv
    nC = T // CH
    rb = row_off // CH
    cq, cf, ci, co = cols
    depth = lb_logits.shape[0]
    wb = HB * dk

    def col(c0):
        return pl.BlockSpec((CH, wb), lambda b, h, c: (rb + b * nC + c, c0 // wb + h))

    in_specs = [
        col(cq), col(cf), col(ci), col(co),
        pl.BlockSpec((depth, wb), lambda b, h, c: (0, h)),
        pl.BlockSpec((HB, 1, dv), lambda b, h, c: (h, 0, 0)),
        pl.BlockSpec((None, HB, dk, dv), lambda b, h, c: (b, h, 0, 0)),
    ]
    out_specs = [
        pl.BlockSpec((CH, wb), lambda b, h, c: (b * nC + c, h)),
        pl.BlockSpec((None, HB, dk, dv), lambda b, h, c: (b, h, 0, 0)),
    ]
    out_shape = [jax.ShapeDtypeStruct((B * T, H * dv), BF16), jax.ShapeDtypeStruct((B, H, dk, dv), F32)]
    return pl.pallas_call(
        functools.partial(_hgrn_kernel, SB=SB, HB=HB, dk=dk, layer=layer, n_chunks=nC),
        grid=(B, H // HB, nC),
        in_specs=in_specs, out_specs=out_specs, out_shape=out_shape,
        scratch_shapes=[pltpu.VMEM((HB, dv, dk), F32)],
        compiler_params=_cparams(("parallel", "parallel", "arbitrary")),
        name="hgrn",
    )(proj, proj, proj, proj, lb_logits, norm.reshape(H, 1, dv), s0)


def _fox_cumsum_kernel(x_ref, b_ref, lf_ref, c_ref):
    lf = _log_sigmoid(x_ref[...] + b_ref[...])
    lf_ref[...] = lf
    c_ref[...] = _cumsum_lanes(lf, 1)


def fox_cumsum(df_t, bias):
    B, H, T = df_t.shape
    spec = pl.BlockSpec((None, H, T), lambda b: (b, 0, 0))
    return pl.pallas_call(
        _fox_cumsum_kernel,
        grid=(B,),
        in_specs=[spec, pl.BlockSpec((H, 1), lambda b: (0, 0))],
        out_specs=[spec, spec],
        out_shape=[jax.ShapeDtypeStruct((B, H, T), F32)] * 2,
        compiler_params=_cparams(("parallel",)),
        name="fox_cumsum",
    )(df_t, bias.reshape(H, 1))


def _fox_prompt_kernel(q_ref, k_ref, v_ref, og_ref, cr_ref, o_ref, *, TQ, T, dh):
    kb = k_ref[...].astype(BF16)
    vb = v_ref[...].astype(BF16)
    c_row = cr_ref[...]
    for i in range(T // TQ):
        lo, hi = i * TQ, (i + 1) * TQ
        q = (q_ref[lo:hi, :] * (dh ** -0.5)).astype(BF16)
        s = _nt(q, kb[:hi]) - c_row[:, :hi]
        qpos = lo + lax.broadcasted_iota(jnp.int32, (TQ, hi), 0)
        kpos = lax.broadcasted_iota(jnp.int32, (TQ, hi), 1)
        s = jnp.where(kpos <= qpos, s, NEG)
        p = jnp.exp(s - jnp.max(s, axis=1, keepdims=True))
        l = jnp.sum(p, axis=1, keepdims=True)
        o = _nn(p.astype(BF16), vb[:hi]) / l
        o_ref[lo:hi, :] = (o * _sigmoid(og_ref[lo:hi, :])).astype(o_ref.dtype)


def fox_prompt(proj, c_row, *, B, T, row_off, cols, H, dh, TQ):
    cq, ck, cv, co = cols
    rb = row_off // T

    def col(c0):
        return pl.BlockSpec((T, dh), lambda b, h: (rb + b, c0 // dh + h))

    in_specs = [col(cq), col(ck), col(cv), col(co),
                pl.BlockSpec((None, None, 1, T), lambda b, h: (b, h, 0, 0))]
    return pl.pallas_call(
        functools.partial(_fox_prompt_kernel, TQ=TQ, T=T, dh=dh),
        grid=(B, H),
        in_specs=in_specs,
        out_specs=pl.BlockSpec((T, dh), lambda b, h: (b, h)),
        out_shape=jax.ShapeDtypeStruct((B * T, H * dh), BF16),
        compiler_params=_cparams(("parallel", "parallel")),
        name="fox_prompt",
    )(proj, proj, proj, proj, c_row)


def _fox_sample_kernel(pt_ref, q_ref, kn_ref, vn_ref, og_ref, df_ref, fb_ref, *rest, PP, T, H, dh, n_steps):
    k_refs = rest[0:PP]
    v_refs = rest[PP:2 * PP]
    f_refs = rest[2 * PP:3 * PP]
    o_ref, lf_ref = rest[3 * PP], rest[3 * PP + 1]
    qall, acc, m_scr, l_scr, carry, crow, hmask = rest[3 * PP + 2:]
    s_id = pl.program_id(1)
    R = T * H
    W = k_refs[0].shape[0]

    @pl.when(s_id == 0)
    def _():
        qall[...] = (q_ref[...].reshape(R, dh) * (dh ** -0.5)).astype(BF16)
        acc[...] = jnp.zeros_like(acc)
        m_scr[...] = jnp.full_like(m_scr, NEG)
        l_scr[...] = jnp.zeros_like(l_scr)
        carry[...] = jnp.zeros_like(carry)
        lf_new = _log_sigmoid(df_ref[...] + fb_ref[...])
        lf_ref[...] = lf_new
        crow[...] = _cumsum_lanes(lf_new, H)
        rr = lax.broadcasted_iota(jnp.int32, (R, W), 0)
        ll = lax.broadcasted_iota(jnp.int32, (R, W), 1)
        hmask[...] = jnp.where((rr % H) == (ll % H), 0.0, NEG)

    lf = jnp.concatenate([f_refs[p][...] for p in range(PP)], axis=0)
    lane = lax.broadcasted_iota(jnp.int32, (PP, W), 1)
    suf = lf
    tot = lf
    s = H
    while s < W:
        suf = suf + jnp.where(lane + s < W, pltpu.roll(suf, W - s, 1), 0.0)
        tot = tot + pltpu.roll(tot, s, 1)
        s *= 2
    tot_cum = _cumsum_rows(tot)
    later = carry[...]
    d_all = suf - lf + (tot_cum - tot) + later
    carry[...] = later + tot_cum[PP - 1:PP, :]

    qa = qall[...]
    hm = hmask[...]
    m_pg, l_pg, a_pg = [], [], []
    for p in range(PP):
        kb = k_refs[p][...].astype(BF16)
        vb = v_refs[p][...].astype(BF16)
        st = _nt(qa, kb) + (hm + d_all[p:p + 1, :])
        m_p = jnp.max(st, axis=1, keepdims=True)
        pm = jnp.exp(st - m_p)
        m_pg.append(m_p)
        l_pg.append(jnp.sum(pm, axis=1, keepdims=True))
        a_pg.append(_nn(pm.astype(BF16), vb))
    m_prev = m_scr[...]
    m_old = m_prev
    for m_p in m_pg:
        m_old = jnp.maximum(m_old, m_p)
    alpha = jnp.exp(m_prev - m_old)
    l_old = alpha * l_scr[...]
    a_old = alpha * acc[...]
    for m_p, l_p, a_p in zip(m_pg, l_pg, a_pg):
        w_p = jnp.exp(m_p - m_old)
        l_old = l_old + w_p * l_p
        a_old = a_old + w_p * a_p
    m_scr[...] = m_old
    l_scr[...] = l_old
    acc[...] = a_old

    @pl.when(s_id == n_steps - 1)
    def _():
        knb = kn_ref[...].reshape(R, dh).astype(BF16)
        vnb = vn_ref[...].reshape(R, dh).astype(BF16)
        r2 = lax.broadcasted_iota(jnp.int32, (R, R), 0)
        l2 = lax.broadcasted_iota(jnp.int32, (R, R), 1)
        ok = ((r2 % H) == (l2 % H)) & (l2 // H <= r2 // H)
        sn = jnp.where(ok, _nt(qa, knb) - crow[...], NEG)
        m_fin = jnp.maximum(m_old, jnp.max(sn, axis=1, keepdims=True))
        a2 = jnp.exp(m_old - m_fin)
        pn = jnp.exp(sn - m_fin)
        l_fin = a2 * l_old + jnp.sum(pn, axis=1, keepdims=True)
        out = (a2 * a_old + _nn(pn.astype(BF16), vnb)) / l_fin
        out = out * _sigmoid(og_ref[...].reshape(R, dh))
        o_ref[...] = out.reshape(T, H, dh).astype(o_ref.dtype)


def fox_sample(proj3, df_flat, fb_flat, cache_k, cache_v, cache_f, page_table, *, B, T, page_off, cols, H, dh, PP):
    R = T * H
    W = cache_k.shape[1]
    n_pages = page_table.shape[1]
    n_steps = n_pages // PP
    cq, ck, cv, co = cols

    def rowspec(c0):
        return pl.BlockSpec((T, H, dh), lambda b, s, pt: (b, c0 // H, 0))

    def page(p):
        return lambda b, s, pt: (page_off + pt[b, n_pages - 1 - (s * PP + p)], 0, 0)

    in_specs = [rowspec(cq), rowspec(ck), rowspec(cv), rowspec(co),
                pl.BlockSpec((None, 1, R), lambda b, s, pt: (b, 0, 0)),
                pl.BlockSpec((1, R), lambda b, s, pt: (0, 0))]
    in_specs += [pl.BlockSpec((None, W, dh), page(p)) for p in range(PP)]
    in_specs += [pl.BlockSpec((None, W, dh), page(p)) for p in range(PP)]
    in_specs += [pl.BlockSpec((None, 1, W), page(p)) for p in range(PP)]
    grid_spec = pltpu.PrefetchScalarGridSpec(
        num_scalar_prefetch=1,
        grid=(B, n_steps),
        in_specs=in_specs,
        out_specs=[pl.BlockSpec((T, H, dh), lambda b, s, pt: (b, 0, 0)),
                   pl.BlockSpec((None, 1, R), lambda b, s, pt: (b, 0, 0))],
        scratch_shapes=[
            pltpu.VMEM((R, dh), BF16),
            pltpu.VMEM((R, dh), F32),
            pltpu.VMEM((R, 1), F32),
            pltpu.VMEM((R, 1), F32),
            pltpu.VMEM((1, W), F32),
            pltpu.VMEM((1, R), F32),
            pltpu.VMEM((R, W), F32),
        ],
    )
    return pl.pallas_call(
        functools.partial(_fox_sample_kernel, PP=PP, T=T, H=H, dh=dh, n_steps=n_steps),
        grid_spec=grid_spec,
        out_shape=[jax.ShapeDtypeStruct((B * T, H, dh), BF16), jax.ShapeDtypeStruct((B, 1, R), F32)],
        compiler_params=_cparams(("parallel", "arbitrary")),
        name="fox_sample",
    )(page_table, proj3, proj3, proj3, proj3, df_flat, fb_flat,
      *([cache_k] * PP), *([cache_v] * PP), *([cache_f] * PP))


def _pad_lanes(w):
    return jnp.pad(w, ((0, 0), (0, LANES - w.shape[1])))


def _row_tile(m, cap):
    best = None
    for t in range(BF16_ROWS, min(m, cap) + 1, BF16_ROWS):
        if m % t == 0:
            best = t
    assert best is not None, (m, cap)
    return best


def _col_tile(n, cap):
    t = cap
    while n % t:
        t //= 2
    return t


def kernel(x_prompt, x_sample, state_mlstm_C, state_mlstm_n, state_mlstm_m, state_gla_S, state_hgrn_S, cache_fox_k, cache_fox_v, cache_fox_logf, page_table, norm_mix_pre, norm_mix_post, norm_ffn_pre, norm_ffn_post, ffn_w_gate, ffn_w_up, ffn_w_down, w_in_even, w_out_even, mlstm_b_i, mlstm_b_f, mlstm_norm, gla_w_gk2, gla_b_gk, gla_norm, w_in_odd, w_out_odd, hgrn_lb_logits, hgrn_norm, fox_b_f):
    bp, tp, D = x_prompt.shape
    bs, ts, _ = x_sample.shape
    mp, ms = bp * tp, bs * ts
    M = mp + ms
    depth = norm_mix_pre.shape[0]
    a_heads, a_dk, a_dv = state_mlstm_C.shape[2:]
    b_heads, b_dk, b_dv = state_gla_S.shape[2:]
    c_heads, c_dk, c_dv = state_hgrn_S.shape[2:]
    d_heads = fox_b_f.shape[1]
    half = D // 2
    d_dh = half // d_heads
    ff = ffn_w_gate.shape[3]
    n_phys, page = cache_fox_k.shape[1:3]

    tm = _row_tile(M, MM_ROWS_CAP)
    tr = _row_tile(M, NORM_ROWS_CAP)
    wd_bf = [ffn_w_down[0, 0].astype(BF16)]

    def ffn(h, layer, j):
        u = ffn_up(h, ffn_w_gate, ffn_w_up, (layer, j), tm=tm, tn=_col_tile(ff, FFN_COLS))
        nxt = (layer, 1) if j == 0 else (layer + 1, 0)
        ride = (ffn_w_down, nxt, CAST_ROWS) if nxt[0] < depth else None
        res = matmul([(u, wd_bf[-1], (), 0)], out_dtype=BF16, tm=tm, tn=_col_tile(D, PROJ_COLS), nk=2, ride=ride)
        if ride is None:
            return res
        wd_bf.append(res[1])
        return res[0]

    groups = ((bp, tp, 0), (bs, ts, mp))
    outs = {k: ([], []) for k in ('c', 'n', 'm', 'g', 'h', 'k', 'v', 'f')}

    x, h = prenorm_join(x_prompt.reshape(mp, D), x_sample.reshape(ms, D), norm_ffn_pre[0, 0], tr=math.gcd(mp, ms))
    for layer in range(depth):
        y = ffn(h, layer, 0)
        x, h = resid_norm(x, y, norm_ffn_post[layer, 0], norm_mix_pre[layer], scale=0.5, tr=tr)
        if layer % 2 == 0:
            e = layer // 2
            wt = jnp.swapaxes(w_in_even, 1, 2)
            n_a = a_heads * (2 * a_dk + 2 * a_dv)
            n_b = b_heads * (2 * b_dk + 2 * b_dv)
            o_b = n_a + 2 * a_heads
            wt_small = jnp.concatenate([wt[e, n_a:o_b], wt[e, o_b + n_b:]], axis=0)
            wt_small = jnp.pad(wt_small, ((0, LANES - wt_small.shape[0]), (0, 0)))
            proj_a = matmul_nt(h, wt, (e,), n=n_a, out_dtype=F32, tm=tm, tn=_col_tile(n_a, PROJ_COLS))
            proj_b = matmul_nt(h, wt[e, o_b:o_b + n_b], (), n=n_b, out_dtype=F32, tm=tm, tn=_col_tile(n_b, PROJ_COLS))
            small = matmul_nt(h, wt_small, (), n=LANES, out_dtype=F32, tm=tm, tn=LANES)
            gate_bias = _pad_lanes(jnp.concatenate([mlstm_b_i[e], mlstm_b_f[e]])[None, :])
            a_cols = (0, a_heads * a_dk, 2 * a_heads * a_dk, 2 * a_heads * a_dk + a_heads * a_dv)
            w2 = gla_w_gk2[e]
            w2pad = jnp.zeros((LANES, b_heads * b_dk), F32).at[2 * a_heads:2 * a_heads + w2.shape[0]].set(w2)
            w2pad = w2pad.reshape(LANES, b_heads, b_dk).transpose(1, 0, 2).astype(BF16)
            b_cols = (0, b_heads * b_dk, 2 * b_heads * b_dk, 2 * b_heads * b_dk + b_heads * b_dv)
            ha, hb = [], []
            for gi, (B, T, r0) in enumerate(groups):
                if gi == 0:
                    c0, n0, m0, s0 = (jnp.zeros((B,) + s.shape[2:], F32)
                                      for s in (state_mlstm_C, state_mlstm_n, state_mlstm_m, state_gla_S))
                else:
                    c0, n0, m0, s0 = state_mlstm_C[e], state_mlstm_n[e], state_mlstm_m[e], state_gla_S[e]
                CH = min(T, SCAN_CHUNK)
                ha_g, c1, n1, m1 = mlstm_mixer(proj_a, small, gate_bias, mlstm_norm[e], c0, n0, m0,
                                               B=B, T=T, row_off=r0, cols=a_cols,
                                               H=a_heads, dk=a_dk, dv=a_dv, L=CH,
                                               HB=math.gcd(a_heads, MLSTM_HEADS_PER_STEP))
                hb_g, s1 = gla_mixer(proj_b, small, w2pad, gla_b_gk[e], gla_norm[e], s0,
                                     B=B, T=T, row_off=r0, cols=b_cols,
                                     H=b_heads, dk=b_dk, dv=b_dv, CH=CH, SB=min(CH, SCAN_SUB),
                                     HB=math.gcd(b_heads, GLA_HEADS_PER_STEP))
                ha.append(ha_g); hb.append(hb_g)
                outs['c'][gi].append(c1); outs['n'][gi].append(n1)
                outs['m'][gi].append(m1); outs['g'][gi].append(s1)
            wo = w_out_even.astype(BF16)
            mix = matmul([(jnp.concatenate(ha, axis=0), wo, (e,), 0), (jnp.concatenate(hb, axis=0), wo, (e,), 1)],
                         out_dtype=BF16, tm=tm, tn=_col_tile(D, PROJ_COLS))
        else:
            o = layer // 2
            n_main = 4 * c_heads * c_dk + 4 * half
            wt = jnp.swapaxes(w_in_odd, 1, 2)
            wt_small = wt[o, n_main:]
            wt_small = jnp.pad(wt_small, ((0, LANES - wt_small.shape[0]), (0, 0)))
            proj = matmul_nt(h, wt, (o,), n=n_main, out_dtype=F32, tm=tm, tn=_col_tile(n_main, PROJ_COLS))
            small = matmul_nt(h, wt_small, (), n=LANES, out_dtype=F32, tm=tm, tn=LANES)
            cw = c_heads * c_dk
            dq = 4 * cw
            d_cols = (dq, dq + half, dq + 2 * half, dq + 3 * half)
            hc, hd = [], []
            for gi, (B, T, r0) in enumerate(groups):
                s0 = jnp.zeros((B,) + state_hgrn_S.shape[2:], F32) if gi == 0 else state_hgrn_S[o]
                CH = min(T, SCAN_CHUNK)
                hc_g, s1 = hgrn_mixer(proj, hgrn_lb_logits, hgrn_norm[o], s0,
                                      B=B, T=T, row_off=r0, cols=(0, cw, 2 * cw, 3 * cw),
                                      H=c_heads, dk=c_dk, dv=c_dv, CH=CH, SB=min(CH, SCAN_SUB),
                                      HB=math.gcd(c_heads, HGRN_HEADS_PER_STEP), layer=layer)
                df = small[r0:r0 + B * T, :d_heads]
                if gi == 0:
                    lf_t, c_t = fox_cumsum(df.reshape(B, T, d_heads).transpose(0, 2, 1), fox_b_f[o])
                    logf = lf_t.transpose(0, 2, 1)
                    hd_g = fox_prompt(proj, c_t.reshape(B, d_heads, 1, T), B=B, T=T, row_off=r0,
                                      cols=d_cols, H=d_heads, dh=d_dh, TQ=min(T, FOX_Q_BLOCK))
                else:
                    n_cb = proj.shape[1] // d_dh
                    proj3 = proj[r0:r0 + B * T].reshape(B * T, n_cb, d_dh)
                    n_odd = cache_fox_k.shape[0]
                    hd3, lf = fox_sample(
                        proj3, df.reshape(B, 1, T * d_heads), jnp.tile(fox_b_f[o], T)[None, :],
                        cache_fox_k.reshape(n_odd * n_phys, page * d_heads, d_dh),
                        cache_fox_v.reshape(n_odd * n_phys, page * d_heads, d_dh),
                        cache_fox_logf.reshape(n_odd * n_phys, 1, page * d_heads),
                        page_table, B=B, T=T, page_off=o * n_phys,
                        cols=tuple(c // d_dh for c in d_cols), H=d_heads, dh=d_dh, PP=FOX_PAGES_PER_STEP)
                    hd_g = hd3.reshape(B * T, half)
                    logf = lf.reshape(B, T, d_heads)
                hc.append(hc_g); hd.append(hd_g)
                outs['h'][gi].append(s1)
                k_new, v_new = split_heads(proj, rows=B * T, row_off=r0, ck=d_cols[1], cv=d_cols[2],
                                           H=d_heads, dh=d_dh, tr=min(B * T, SPLIT_ROWS))
                outs['k'][gi].append(k_new.reshape(B, T, d_heads, d_dh))
                outs['v'][gi].append(v_new.reshape(B, T, d_heads, d_dh))
                outs['f'][gi].append(logf)
            wo = w_out_odd.astype(BF16)
            mix = matmul([(jnp.concatenate(hc, axis=0), wo, (o,), 0), (jnp.concatenate(hd, axis=0), wo, (o,), 1)],
                         out_dtype=BF16, tm=tm, tn=_col_tile(D, PROJ_COLS))
        x, h = resid_norm(x, mix, norm_mix_post[layer], norm_ffn_pre[layer, 1], scale=1.0, tr=tr)
        y = ffn(h, layer, 1)
        if layer + 1 < depth:
            x, h = resid_norm(x, y, norm_ffn_post[layer, 1], norm_ffn_pre[layer + 1, 0], scale=0.5, tr=tr)
        else:
            y_p, y_s = resid_split(x, y, norm_ffn_post[layer, 1], scale=0.5, rows_a=mp, tr=math.gcd(mp, ms))

    names = ('c', 'n', 'm', 'g', 'h', 'k', 'v', 'f')
    res_p = tuple(jnp.stack(outs[k][0]) for k in names)
    res_s = tuple(jnp.stack(outs[k][1]) for k in names)
    return (y_p.reshape(bp, tp, D), y_s.reshape(bs, ts, D)) + res_p + res_s
```

```python
import functools
import math

import jax
import jax.numpy as jnp
from jax import lax
from jax.experimental import pallas as pl
from jax.experimental.pallas import tpu as pltpu

F32 = jnp.float32
BF16 = jnp.bfloat16
EPS = 1e-6
NEG = -1e30
LANES = 128
BF16_ROWS = 16
VMEM_LIMIT = 56 * 1024 * 1024

B_GATE_NORMALIZER = 16.0

MM_ROWS_CAP = 1408
NORM_ROWS_CAP = 256
FFN_COLS = 256
PROJ_COLS = 512
SCAN_CHUNK = 128
SCAN_SUB = 32
FOX_Q_BLOCK = 256
FOX_PAGES_PER_STEP = 8
MLSTM_HEADS_PER_STEP = 4
GLA_HEADS_PER_STEP = 4
HGRN_HEADS_PER_STEP = 8
SPLIT_ROWS = 256
CAST_ROWS = 128


def _cparams(sem):
    return pltpu.CompilerParams(dimension_semantics=sem, vmem_limit_bytes=VMEM_LIMIT)


def _log_sigmoid(x):
    return jnp.minimum(x, 0.0) - jnp.log1p(jnp.exp(-jnp.abs(x)))


def _sigmoid(x):
    return jax.nn.sigmoid(x)


def _nt(a, b):
    return lax.dot_general(a, b, (((1,), (1,)), ((), ())), preferred_element_type=F32)


def _tn(a, b):
    return lax.dot_general(a, b, (((0,), (0,)), ((), ())), preferred_element_type=F32)


def _nn(a, b):
    return jnp.dot(a, b, preferred_element_type=F32)


def _cumsum_rows(x):
    n = x.shape[0]
    row = lax.broadcasted_iota(jnp.int32, x.shape, 0)
    s = 1
    while s < n:
        x = x + jnp.where(row >= s, pltpu.roll(x, s, 0), 0.0)
        s *= 2
    return x


def _cumsum_lanes(x, step):
    n = x.shape[1]
    lane = lax.broadcasted_iota(jnp.int32, x.shape, 1)
    s = step
    while s < n:
        x = x + jnp.where(lane >= s, pltpu.roll(x, s, 1), 0.0)
        s *= 2
    return x


def _prenorm_join_kernel(xa_ref, xb_ref, g_ref, xo_ref, ho_ref, *, n_a):
    x = jnp.where(pl.program_id(0) < n_a, xa_ref[...], xb_ref[...])
    xo_ref[...] = x
    r = lax.rsqrt(jnp.mean(x * x, axis=-1, keepdims=True) + EPS)
    ho_ref[...] = (x * r * g_ref[...]).astype(ho_ref.dtype)


def prenorm_join(xa, xb, g, *, tr):
    (ma, d), mb = xa.shape, xb.shape[0]
    n_a = ma // tr
    row = pl.BlockSpec((tr, d), lambda i: (i, 0))
    return pl.pallas_call(
        functools.partial(_prenorm_join_kernel, n_a=n_a),
        grid=((ma + mb) // tr,),
        in_specs=[pl.BlockSpec((tr, d), lambda i: (jnp.minimum(i, n_a - 1), 0)),
                  pl.BlockSpec((tr, d), lambda i: (jnp.maximum(i - n_a, 0), 0)),
                  pl.BlockSpec((1, d), lambda i: (0, 0))],
        out_specs=[row, row],
        out_shape=[jax.ShapeDtypeStruct((ma + mb, d), F32), jax.ShapeDtypeStruct((ma + mb, d), BF16)],
        compiler_params=_cparams(("parallel",)),
        name="prenorm_join",
    )(xa, xb, g.reshape(1, d))


def _resid_kernel(x_ref, y_ref, gp_ref, gn_ref, xo_ref, ho_ref, *, scale):
    y = y_ref[...].astype(F32)
    r = lax.rsqrt(jnp.mean(y * y, axis=-1, keepdims=True) + EPS)
    xn = x_ref[...] + scale * (y * r * gp_ref[...])
    xo_ref[...] = xn
    r2 = lax.rsqrt(jnp.mean(xn * xn, axis=-1, keepdims=True) + EPS)
    ho_ref[...] = (xn * r2 * gn_ref[...]).astype(ho_ref.dtype)


def resid_norm(x, y, g_post, g_next, *, scale, tr):
    m, d = x.shape
    row = pl.BlockSpec((tr, d), lambda i: (i, 0))
    vec = pl.BlockSpec((1, d), lambda i: (0, 0))
    return pl.pallas_call(
        functools.partial(_resid_kernel, scale=scale),
        grid=(m // tr,),
        in_specs=[row, row, vec, vec],
        out_specs=[row, row],
        out_shape=[jax.ShapeDtypeStruct((m, d), F32), jax.ShapeDtypeStruct((m, d), BF16)],
        compiler_params=_cparams(("parallel",)),
        name="resid_norm",
    )(x, y, g_post.reshape(1, d), g_next.reshape(1, d))


def _resid_split_kernel(x_ref, y_ref, gp_ref, xa_ref, xb_ref, *, scale, n_a):
    i = pl.program_id(0)
    y = y_ref[...].astype(F32)
    r = lax.rsqrt(jnp.mean(y * y, axis=-1, keepdims=True) + EPS)
    xn = x_ref[...] + scale * (y * r * gp_ref[...])

    @pl.when(i < n_a)
    def _():
        xa_ref[...] = xn

    @pl.when(i >= n_a)
    def _():
        xb_ref[...] = xn


def resid_split(x, y, g_post, *, scale, rows_a, tr):
    m, d = x.shape
    n_a = rows_a // tr
    row = pl.BlockSpec((tr, d), lambda i: (i, 0))
    return pl.pallas_call(
        functools.partial(_resid_split_kernel, scale=scale, n_a=n_a),
        grid=(m // tr,),
        in_specs=[row, row, pl.BlockSpec((1, d), lambda i: (0, 0))],
        out_specs=[pl.BlockSpec((tr, d), lambda i: (jnp.minimum(i, n_a - 1), 0)),
                   pl.BlockSpec((tr, d), lambda i: (jnp.maximum(i - n_a, 0), 0))],
        out_shape=[jax.ShapeDtypeStruct((rows_a, d), F32), jax.ShapeDtypeStruct((m - rows_a, d), F32)],
        compiler_params=_cparams(("arbitrary",)),
        name="resid_split",
    )(x, y, g_post.reshape(1, d))


def _split_heads_kernel(k_ref, v_ref, ko_ref, vo_ref, *, H, dh):
    for h in range(H):
        ko_ref[:, h, :] = k_ref[:, h * dh:(h + 1) * dh]
        vo_ref[:, h, :] = v_ref[:, h * dh:(h + 1) * dh]


def split_heads(proj, *, rows, row_off, ck, cv, H, dh, tr):
    w = H * dh
    out = jax.ShapeDtypeStruct((rows, H, dh), proj.dtype)
    ospec = pl.BlockSpec((tr, H, dh), lambda i: (i, 0, 0))
    return pl.pallas_call(
        functools.partial(_split_heads_kernel, H=H, dh=dh),
        grid=(rows // tr,),
        in_specs=[pl.BlockSpec((tr, w), lambda i: (row_off // tr + i, ck // w)),
                  pl.BlockSpec((tr, w), lambda i: (row_off // tr + i, cv // w))],
        out_specs=[ospec, ospec],
        out_shape=[out, out],
        compiler_params=_cparams(("parallel",)),
        name="split_heads",
    )(proj, proj)


def _snake(j, i, n_i):
    return jnp.where(j % 2 == 0, i, n_i - 1 - i)


def _mm_kernel(*refs, n_pairs, nk, n_ride):
    refs = list(refs)
    pair_refs = [refs.pop(0) for _ in range(2 * n_pairs)]
    ride_in = refs.pop(0) if n_ride else None
    o_ref = refs.pop(0)
    ride_out = refs.pop(0) if n_ride else None
    if n_ride:
        step = (pl.program_id(0) * pl.num_programs(1) + pl.program_id(1)) * nk + pl.program_id(2)

        @pl.when(step < n_ride)
        def _():
            ride_out[...] = ride_in[...].astype(ride_out.dtype)

    acc = _nn(pair_refs[0][...], pair_refs[1][...])
    for p in range(1, n_pairs):
        acc = acc + _nn(pair_refs[2 * p][...], pair_refs[2 * p + 1][...])
    if nk == 1:
        o_ref[...] = acc.astype(o_ref.dtype)
    else:
        acc_ref = refs.pop(0)
        k = pl.program_id(2)

        @pl.when(k == 0)
        def _():
            acc_ref[...] = acc

        @pl.when((k > 0) & (k < nk - 1))
        def _():
            acc_ref[...] += acc

        @pl.when(k == nk - 1)
        def _():
            o_ref[...] = (acc_ref[...] + acc).astype(o_ref.dtype)


def matmul(pairs, *, out_dtype, tm, tn, nk=1, ride=None):
    m = pairs[0][0].shape[0]
    n = pairs[0][1].shape[-1]
    grid = (m // tm, n // tn, nk)

    def jblk(i, j):
        return _snake(i, j, grid[1])

    def kblk(i, j, k):
        return _snake(i * grid[1] + j, k, nk)

    in_specs, args = [], []
    for x, w, lead, kb0 in pairs:
        tk = x.shape[1] // nk
        in_specs.append(pl.BlockSpec((tm, tk), lambda i, j, k: (i, kblk(i, j, k))))
        in_specs.append(pl.BlockSpec(
            (None,) * len(lead) + (tk, tn),
            lambda i, j, k, lead=lead, kb0=kb0: lead + (kb0 * nk + kblk(i, j, k), jblk(i, j))))
        args += [x, w]
    out_specs = [pl.BlockSpec((tm, tn), lambda i, j, k: (i, jblk(i, j)))]
    out_shape = [jax.ShapeDtypeStruct((m, n), out_dtype)]
    n_ride = 0
    if ride is not None:
        a, lead, rows = ride
        r, c = a.shape[-2:]
        n_ride = r // rows
        assert n_ride * rows == r and n_ride <= grid[0] * grid[1] * grid[2]

        def blk(i, j, k):
            return jnp.minimum((i * grid[1] + j) * nk + k, n_ride - 1)

        in_specs.append(pl.BlockSpec((None,) * len(lead) + (rows, c), lambda i, j, k: lead + (blk(i, j, k), 0)))
        args.append(a)
        out_specs.append(pl.BlockSpec((rows, c), lambda i, j, k: (blk(i, j, k), 0)))
        out_shape.append(jax.ShapeDtypeStruct((r, c), BF16))
    res = pl.pallas_call(
        functools.partial(_mm_kernel, n_pairs=len(pairs), nk=nk, n_ride=n_ride),
        grid=grid,
        in_specs=in_specs,
        out_specs=out_specs,
        out_shape=out_shape,
        scratch_shapes=[pltpu.VMEM((tm, tn), F32)] if nk > 1 else [],
        compiler_params=_cparams(("arbitrary", "arbitrary", "arbitrary") if n_ride else
                                 ("parallel", "parallel", "arbitrary")),
        name="matmul",
    )(*args)
    return res if n_ride else res[0]


def _mm_nt_kernel(x_ref, wt_ref, o_ref, w_bf):
    @pl.when(pl.program_id(1) == 0)
    def _():
        w_bf[...] = wt_ref[...].astype(BF16)

    o_ref[...] = _nt(x_ref[...], w_bf[...]).astype(o_ref.dtype)


def matmul_nt(x, wt, lead, *, n, out_dtype, tm, tn):
    m, k = x.shape
    n_i = m // tm
    return pl.pallas_call(
        _mm_nt_kernel,
        grid=(n // tn, n_i),
        in_specs=[pl.BlockSpec((tm, k), lambda j, i: (_snake(j, i, n_i), 0)),
                  pl.BlockSpec((None,) * len(lead) + (tn, k), lambda j, i: lead + (j, 0))],
        out_specs=pl.BlockSpec((tm, tn), lambda j, i: (_snake(j, i, n_i), j)),
        out_shape=jax.ShapeDtypeStruct((m, n), out_dtype),
        scratch_shapes=[pltpu.VMEM((tn, k), BF16)],
        compiler_params=_cparams(("parallel", "arbitrary")),
        name="matmul_nt",
    )(x, wt)


def _ffn_up_kernel(h_ref, wg_ref, wu_ref, o_ref, wg_bf, wu_bf):
    @pl.when(pl.program_id(1) == 0)
    def _():
        wg_bf[...] = wg_ref[...].astype(BF16)
        wu_bf[...] = wu_ref[...].astype(BF16)

    h = h_ref[...]
    g = _nn(h, wg_bf[...])
    u = _nn(h, wu_bf[...])
    o_ref[...] = (g * _sigmoid(g) * u).astype(o_ref.dtype)


def ffn_up(h, wg, wu, lead, *, tm, tn):
    m, k = h.shape
    n = wg.shape[-1]
    n_i = m // tm
    wspec = pl.BlockSpec((None,) * len(lead) + (k, tn), lambda j, i: lead + (0, j))
    return pl.pallas_call(
        _ffn_up_kernel,
        grid=(n // tn, n_i),
        in_specs=[pl.BlockSpec((tm, k), lambda j, i: (_snake(j, i, n_i), 0)), wspec, wspec],
        out_specs=pl.BlockSpec((tm, tn), lambda j, i: (_snake(j, i, n_i), j)),
        out_shape=jax.ShapeDtypeStruct((m, n), BF16),
        scratch_shapes=[pltpu.VMEM((k, tn), BF16), pltpu.VMEM((k, tn), BF16)],
        compiler_params=_cparams(("parallel", "arbitrary")),
        name="ffn_up",
    )(h, wg, wu)


def _mlstm_kernel(q_ref, k_ref, v_ref, og_ref, gt_ref, gb_ref, nrm_ref, c0_ref, n0_ref, m0_ref,
                  h_ref, c_ref, n_ref, m_ref, *, L, HB, dk, dv, n_heads):
    @pl.when(pl.program_id(2) == 0)
    def _():
        c_ref[...] = c0_ref[...]
        n_ref[...] = n0_ref[...]
        m_ref[...] = m0_ref[...]

    gates = gt_ref[...] + gb_ref[...]
    lane = lax.broadcasted_iota(jnp.int32, gates.shape, 1)
    r = lax.broadcasted_iota(jnp.int32, (L, L), 0)
    s = lax.broadcasted_iota(jnp.int32, (L, L), 1)
    tril = r >= s
    eye = r == s

    def col2row(col):
        return jnp.sum(jnp.where(eye, col, 0.0), axis=0, keepdims=True)

    for hb in range(HB):
        hh = pl.program_id(1) * HB + hb
        ks = slice(hb * dk, (hb + 1) * dk)
        vs = slice(hb * dv, (hb + 1) * dv)
        i_col = jnp.sum(jnp.where(lane == hh, gates, 0.0), axis=1, keepdims=True)
        f_col = jnp.sum(jnp.where(lane == hh + n_heads, gates, 0.0), axis=1, keepdims=True)
        logf = _log_sigmoid(f_col)
        logf_row = col2row(logf)
        b_col = jnp.sum(jnp.where(tril, logf_row, 0.0), axis=1, keepdims=True)
        b_row = col2row(b_col)
        i_row = col2row(i_col)
        m_prev = m_ref[hb]
        dmat = jnp.where(tril, b_col - b_row + i_row, -jnp.inf)
        from_state = b_col + m_prev
        m_t = jnp.maximum(from_state, jnp.max(dmat, axis=1, keepdims=True))

        q = q_ref[:, ks]
        k = k_ref[:, ks] * (dk ** -0.5)
        qb = q.astype(BF16)
        vb = v_ref[:, vs].astype(BF16)
        w = jnp.exp(dmat - m_t) * _nt(qb, k.astype(BF16))
        s_state = jnp.exp(from_state - m_t)
        c_old = c_ref[hb]
        n_old = n_ref[hb]
        num = s_state * _nn(qb, c_old.astype(BF16)) + _nn(w.astype(BF16), vb)
        den = s_state * jnp.sum(q * n_old, axis=1, keepdims=True) + jnp.sum(w, axis=1, keepdims=True)
        h = num / jnp.maximum(jnp.abs(den), jnp.exp(-m_t))

        m_new = m_t[L - 1:L, :]
        b_last = b_col[L - 1:L, :]
        w_end = jnp.exp(b_last - b_col + i_col - m_new)
        s_end = jnp.exp(b_last + m_prev - m_new)
        kw = k * w_end
        c_ref[hb] = s_end * c_old + _tn(kw.astype(BF16), vb)
        n_ref[hb] = s_end * n_old + jnp.sum(kw, axis=0, keepdims=True)
        m_ref[hb] = m_new

        hn = h * lax.rsqrt(jnp.mean(h * h, axis=1, keepdims=True) + EPS) * nrm_ref[hb]
        h_ref[:, vs] = (hn * _sigmoid(og_ref[:, vs])).astype(h_ref.dtype)


def mlstm_mixer(proj, small, gate_bias, norm, c0, n0, m0, *, B, T, row_off, cols, H, dk, dv, L, HB):
    nC = T // L
    rb = row_off // L
    cq, ck, cv, co = cols
    wk, wv = HB * dk, HB * dv

    def rows(b, h, c):
        return rb + b * nC + c

    in_specs = [
        pl.BlockSpec((L, wk), lambda b, h, c: (rows(b, h, c), cq // wk + h)),
        pl.BlockSpec((L, wk), lambda b, h, c: (rows(b, h, c), ck // wk + h)),
        pl.BlockSpec((L, wv), lambda b, h, c: (rows(b, h, c), cv // wv + h)),
        pl.BlockSpec((L, wv), lambda b, h, c: (rows(b, h, c), co // wv + h)),
        pl.BlockSpec((L, LANES), lambda b, h, c: (rows(b, h, c), 0)),
        pl.BlockSpec((1, LANES), lambda b, h, c: (0, 0)),
        pl.BlockSpec((HB, 1, dv), lambda b, h, c: (h, 0, 0)),
        pl.BlockSpec((None, HB, dk, dv), lambda b, h, c: (b, h, 0, 0)),
        pl.BlockSpec((None, HB, 1, dk), lambda b, h, c: (b, h, 0, 0)),
        pl.BlockSpec((None, HB, 1, 1), lambda b, h, c: (b, h, 0, 0)),
    ]
    out_specs = [
        pl.BlockSpec((L, wv), lambda b, h, c: (b * nC + c, h)),
        pl.BlockSpec((None, HB, dk, dv), lambda b, h, c: (b, h, 0, 0)),
        pl.BlockSpec((None, HB, 1, dk), lambda b, h, c: (b, h, 0, 0)),
        pl.BlockSpec((None, HB, 1, 1), lambda b, h, c: (b, h, 0, 0)),
    ]
    out_shape = [
        jax.ShapeDtypeStruct((B * T, H * dv), BF16),
        jax.ShapeDtypeStruct((B, H, dk, dv), F32),
        jax.ShapeDtypeStruct((B, H, 1, dk), F32),
        jax.ShapeDtypeStruct((B, H, 1, 1), F32),
    ]
    h, c1, n1, m1 = pl.pallas_call(
        functools.partial(_mlstm_kernel, L=L, HB=HB, dk=dk, dv=dv, n_heads=H),
        grid=(B, H // HB, nC),
        in_specs=in_specs, out_specs=out_specs, out_shape=out_shape,
        compiler_params=_cparams(("parallel", "parallel", "arbitrary")),
        name="mlstm",
    )(proj, proj, proj, proj, small, gate_bias, norm.reshape(H, 1, dv), c0,
      n0.reshape(B, H, 1, dk), m0.reshape(B, H, 1, 1))
    return h, c1, n1.reshape(B, H, dk), m1.reshape(B, H)


def _scan_chunk(q, k, v, g, st, *, SB):
    CH = q.shape[0]
    bc = _cumsum_rows(g)
    b_last = bc[CH - 1:CH, :]
    vb = v.astype(BF16)
    o = _nt((q * jnp.exp(bc)).astype(BF16), st.astype(BF16))
    row = lax.broadcasted_iota(jnp.int32, bc.shape, 0)
    srow = lax.broadcasted_iota(jnp.int32, (SB, CH), 0)
    scol = lax.broadcasted_iota(jnp.int32, (SB, CH), 1)
    blocks = []
    for i in range(CH // SB):
        lo, hi = i * SB, (i + 1) * SB
        ref = bc[lo - 1:lo, :] if i > 0 else jnp.zeros_like(b_last)
        qi = q[lo:hi] * jnp.exp(bc[lo:hi] - ref)
        ki = k * jnp.exp(jnp.where(row < hi, ref - bc, 0.0))
        a = _nt(qi.astype(BF16), ki.astype(BF16))
        blocks.append(jnp.where(scol <= srow + lo, a, 0.0))
    a_full = blocks[0] if len(blocks) == 1 else jnp.concatenate(blocks, axis=0)
    o = o + _nn(a_full.astype(BF16), vb)
    k_dec = k * jnp.exp(b_last - bc)
    st_new = st * jnp.exp(b_last) + _tn(vb, k_dec.astype(BF16))
    return o, st_new


def _head_out(o, nrm, gate_act):
    return o * lax.rsqrt(jnp.mean(o * o, axis=1, keepdims=True) + EPS) * nrm * gate_act


def _gla_kernel(q_ref, k_ref, v_ref, og_ref, gt_ref, w2_ref, bgk_ref, nrm_ref, s0_ref,
                o_ref, s_ref, st_scr, *, SB, HB, dk, dv, n_chunks):
    c = pl.program_id(2)

    @pl.when(c == 0)
    def _():
        for hb in range(HB):
            st_scr[hb] = s0_ref[hb].T

    gates = gt_ref[...].astype(BF16)
    for hb in range(HB):
        ks = slice(hb * dk, (hb + 1) * dk)
        vs = slice(hb * dv, (hb + 1) * dv)
        gk = _nn(gates, w2_ref[hb]) + bgk_ref[hb]
        g = _log_sigmoid(gk) / B_GATE_NORMALIZER
        o, st_new = _scan_chunk(q_ref[:, ks] * (dk ** -0.5), k_ref[:, ks], v_ref[:, vs], g, st_scr[hb], SB=SB)
        st_scr[hb] = st_new
        og = og_ref[:, vs]
        o_ref[:, vs] = _head_out(o, nrm_ref[hb], og * _sigmoid(og)).astype(o_ref.dtype)

    @pl.when(c == n_chunks - 1)
    def _():
        for hb in range(HB):
            s_ref[hb] = st_scr[hb].T


def gla_mixer(proj, small, w2pad, b_gk, norm, s0, *, B, T, row_off, cols, H, dk, dv, CH, SB, HB):
    nC = T // CH
    rb = row_off // CH
    cq, ck, cv, co = cols
    wk, wv = HB * dk, HB * dv

    def rows(b, h, c):
        return rb + b * nC + c

    in_specs = [
        pl.BlockSpec((CH, wk), lambda b, h, c: (rows(b, h, c), cq // wk + h)),
        pl.BlockSpec((CH, wk), lambda b, h, c: (rows(b, h, c), ck // wk + h)),
        pl.BlockSpec((CH, wv), lambda b, h, c: (rows(b, h, c), cv // wv + h)),
        pl.BlockSpec((CH, wv), lambda b, h, c: (rows(b, h, c), co // wv + h)),
        pl.BlockSpec((CH, LANES), lambda b, h, c: (rows(b, h, c), 0)),
        pl.BlockSpec((HB, LANES, dk), lambda b, h, c: (h, 0, 0)),
        pl.BlockSpec((HB, 1, dk), lambda b, h, c: (h, 0, 0)),
        pl.BlockSpec((HB, 1, dv), lambda b, h, c: (h, 0, 0)),
        pl.BlockSpec((None, HB, dk, dv), lambda b, h, c: (b, h, 0, 0)),
    ]
    out_specs = [
        pl.BlockSpec((CH, wv), lambda b, h, c: (b * nC + c, h)),
        pl.BlockSpec((None, HB, dk, dv), lambda b, h, c: (b, h, 0, 0)),
    ]
    out_shape = [jax.ShapeDtypeStruct((B * T, H * dv), BF16), jax.ShapeDtypeStruct((B, H, dk, dv), F32)]
    return pl.pallas_call(
        functools.partial(_gla_kernel, SB=SB, HB=HB, dk=dk, dv=dv, n_chunks=nC),
        grid=(B, H // HB, nC),
        in_specs=in_specs, out_specs=out_specs, out_shape=out_shape,
        scratch_shapes=[pltpu.VMEM((HB, dv, dk), F32)],
        compiler_params=_cparams(("parallel", "parallel", "arbitrary")),
        name="gla",
    )(proj, proj, proj, proj, small, w2pad, b_gk.reshape(H, 1, dk), norm.reshape(H, 1, dv), s0)


def _hgrn_kernel(q_ref, f_ref, i_ref, og_ref, lg_ref, nrm_ref, s0_ref, o_ref, s_ref, st_scr,
                 *, SB, HB, dk, layer, n_chunks):
    c = pl.program_id(2)
    lg = lg_ref[...]
    e = jnp.exp(lg - jnp.max(lg, axis=0, keepdims=True))
    sm = e / jnp.sum(e, axis=0, keepdims=True)
    lb_all = jnp.sum(sm[1:layer + 1], axis=0, keepdims=True) if layer > 0 else jnp.zeros_like(sm[0:1])
    @pl.when(c == 0)
    def _():
        for hb in range(HB):
            st_scr[hb] = s0_ref[hb].T

    for hb in range(HB):
        sl = slice(hb * dk, (hb + 1) * dk)
        lb = lb_all[:, sl]
        z = f_ref[:, sl]
        g = jnp.log(lb + (1.0 - lb) * _sigmoid(z))
        key = (1.0 - lb) * _sigmoid(-z)
        cq = q_ref[:, sl]
        o, st_new = _scan_chunk(cq * _sigmoid(cq), key, i_ref[:, sl], g, st_scr[hb], SB=SB)
        st_scr[hb] = st_new
        og = og_ref[:, sl]
        o_ref[:, sl] = _head_out(o, nrm_ref[hb], og * _sigmoid(og)).astype(o_ref.dtype)

    @pl.when(c == n_chunks - 1)
    def _():
        for hb in range(HB):
            s_ref[hb] = st_scr[hb].T


def hgrn_mixer(proj, lb_logits, norm, s0, *, B, T, row_off, cols, H, dk, dv, CH, SB, HB, layer):
    assert dk == dv
    nC = T // CH
    rb = row_off // CH
    cq, cf, ci, co = cols
    depth = lb_logits.shape[0]
    wb = HB * dk

    def col(c0):
        return pl.BlockSpec((CH, wb), lambda b, h, c: (rb + b * nC + c, c0 // wb + h))

    in_specs = [
        col(cq), col(cf), col(ci), col(co),
        pl.BlockSpec((depth, wb), lambda b, h, c: (0, h)),
        pl.BlockSpec((HB, 1, dv), lambda b, h, c: (h, 0, 0)),
        pl.BlockSpec((None, HB, dk, dv), lambda b, h, c: (b, h, 0, 0)),
    ]
    out_specs = [
        pl.BlockSpec((CH, wb), lambda b, h, c: (b * nC + c, h)),
        pl.BlockSpec((None, HB, dk, dv), lambda b, h, c: (b, h, 0, 0)),
    ]
    out_shape = [jax.ShapeDtypeStruct((B * T, H * dv), BF16), jax.ShapeDtypeStruct((B, H, dk, dv), F32)]
    return pl.pallas_call(
        functools.partial(_hgrn_kernel, SB=SB, HB=HB, dk=dk, layer=layer, n_chunks=nC),
        grid=(B, H // HB, nC),
        in_specs=in_specs, out_specs=out_specs, out_shape=out_shape,
        scratch_shapes=[pltpu.VMEM((HB, dv, dk), F32)],
        compiler_params=_cparams(("parallel", "parallel", "arbitrary")),
        name="hgrn",
    )(proj, proj, proj, proj, lb_logits, norm.reshape(H, 1, dv), s0)


def _fox_cumsum_kernel(x_ref, b_ref, lf_ref, c_ref):
    lf = _log_sigmoid(x_ref[...] + b_ref[...])
    lf_ref[...] = lf
    c_ref[...] = _cumsum_lanes(lf, 1)


def fox_cumsum(df_t, bias):
    B, H, T = df_t.shape
    spec = pl.BlockSpec((None, H, T), lambda b: (b, 0, 0))
    return pl.pallas_call(
        _fox_cumsum_kernel,
        grid=(B,),
        in_specs=[spec, pl.BlockSpec((H, 1), lambda b: (0, 0))],
        out_specs=[spec, spec],
        out_shape=[jax.ShapeDtypeStruct((B, H, T), F32)] * 2,
        compiler_params=_cparams(("parallel",)),
        name="fox_cumsum",
    )(df_t, bias.reshape(H, 1))


def _fox_prompt_kernel(q_ref, k_ref, v_ref, og_ref, cr_ref, o_ref, *, TQ, T, dh):
    kb = k_ref[...].astype(BF16)
    vb = v_ref[...].astype(BF16)
    c_row = cr_ref[...]
    for i in range(T // TQ):
        lo, hi = i * TQ, (i + 1) * TQ
        q = (q_ref[lo:hi, :] * (dh ** -0.5)).astype(BF16)
        s = _nt(q, kb[:hi]) - c_row[:, :hi]
        qpos = lo + lax.broadcasted_iota(jnp.int32, (TQ, hi), 0)
        kpos = lax.broadcasted_iota(jnp.int32, (TQ, hi), 1)
        s = jnp.where(kpos <= qpos, s, NEG)
        p = jnp.exp(s - jnp.max(s, axis=1, keepdims=True))
        l = jnp.sum(p, axis=1, keepdims=True)
        o = _nn(p.astype(BF16), vb[:hi]) / l
        o_ref[lo:hi, :] = (o * _sigmoid(og_ref[lo:hi, :])).astype(o_ref.dtype)


def fox_prompt(proj, c_row, *, B, T, row_off, cols, H, dh, TQ):
    cq, ck, cv, co = cols
    rb = row_off // T

    def col(c0):
        return pl.BlockSpec((T, dh), lambda b, h: (rb + b, c0 // dh + h))

    in_specs = [col(cq), col(ck), col(cv), col(co),
                pl.BlockSpec((None, None, 1, T), lambda b, h: (b, h, 0, 0))]
    return pl.pallas_call(
        functools.partial(_fox_prompt_kernel, TQ=TQ, T=T, dh=dh),
        grid=(B, H),
        in_specs=in_specs,
        out_specs=pl.BlockSpec((T, dh), lambda b, h: (b, h)),
        out_shape=jax.ShapeDtypeStruct((B * T, H * dh), BF16),
        compiler_params=_cparams(("parallel", "parallel")),
        name="fox_prompt",
    )(proj, proj, proj, proj, c_row)


def _fox_sample_kernel(pt_ref, q_ref, kn_ref, vn_ref, og_ref, df_ref, fb_ref, *rest, PP, T, H, dh, n_steps):
    k_refs = rest[0:PP]
    v_refs = rest[PP:2 * PP]
    f_refs = rest[2 * PP:3 * PP]
    o_ref, lf_ref = rest[3 * PP], rest[3 * PP + 1]
    qall, acc, m_scr, l_scr, carry, crow, hmask = rest[3 * PP + 2:]
    s_id = pl.program_id(1)
    R = T * H
    W = k_refs[0].shape[0]

    @pl.when(s_id == 0)
    def _():
        qall[...] = (q_ref[...].reshape(R, dh) * (dh ** -0.5)).astype(BF16)
        acc[...] = jnp.zeros_like(acc)
        m_scr[...] = jnp.full_like(m_scr, NEG)
        l_scr[...] = jnp.zeros_like(l_scr)
        carry[...] = jnp.zeros_like(carry)
        lf_new = _log_sigmoid(df_ref[...] + fb_ref[...])
        lf_ref[...] = lf_new
        crow[...] = _cumsum_lanes(lf_new, H)
        rr = lax.broadcasted_iota(jnp.int32, (R, W), 0)
        ll = lax.broadcasted_iota(jnp.int32, (R, W), 1)
        hmask[...] = jnp.where((rr % H) == (ll % H), 0.0, NEG)

    lf = jnp.concatenate([f_refs[p][...] for p in range(PP)], axis=0)
    lane = lax.broadcasted_iota(jnp.int32, (PP, W), 1)
    suf = lf
    tot = lf
    s = H
    while s < W:
        suf = suf + jnp.where(lane + s < W, pltpu.roll(suf, W - s, 1), 0.0)
        tot = tot + pltpu.roll(tot, s, 1)
        s *= 2
    tot_cum = _cumsum_rows(tot)
    later = carry[...]
    d_all = suf - lf + (tot_cum - tot) + later
    carry[...] = later + tot_cum[PP - 1:PP, :]

    qa = qall[...]
    hm = hmask[...]
    m_pg, l_pg, a_pg = [], [], []
    for p in range(PP):
        kb = k_refs[p][...].astype(BF16)
        vb = v_refs[p][...].astype(BF16)
        st = _nt(qa, kb) + (hm + d_all[p:p + 1, :])
        m_p = jnp.max(st, axis=1, keepdims=True)
        pm = jnp.exp(st - m_p)
        m_pg.append(m_p)
        l_pg.append(jnp.sum(pm, axis=1, keepdims=True))
        a_pg.append(_nn(pm.astype(BF16), vb))
    m_prev = m_scr[...]
    m_old = m_prev
    for m_p in m_pg:
        m_old = jnp.maximum(m_old, m_p)
    alpha = jnp.exp(m_prev - m_old)
    l_old = alpha * l_scr[...]
    a_old = alpha * acc[...]
    for m_p, l_p, a_p in zip(m_pg, l_pg, a_pg):
        w_p = jnp.exp(m_p - m_old)
        l_old = l_old + w_p * l_p
        a_old = a_old + w_p * a_p
    m_scr[...] = m_old
    l_scr[...] = l_old
    acc[...] = a_old

    @pl.when(s_id == n_steps - 1)
    def _():
        knb = kn_ref[...].reshape(R, dh).astype(BF16)
        vnb = vn_ref[...].reshape(R, dh).astype(BF16)
        r2 = lax.broadcasted_iota(jnp.int32, (R, R), 0)
        l2 = lax.broadcasted_iota(jnp.int32, (R, R), 1)
        ok = ((r2 % H) == (l2 % H)) & (l2 // H <= r2 // H)
        sn = jnp.where(ok, _nt(qa, knb) - crow[...], NEG)
        m_fin = jnp.maximum(m_old, jnp.max(sn, axis=1, keepdims=True))
        a2 = jnp.exp(m_old - m_fin)
        pn = jnp.exp(sn - m_fin)
        l_fin = a2 * l_old + jnp.sum(pn, axis=1, keepdims=True)
        out = (a2 * a_old + _nn(pn.astype(BF16), vnb)) / l_fin
        out = out * _sigmoid(og_ref[...].reshape(R, dh))
        o_ref[...] = out.reshape(T, H, dh).astype(o_ref.dtype)


def fox_sample(proj3, df_flat, fb_flat, cache_k, cache_v, cache_f, page_table, *, B, T, page_off, cols, H, dh, PP):
    R = T * H
    W = cache_k.shape[1]
    n_pages = page_table.shape[1]
    n_steps = n_pages // PP
    cq, ck, cv, co = cols

    def rowspec(c0):
        return pl.BlockSpec((T, H, dh), lambda b, s, pt: (b, c0 // H, 0))

    def page(p):
        return lambda b, s, pt: (page_off + pt[b, n_pages - 1 - (s * PP + p)], 0, 0)

    in_specs = [rowspec(cq), rowspec(ck), rowspec(cv), rowspec(co),
                pl.BlockSpec((None, 1, R), lambda b, s, pt: (b, 0, 0)),
                pl.BlockSpec((1, R), lambda b, s, pt: (0, 0))]
    in_specs += [pl.BlockSpec((None, W, dh), page(p)) for p in range(PP)]
    in_specs += [pl.BlockSpec((None, W, dh), page(p)) for p in range(PP)]
    in_specs += [pl.BlockSpec((None, 1, W), page(p)) for p in range(PP)]
    grid_spec = pltpu.PrefetchScalarGridSpec(
        num_scalar_prefetch=1,
        grid=(B, n_steps),
        in_specs=in_specs,
        out_specs=[pl.BlockSpec((T, H, dh), lambda b, s, pt: (b, 0, 0)),
                   pl.BlockSpec((None, 1, R), lambda b, s, pt: (b, 0, 0))],
        scratch_shapes=[
            pltpu.VMEM((R, dh), BF16),
            pltpu.VMEM((R, dh), F32),
            pltpu.VMEM((R, 1), F32),
            pltpu.VMEM((R, 1), F32),
            pltpu.VMEM((1, W), F32),
            pltpu.VMEM((1, R), F32),
            pltpu.VMEM((R, W), F32),
        ],
    )
    return pl.pallas_call(
        functools.partial(_fox_sample_kernel, PP=PP, T=T, H=H, dh=dh, n_steps=n_steps),
        grid_spec=grid_spec,
        out_shape=[jax.ShapeDtypeStruct((B * T, H, dh), BF16), jax.ShapeDtypeStruct((B, 1, R), F32)],
        compiler_params=_cparams(("parallel", "arbitrary")),
        name="fox_sample",
    )(page_table, proj3, proj3, proj3, proj3, df_flat, fb_flat,
      *([cache_k] * PP), *([cache_v] * PP), *([cache_f] * PP))


def _pad_lanes(w):
    return jnp.pad(w, ((0, 0), (0, LANES - w.shape[1])))


def _row_tile(m, cap):
    best = None
    for t in range(BF16_ROWS, min(m, cap) + 1, BF16_ROWS):
        if m % t == 0:
            best = t
    assert best is not None, (m, cap)
    return best


def _col_tile(n, cap):
    t = cap
    while n % t:
        t //= 2
    return t


def kernel(x_prompt, x_sample, state_mlstm_C, state_mlstm_n, state_mlstm_m, state_gla_S, state_hgrn_S, cache_fox_k, cache_fox_v, cache_fox_logf, page_table, norm_mix_pre, norm_mix_post, norm_ffn_pre, norm_ffn_post, ffn_w_gate, ffn_w_up, ffn_w_down, w_in_even, w_out_even, mlstm_b_i, mlstm_b_f, mlstm_norm, gla_w_gk2, gla_b_gk, gla_norm, w_in_odd, w_out_odd, hgrn_lb_logits, hgrn_norm, fox_b_f):
    bp, tp, D = x_prompt.shape
    bs, ts, _ = x_sample.shape
    mp, ms = bp * tp, bs * ts
    M = mp + ms
    depth = norm_mix_pre.shape[0]
    a_heads, a_dk, a_dv = state_mlstm_C.shape[2:]
    b_heads, b_dk, b_dv = state_gla_S.shape[2:]
    c_heads, c_dk, c_dv = state_hgrn_S.shape[2:]
    d_heads = fox_b_f.shape[1]
    half = D // 2
    d_dh = half // d_heads
    ff = ffn_w_gate.shape[3]
    n_phys, page = cache_fox_k.shape[1:3]

    tm = _row_tile(M, MM_ROWS_CAP)
    tr = _row_tile(M, NORM_ROWS_CAP)
    wd_bf = [ffn_w_down[0, 0].astype(BF16)]

    def ffn(h, layer, j):
        u = ffn_up(h, ffn_w_gate, ffn_w_up, (layer, j), tm=tm, tn=_col_tile(ff, FFN_COLS))
        nxt = (layer, 1) if j == 0 else (layer + 1, 0)
        ride = (ffn_w_down, nxt, CAST_ROWS) if nxt[0] < depth else None
        res = matmul([(u, wd_bf[-1], (), 0)], out_dtype=BF16, tm=tm, tn=_col_tile(D, PROJ_COLS), nk=2, ride=ride)
        if ride is None:
            return res
        wd_bf.append(res[1])
        return res[0]

    groups = ((bp, tp, 0), (bs, ts, mp))
    outs = {k: ([], []) for k in ('c', 'n', 'm', 'g', 'h', 'k', 'v', 'f')}

    x, h = prenorm_join(x_prompt.reshape(mp, D), x_sample.reshape(ms, D), norm_ffn_pre[0, 0], tr=math.gcd(mp, ms))
    for layer in range(depth):
        y = ffn(h, layer, 0)
        x, h = resid_norm(x, y, norm_ffn_post[layer, 0], norm_mix_pre[layer], scale=0.5, tr=tr)
        if layer % 2 == 0:
            e = layer // 2
            wt = jnp.swapaxes(w_in_even, 1, 2)
            n_a = a_heads * (2 * a_dk + 2 * a_dv)
            n_b = b_heads * (2 * b_dk + 2 * b_dv)
            o_b = n_a + 2 * a_heads
            wt_small = jnp.concatenate([wt[e, n_a:o_b], wt[e, o_b + n_b:]], axis=0)
            wt_small = jnp.pad(wt_small, ((0, LANES - wt_small.shape[0]), (0, 0)))
            proj_a = matmul_nt(h, wt, (e,), n=n_a, out_dtype=F32, tm=tm, tn=_col_tile(n_a, PROJ_COLS))
            proj_b = matmul_nt(h, wt[e, o_b:o_b + n_b], (), n=n_b, out_dtype=F32, tm=tm, tn=_col_tile(n_b, PROJ_COLS))
            small = matmul_nt(h, wt_small, (), n=LANES, out_dtype=F32, tm=tm, tn=LANES)
            gate_bias = _pad_lanes(jnp.concatenate([mlstm_b_i[e], mlstm_b_f[e]])[None, :])
            a_cols = (0, a_heads * a_dk, 2 * a_heads * a_dk, 2 * a_heads * a_dk + a_heads * a_dv)
            w2 = gla_w_gk2[e]
            w2pad = jnp.zeros((LANES, b_heads * b_dk), F32).at[2 * a_heads:2 * a_heads + w2.shape[0]].set(w2)
            w2pad = w2pad.reshape(LANES, b_heads, b_dk).transpose(1, 0, 2).astype(BF16)
            b_cols = (0, b_heads * b_dk, 2 * b_heads * b_dk, 2 * b_heads * b_dk + b_heads * b_dv)
            ha, hb = [], []
            for gi, (B, T, r0) in enumerate(groups):
                if gi == 0:
                    c0, n0, m0, s0 = (jnp.zeros((B,) + s.shape[2:], F32)
                                      for s in (state_mlstm_C, state_mlstm_n, state_mlstm_m, state_gla_S))
                else:
                    c0, n0, m0, s0 = state_mlstm_C[e], state_mlstm_n[e], state_mlstm_m[e], state_gla_S[e]
                CH = min(T, SCAN_CHUNK)
                ha_g, c1, n1, m1 = mlstm_mixer(proj_a, small, gate_bias, mlstm_norm[e], c0, n0, m0,
                                               B=B, T=T, row_off=r0, cols=a_cols,
                                               H=a_heads, dk=a_dk, dv=a_dv, L=CH,
                                               HB=math.gcd(a_heads, MLSTM_HEADS_PER_STEP))
                hb_g, s1 = gla_mixer(proj_b, small, w2pad, gla_b_gk[e], gla_norm[e], s0,
                                     B=B, T=T, row_off=r0, cols=b_cols,
                                     H=b_heads, dk=b_dk, dv=b_dv, CH=CH, SB=min(CH, SCAN_SUB),
                                     HB=math.gcd(b_heads, GLA_HEADS_PER_STEP))
                ha.append(ha_g); hb.append(hb_g)
                outs['c'][gi].append(c1); outs['n'][gi].append(n1)
                outs['m'][gi].append(m1); outs['g'][gi].append(s1)
            wo = w_out_even.astype(BF16)
            mix = matmul([(jnp.concatenate(ha, axis=0), wo, (e,), 0), (jnp.concatenate(hb, axis=0), wo, (e,), 1)],
                         out_dtype=BF16, tm=tm, tn=_col_tile(D, PROJ_COLS))
        else:
            o = layer // 2
            n_main = 4 * c_heads * c_dk + 4 * half
            wt = jnp.swapaxes(w_in_odd, 1, 2)
            wt_small = wt[o, n_main:]
            wt_small = jnp.pad(wt_small, ((0, LANES - wt_small.shape[0]), (0, 0)))
            proj = matmul_nt(h, wt, (o,), n=n_main, out_dtype=F32, tm=tm, tn=_col_tile(n_main, PROJ_COLS))
            small = matmul_nt(h, wt_small, (), n=LANES, out_dtype=F32, tm=tm, tn=LANES)
            cw = c_heads * c_dk
            dq = 4 * cw
            d_cols = (dq, dq + half, dq + 2 * half, dq + 3 * half)
            hc, hd = [], []
            for gi, (B, T, r0) in enumerate(groups):
                s0 = jnp.zeros((B,) + state_hgrn_S.shape[2:], F32) if gi == 0 else state_hgrn_S[o]
                CH = min(T, SCAN_CHUNK)
                hc_g, s1 = hgrn_mixer(proj, hgrn_lb_logits, hgrn_norm[o], s0,
                                      B=B, T=T, row_off=r0, cols=(0, cw, 2 * cw, 3 * cw),
                                      H=c_heads, dk=c_dk, dv=c_dv, CH=CH, SB=min(CH, SCAN_SUB),
                                      HB=math.gcd(c_heads, HGRN_HEADS_PER_STEP), layer=layer)
                df = small[r0:r0 + B * T, :d_heads]
                if gi == 0:
                    lf_t, c_t = fox_cumsum(df.reshape(B, T, d_heads).transpose(0, 2, 1), fox_b_f[o])
                    logf = lf_t.transpose(0, 2, 1)
                    hd_g = fox_prompt(proj, c_t.reshape(B, d_heads, 1, T), B=B, T=T, row_off=r0,
                                      cols=d_cols, H=d_heads, dh=d_dh, TQ=min(T, FOX_Q_BLOCK))
                else:
                    n_cb = proj.shape[1] // d_dh
                    proj3 = proj[r0:r0 + B * T].reshape(B * T, n_cb, d_dh)
                    n_odd = cache_fox_k.shape[0]
                    hd3, lf = fox_sample(
                        proj3, df.reshape(B, 1, T * d_heads), jnp.tile(fox_b_f[o], T)[None, :],
                        cache_fox_k.reshape(n_odd * n_phys, page * d_heads, d_dh),
                        cache_fox_v.reshape(n_odd * n_phys, page * d_heads, d_dh),
                        cache_fox_logf.reshape(n_odd * n_phys, 1, page * d_heads),
                        page_table, B=B, T=T, page_off=o * n_phys,
                        cols=tuple(c // d_dh for c in d_cols), H=d_heads, dh=d_dh, PP=FOX_PAGES_PER_STEP)
                    hd_g = hd3.reshape(B * T, half)
                    logf = lf.reshape(B, T, d_heads)
                hc.append(hc_g); hd.append(hd_g)
                outs['h'][gi].append(s1)
                k_new, v_new = split_heads(proj, rows=B * T, row_off=r0, ck=d_cols[1], cv=d_cols[2],
                                           H=d_heads, dh=d_dh, tr=min(B * T, SPLIT_ROWS))
                outs['k'][gi].append(k_new.reshape(B, T, d_heads, d_dh))
                outs['v'][gi].append(v_new.reshape(B, T, d_heads, d_dh))
                outs['f'][gi].append(logf)
            wo = w_out_odd.astype(BF16)
            mix = matmul([(jnp.concatenate(hc, axis=0), wo, (o,), 0), (jnp.concatenate(hd, axis=0), wo, (o,), 1)],
                         out_dtype=BF16, tm=tm, tn=_col_tile(D, PROJ_COLS))
        x, h = resid_norm(x, mix, norm_mix_post[layer], norm_ffn_pre[layer, 1], scale=1.0, tr=tr)
        y = ffn(h, layer, 1)
        if layer + 1 < depth:
            x, h = resid_norm(x, y, norm_ffn_post[layer, 1], norm_ffn_pre[layer + 1, 0], scale=0.5, tr=tr)
        else:
            y_p, y_s = resid_split(x, y, norm_ffn_post[layer, 1], scale=0.5, rows_a=mp, tr=math.gcd(mp, ms))

    names = ('c', 'n', 'm', 'g', 'h', 'k', 'v', 'f')
    res_p = tuple(jnp.stack(outs[k][0]) for k in names)
    res_s = tuple(jnp.stack(outs[k][1]) for k in names)
    return (y_p.reshape(bp, tp, D), y_s.reshape(bs, ts, D)) + res_p + res_s
```

```python
import functools
import math

import jax
import jax.numpy as jnp
from jax import lax
from jax.experimental import pallas as pl
from jax.experimental.pallas import tpu as pltpu

F32 = jnp.float32
BF16 = jnp.bfloat16
EPS = 1e-6
NEG = -1e30
LANES = 128
BF16_ROWS = 16
VMEM_LIMIT = 56 * 1024 * 1024

B_GATE_NORMALIZER = 16.0

MM_ROWS_CAP = 1408
NORM_ROWS_CAP = 256
FFN_COLS = 256
PROJ_COLS = 512
SCAN_CHUNK = 128
SCAN_SUB = 32
FOX_Q_BLOCK = 256
FOX_PAGES_PER_STEP = 8
MLSTM_HEADS_PER_STEP = 4
GLA_HEADS_PER_STEP = 4
HGRN_HEADS_PER_STEP = 8
SPLIT_ROWS = 256
CAST_ROWS = 128


def _cparams(sem):
    return pltpu.CompilerParams(dimension_semantics=sem, vmem_limit_bytes=VMEM_LIMIT)


def _log_sigmoid(x):
    return jnp.minimum(x, 0.0) - jnp.log1p(jnp.exp(-jnp.abs(x)))


def _sigmoid(x):
    return jax.nn.sigmoid(x)


def _nt(a, b):
    return lax.dot_general(a, b, (((1,), (1,)), ((), ())), preferred_element_type=F32)


def _tn(a, b):
    return lax.dot_general(a, b, (((0,), (0,)), ((), ())), preferred_element_type=F32)


def _nn(a, b):
    return jnp.dot(a, b, preferred_element_type=F32)


def _cumsum_rows(x):
    n = x.shape[0]
    row = lax.broadcasted_iota(jnp.int32, x.shape, 0)
    s = 1
    while s < n:
        x = x + jnp.where(row >= s, pltpu.roll(x, s, 0), 0.0)
        s *= 2
    return x


def _cumsum_lanes(x, step):
    n = x.shape[1]
    lane = lax.broadcasted_iota(jnp.int32, x.shape, 1)
    s = step
    while s < n:
        x = x + jnp.where(lane >= s, pltpu.roll(x, s, 1), 0.0)
        s *= 2
    return x


def _prenorm_join_kernel(xa_ref, xb_ref, g_ref, xo_ref, ho_ref, *, n_a, mb):
    i = pl.program_id(0)

    def emit(x, rows):
        xo_ref[rows, :] = x
        r = lax.rsqrt(jnp.mean(x * x, axis=-1, keepdims=True) + EPS)
        ho_ref[rows, :] = (x * r * g_ref[...]).astype(ho_ref.dtype)

    @pl.when(i < n_a)
    def _():
        emit(xa_ref[...], slice(None))

    @pl.when(i == n_a)
    def _():
        emit(xb_ref[...], slice(0, mb))


def prenorm_join(xa, xb, g, *, tr):
    (ma, d), mb = xa.shape, xb.shape[0]
    n_a = ma // tr
    assert n_a * tr == ma and mb <= tr
    row = pl.BlockSpec((tr, d), lambda i: (i, 0))
    return pl.pallas_call(
        functools.partial(_prenorm_join_kernel, n_a=n_a, mb=mb),
        grid=(n_a + 1,),
        in_specs=[pl.BlockSpec((tr, d), lambda i: (jnp.minimum(i, n_a - 1), 0)),
                  pl.BlockSpec((mb, d), lambda i: (0, 0)),
                  pl.BlockSpec((1, d), lambda i: (0, 0))],
        out_specs=[row, row],
        out_shape=[jax.ShapeDtypeStruct((ma + mb, d), F32), jax.ShapeDtypeStruct((ma + mb, d), BF16)],
        compiler_params=_cparams(("parallel",)),
        name="prenorm_join",
    )(xa, xb, g.reshape(1, d))


def _resid_kernel(x_ref, y_ref, gp_ref, gn_ref, xo_ref, ho_ref, *, scale):
    y = y_ref[...].astype(F32)
    r = lax.rsqrt(jnp.mean(y * y, axis=-1, keepdims=True) + EPS)
    xn = x_ref[...] + scale * (y * r * gp_ref[...])
    xo_ref[...] = xn
    r2 = lax.rsqrt(jnp.mean(xn * xn, axis=-1, keepdims=True) + EPS)
    ho_ref[...] = (xn * r2 * gn_ref[...]).astype(ho_ref.dtype)


def resid_norm(x, y, g_post, g_next, *, scale, tr):
    m, d = x.shape
    row = pl.BlockSpec((tr, d), lambda i: (i, 0))
    vec = pl.BlockSpec((1, d), lambda i: (0, 0))
    return pl.pallas_call(
        functools.partial(_resid_kernel, scale=scale),
        grid=(m // tr,),
        in_specs=[row, row, vec, vec],
        out_specs=[row, row],
        out_shape=[jax.ShapeDtypeStruct((m, d), F32), jax.ShapeDtypeStruct((m, d), BF16)],
        compiler_params=_cparams(("parallel",)),
        name="resid_norm",
    )(x, y, g_post.reshape(1, d), g_next.reshape(1, d))


def _resid_split_kernel(x_ref, y_ref, gp_ref, xa_ref, xb_ref, *, scale, n_a, mb):
    i = pl.program_id(0)

    def resid(x, y):
        y = y.astype(F32)
        r = lax.rsqrt(jnp.mean(y * y, axis=-1, keepdims=True) + EPS)
        return x + scale * (y * r * gp_ref[...])

    @pl.when(i < n_a)
    def _():
        xa_ref[...] = resid(x_ref[...], y_ref[...])

    @pl.when(i == n_a)
    def _():
        xb_ref[...] = resid(x_ref[0:mb, :], y_ref[0:mb, :])


def resid_split(x, y, g_post, *, scale, rows_a, tr):
    m, d = x.shape
    n_a = rows_a // tr
    mb = m - rows_a
    assert n_a * tr == rows_a and mb <= tr
    row = pl.BlockSpec((tr, d), lambda i: (i, 0))
    return pl.pallas_call(
        functools.partial(_resid_split_kernel, scale=scale, n_a=n_a, mb=mb),
        grid=(n_a + 1,),
        in_specs=[row, row, pl.BlockSpec((1, d), lambda i: (0, 0))],
        out_specs=[pl.BlockSpec((tr, d), lambda i: (jnp.minimum(i, n_a - 1), 0)),
                   pl.BlockSpec((mb, d), lambda i: (0, 0))],
        out_shape=[jax.ShapeDtypeStruct((rows_a, d), F32), jax.ShapeDtypeStruct((m - rows_a, d), F32)],
        compiler_params=_cparams(("arbitrary",)),
        name="resid_split",
    )(x, y, g_post.reshape(1, d))


def _split_heads_kernel(k_ref, v_ref, ko_ref, vo_ref, *, H, dh):
    for h in range(H):
        ko_ref[:, h, :] = k_ref[:, h * dh:(h + 1) * dh]
        vo_ref[:, h, :] = v_ref[:, h * dh:(h + 1) * dh]


def split_heads(proj, *, rows, row_off, ck, cv, H, dh, tr):
    w = H * dh
    out = jax.ShapeDtypeStruct((rows, H, dh), proj.dtype)
    ospec = pl.BlockSpec((tr, H, dh), lambda i: (i, 0, 0))
    return pl.pallas_call(
        functools.partial(_split_heads_kernel, H=H, dh=dh),
        grid=(rows // tr,),
        in_specs=[pl.BlockSpec((tr, w), lambda i: (row_off // tr + i, ck // w)),
                  pl.BlockSpec((tr, w), lambda i: (row_off // tr + i, cv // w))],
        out_specs=[ospec, ospec],
        out_shape=[out, out],
        compiler_params=_cparams(("parallel",)),
        name="split_heads",
    )(proj, proj)


def _mm_kernel(*refs, n_pairs, nk, n_ride):
    refs = list(refs)
    pair_refs = [refs.pop(0) for _ in range(2 * n_pairs)]
    ride_in = refs.pop(0) if n_ride else None
    o_ref = refs.pop(0)
    ride_out = refs.pop(0) if n_ride else None
    if n_ride:
        step = (pl.program_id(0) * pl.num_programs(1) + pl.program_id(1)) * nk + pl.program_id(2)

        @pl.when(step < n_ride)
        def _():
            ride_out[...] = ride_in[...].astype(ride_out.dtype)

    acc = _nn(pair_refs[0][...], pair_refs[1][...])
    for p in range(1, n_pairs):
        acc = acc + _nn(pair_refs[2 * p][...], pair_refs[2 * p + 1][...])
    if nk == 1:
        o_ref[...] = acc.astype(o_ref.dtype)
    else:
        acc_ref = refs.pop(0)
        k = pl.program_id(2)

        @pl.when(k == 0)
        def _():
            acc_ref[...] = acc

        @pl.when((k > 0) & (k < nk - 1))
        def _():
            acc_ref[...] += acc

        @pl.when(k == nk - 1)
        def _():
            o_ref[...] = (acc_ref[...] + acc).astype(o_ref.dtype)


def matmul(pairs, *, out_dtype, tm, tn, nk=1, ride=None):
    m = pairs[0][0].shape[0]
    n = pairs[0][1].shape[-1]
    grid = (m // tm, n // tn, nk)
    in_specs, args = [], []
    for x, w, lead, kb0 in pairs:
        tk = x.shape[1] // nk
        in_specs.append(pl.BlockSpec((tm, tk), lambda i, j, k: (i, k)))
        in_specs.append(pl.BlockSpec((None,) * len(lead) + (tk, tn),
                                     lambda i, j, k, lead=lead, kb0=kb0, nk=nk: lead + (kb0 * nk + k, j)))
        args += [x, w]
    out_specs = [pl.BlockSpec((tm, tn), lambda i, j, k: (i, j))]
    out_shape = [jax.ShapeDtypeStruct((m, n), out_dtype)]
    n_ride = 0
    if ride is not None:
        a, lead, rows = ride
        r, c = a.shape[-2:]
        n_ride = r // rows
        assert n_ride * rows == r and n_ride <= grid[0] * grid[1] * grid[2]

        def blk(i, j, k):
            return jnp.minimum((i * grid[1] + j) * nk + k, n_ride - 1)

        in_specs.append(pl.BlockSpec((None,) * len(lead) + (rows, c), lambda i, j, k: lead + (blk(i, j, k), 0)))
        args.append(a)
        out_specs.append(pl.BlockSpec((rows, c), lambda i, j, k: (blk(i, j, k), 0)))
        out_shape.append(jax.ShapeDtypeStruct((r, c), BF16))
    res = pl.pallas_call(
        functools.partial(_mm_kernel, n_pairs=len(pairs), nk=nk, n_ride=n_ride),
        grid=grid,
        in_specs=in_specs,
        out_specs=out_specs,
        out_shape=out_shape,
        scratch_shapes=[pltpu.VMEM((tm, tn), F32)] if nk > 1 else [],
        compiler_params=_cparams(("arbitrary", "arbitrary", "arbitrary") if n_ride else
                                 ("parallel", "parallel", "arbitrary")),
        name="matmul",
    )(*args)
    return res if n_ride else res[0]


def _mm_nt_kernel(x_ref, wt_ref, o_ref, w_bf):
    @pl.when(pl.program_id(1) == 0)
    def _():
        w_bf[...] = wt_ref[...].astype(BF16)

    o_ref[...] = _nt(x_ref[...], w_bf[...]).astype(o_ref.dtype)


def _snake(j, i, n_i):
    return jnp.where(j % 2 == 0, i, n_i - 1 - i)


def matmul_nt(x, wt, lead, *, n, out_dtype, tm, tn):
    m, k = x.shape
    n_i = m // tm
    return pl.pallas_call(
        _mm_nt_kernel,
        grid=(n // tn, n_i),
        in_specs=[pl.BlockSpec((tm, k), lambda j, i: (_snake(j, i, n_i), 0)),
                  pl.BlockSpec((None,) * len(lead) + (tn, k), lambda j, i: lead + (j, 0))],
        out_specs=pl.BlockSpec((tm, tn), lambda j, i: (_snake(j, i, n_i), j)),
        out_shape=jax.ShapeDtypeStruct((m, n), out_dtype),
        scratch_shapes=[pltpu.VMEM((tn, k), BF16)],
        compiler_params=_cparams(("parallel", "arbitrary")),
        name="matmul_nt",
    )(x, wt)


def _ffn_up_kernel(h_ref, wg_ref, wu_ref, o_ref, wg_bf, wu_bf):
    @pl.when(pl.program_id(1) == 0)
    def _():
        wg_bf[...] = wg_ref[...].astype(BF16)
        wu_bf[...] = wu_ref[...].astype(BF16)

    h = h_ref[...]
    g = _nn(h, wg_bf[...])
    u = _nn(h, wu_bf[...])
    o_ref[...] = (g * _sigmoid(g) * u).astype(o_ref.dtype)


def ffn_up(h, wg, wu, lead, *, tm, tn):
    m, k = h.shape
    n = wg.shape[-1]
    n_i = m // tm
    wspec = pl.BlockSpec((None,) * len(lead) + (k, tn), lambda j, i: lead + (0, j))
    return pl.pallas_call(
        _ffn_up_kernel,
        grid=(n // tn, n_i),
        in_specs=[pl.BlockSpec((tm, k), lambda j, i: (_snake(j, i, n_i), 0)), wspec, wspec],
        out_specs=pl.BlockSpec((tm, tn), lambda j, i: (_snake(j, i, n_i), j)),
        out_shape=jax.ShapeDtypeStruct((m, n), BF16),
        scratch_shapes=[pltpu.VMEM((k, tn), BF16), pltpu.VMEM((k, tn), BF16)],
        compiler_params=_cparams(("parallel", "arbitrary")),
        name="ffn_up",
    )(h, wg, wu)


def _mlstm_kernel(q_ref, k_ref, v_ref, og_ref, gt_ref, gb_ref, nrm_ref, c0_ref, n0_ref, m0_ref,
                  h_ref, c_ref, n_ref, m_ref, *, L, HB, dk, dv, n_heads):
    @pl.when(pl.program_id(2) == 0)
    def _():
        c_ref[...] = c0_ref[...]
        n_ref[...] = n0_ref[...]
        m_ref[...] = m0_ref[...]

    gates = gt_ref[...] + gb_ref[...]
    lane = lax.broadcasted_iota(jnp.int32, gates.shape, 1)
    r = lax.broadcasted_iota(jnp.int32, (L, L), 0)
    s = lax.broadcasted_iota(jnp.int32, (L, L), 1)
    tril = r >= s
    eye = r == s

    def col2row(col):
        return jnp.sum(jnp.where(eye, col, 0.0), axis=0, keepdims=True)

    for hb in range(HB):
        hh = pl.program_id(1) * HB + hb
        ks = slice(hb * dk, (hb + 1) * dk)
        vs = slice(hb * dv, (hb + 1) * dv)
        i_col = jnp.sum(jnp.where(lane == hh, gates, 0.0), axis=1, keepdims=True)
        f_col = jnp.sum(jnp.where(lane == hh + n_heads, gates, 0.0), axis=1, keepdims=True)
        logf = _log_sigmoid(f_col)
        logf_row = col2row(logf)
        b_col = jnp.sum(jnp.where(tril, logf_row, 0.0), axis=1, keepdims=True)
        b_row = col2row(b_col)
        i_row = col2row(i_col)
        m_prev = m_ref[hb]
        dmat = jnp.where(tril, b_col - b_row + i_row, -jnp.inf)
        from_state = b_col + m_prev
        m_t = jnp.maximum(from_state, jnp.max(dmat, axis=1, keepdims=True))

        q = q_ref[:, ks]
        k = k_ref[:, ks] * (dk ** -0.5)
        qb = q.astype(BF16)
        vb = v_ref[:, vs].astype(BF16)
        w = jnp.exp(dmat - m_t) * _nt(qb, k.astype(BF16))
        s_state = jnp.exp(from_state - m_t)
        c_old = c_ref[hb]
        n_old = n_ref[hb]
        num = s_state * _nn(qb, c_old.astype(BF16)) + _nn(w.astype(BF16), vb)
        den = s_state * jnp.sum(q * n_old, axis=1, keepdims=True) + jnp.sum(w, axis=1, keepdims=True)
        h = num / jnp.maximum(jnp.abs(den), jnp.exp(-m_t))

        m_new = m_t[L - 1:L, :]
        b_last = b_col[L - 1:L, :]
        w_end = jnp.exp(b_last - b_col + i_col - m_new)
        s_end = jnp.exp(b_last + m_prev - m_new)
        kw = k * w_end
        c_ref[hb] = s_end * c_old + _tn(kw.astype(BF16), vb)
        n_ref[hb] = s_end * n_old + jnp.sum(kw, axis=0, keepdims=True)
        m_ref[hb] = m_new

        hn = h * lax.rsqrt(jnp.mean(h * h, axis=1, keepdims=True) + EPS) * nrm_ref[hb]
        h_ref[:, vs] = (hn * _sigmoid(og_ref[:, vs])).astype(h_ref.dtype)


def mlstm_mixer(proj, small, gate_bias, norm, c0, n0, m0, *, B, T, row_off, cols, H, dk, dv, L, HB):
    nC = T // L
    rb = row_off // L
    cq, ck, cv, co = cols
    wk, wv = HB * dk, HB * dv

    def rows(b, h, c):
        return rb + b * nC + c

    in_specs = [
        pl.BlockSpec((L, wk), lambda b, h, c: (rows(b, h, c), cq // wk + h)),
        pl.BlockSpec((L, wk), lambda b, h, c: (rows(b, h, c), ck // wk + h)),
        pl.BlockSpec((L, wv), lambda b, h, c: (rows(b, h, c), cv // wv + h)),
        pl.BlockSpec((L, wv), lambda b, h, c: (rows(b, h, c), co // wv + h)),
        pl.BlockSpec((L, LANES), lambda b, h, c: (rows(b, h, c), 0)),
        pl.BlockSpec((1, LANES), lambda b, h, c: (0, 0)),
        pl.BlockSpec((HB, 1, dv), lambda b, h, c: (h, 0, 0)),
        pl.BlockSpec((None, HB, dk, dv), lambda b, h, c: (b, h, 0, 0)),
        pl.BlockSpec((None, HB, 1, dk), lambda b, h, c: (b, h, 0, 0)),
        pl.BlockSpec((None, HB, 1, 1), lambda b, h, c: (b, h, 0, 0)),
    ]
    out_specs = [
        pl.BlockSpec((L, wv), lambda b, h, c: (b * nC + c, h)),
        pl.BlockSpec((None, HB, dk, dv), lambda b, h, c: (b, h, 0, 0)),
        pl.BlockSpec((None, HB, 1, dk), lambda b, h, c: (b, h, 0, 0)),
        pl.BlockSpec((None, HB, 1, 1), lambda b, h, c: (b, h, 0, 0)),
    ]
    out_shape = [
        jax.ShapeDtypeStruct((B * T, H * dv), BF16),
        jax.ShapeDtypeStruct((B, H, dk, dv), F32),
        jax.ShapeDtypeStruct((B, H, 1, dk), F32),
        jax.ShapeDtypeStruct((B, H, 1, 1), F32),
    ]
    h, c1, n1, m1 = pl.pallas_call(
        functools.partial(_mlstm_kernel, L=L, HB=HB, dk=dk, dv=dv, n_heads=H),
        grid=(B, H // HB, nC),
        in_specs=in_specs, out_specs=out_specs, out_shape=out_shape,
        compiler_params=_cparams(("parallel", "parallel", "arbitrary")),
        name="mlstm",
    )(proj, proj, proj, proj, small, gate_bias, norm.reshape(H, 1, dv), c0,
      n0.reshape(B, H, 1, dk), m0.reshape(B, H, 1, 1))
    return h, c1, n1.reshape(B, H, dk), m1.reshape(B, H)


def _scan_chunk(q, k, v, g, st, *, SB):
    CH = q.shape[0]
    bc = _cumsum_rows(g)
    b_last = bc[CH - 1:CH, :]
    vb = v.astype(BF16)
    o = _nt((q * jnp.exp(bc)).astype(BF16), st.astype(BF16))
    row = lax.broadcasted_iota(jnp.int32, bc.shape, 0)
    srow = lax.broadcasted_iota(jnp.int32, (SB, CH), 0)
    scol = lax.broadcasted_iota(jnp.int32, (SB, CH), 1)
    blocks = []
    for i in range(CH // SB):
        lo, hi = i * SB, (i + 1) * SB
        ref = bc[lo - 1:lo, :] if i > 0 else jnp.zeros_like(b_last)
        qi = q[lo:hi] * jnp.exp(bc[lo:hi] - ref)
        ki = k * jnp.exp(jnp.where(row < hi, ref - bc, 0.0))
        a = _nt(qi.astype(BF16), ki.astype(BF16))
        blocks.append(jnp.where(scol <= srow + lo, a, 0.0))
    a_full = blocks[0] if len(blocks) == 1 else jnp.concatenate(blocks, axis=0)
    o = o + _nn(a_full.astype(BF16), vb)
    k_dec = k * jnp.exp(b_last - bc)
    st_new = st * jnp.exp(b_last) + _tn(vb, k_dec.astype(BF16))
    return o, st_new


def _head_out(o, nrm, gate_act):
    return o * lax.rsqrt(jnp.mean(o * o, axis=1, keepdims=True) + EPS) * nrm * gate_act


def _gla_kernel(q_ref, k_ref, v_ref, og_ref, gt_ref, w2_ref, bgk_ref, nrm_ref, s0_ref,
                o_ref, s_ref, st_scr, *, SB, HB, dk, dv, n_chunks):
    c = pl.program_id(2)

    @pl.when(c == 0)
    def _():
        for hb in range(HB):
            st_scr[hb] = s0_ref[hb].T

    gates = gt_ref[...].astype(BF16)
    for hb in range(HB):
        ks = slice(hb * dk, (hb + 1) * dk)
        vs = slice(hb * dv, (hb + 1) * dv)
        gk = _nn(gates, w2_ref[hb]) + bgk_ref[hb]
        g = _log_sigmoid(gk) / B_GATE_NORMALIZER
        o, st_new = _scan_chunk(q_ref[:, ks] * (dk ** -0.5), k_ref[:, ks], v_ref[:, vs], g, st_scr[hb], SB=SB)
        st_scr[hb] = st_new
        og = og_ref[:, vs]
        o_ref[:, vs] = _head_out(o, nrm_ref[hb], og * _sigmoid(og)).astype(o_ref.dtype)

    @pl.when(c == n_chunks - 1)
    def _():
        for hb in range(HB):
            s_ref[hb] = st_scr[hb].T


def gla_mixer(proj, small, w2pad, b_gk, norm, s0, *, B, T, row_off, cols, H, dk, dv, CH, SB, HB):
    nC = T // CH
    rb = row_off // CH
    cq, ck, cv, co = cols
    wk, wv = HB * dk, HB * dv

    def rows(b, h, c):
        return rb + b * nC + c

    in_specs = [
        pl.BlockSpec((CH, wk), lambda b, h, c: (rows(b, h, c), cq // wk + h)),
        pl.BlockSpec((CH, wk), lambda b, h, c: (rows(b, h, c), ck // wk + h)),
        pl.BlockSpec((CH, wv), lambda b, h, c: (rows(b, h, c), cv // wv + h)),
        pl.BlockSpec((CH, wv), lambda b, h, c: (rows(b, h, c), co // wv + h)),
        pl.BlockSpec((CH, LANES), lambda b, h, c: (rows(b, h, c), 0)),
        pl.BlockSpec((HB, LANES, dk), lambda b, h, c: (h, 0, 0)),
        pl.BlockSpec((HB, 1, dk), lambda b, h, c: (h, 0, 0)),
        pl.BlockSpec((HB, 1, dv), lambda b, h, c: (h, 0, 0)),
        pl.BlockSpec((None, HB, dk, dv), lambda b, h, c: (b, h, 0, 0)),
    ]
    out_specs = [
        pl.BlockSpec((CH, wv), lambda b, h, c: (b * nC + c, h)),
        pl.BlockSpec((None, HB, dk, dv), lambda b, h, c: (b, h, 0, 0)),
    ]
    out_shape = [jax.ShapeDtypeStruct((B * T, H * dv), BF16), jax.ShapeDtypeStruct((B, H, dk, dv), F32)]
    return pl.pallas_call(
        functools.partial(_gla_kernel, SB=SB, HB=HB, dk=dk, dv=dv, n_chunks=nC),
        grid=(B, H // HB, nC),
        in_specs=in_specs, out_specs=out_specs, out_shape=out_shape,
        scratch_shapes=[pltpu.VMEM((HB, dv, dk), F32)],
        compiler_params=_cparams(("parallel", "parallel", "arbitrary")),
        name="gla",
    )(proj, proj, proj, proj, small, w2pad, b_gk.reshape(H, 1, dk), norm.reshape(H, 1, dv), s0)


def _hgrn_kernel(q_ref, f_ref, i_ref, og_ref, lg_ref, nrm_ref, s0_ref, o_ref, s_ref, st_scr,
                 *, SB, HB, dk, layer, n_chunks):
    c = pl.program_id(2)
    lg = lg_ref[...]
    e = jnp.exp(lg - jnp.max(lg, axis=0, keepdims=True))
    sm = e / jnp.sum(e, axis=0, keepdims=True)
    lb_all = jnp.sum(sm[1:layer + 1], axis=0, keepdims=True) if layer > 0 else jnp.zeros_like(sm[0:1])
    @pl.when(c == 0)
    def _():
        for hb in range(HB):
            st_scr[hb] = s0_ref[hb].T

    for hb in range(HB):
        sl = slice(hb * dk, (hb + 1) * dk)
        lb = lb_all[:, sl]
        z = f_ref[:, sl]
        g = jnp.log(lb + (1.0 - lb) * _sigmoid(z))
        key = (1.0 - lb) * _sigmoid(-z)
        cq = q_ref[:, sl]
        o, st_new = _scan_chunk(cq * _sigmoid(cq), key, i_ref[:, sl], g, st_scr[hb], SB=SB)
        st_scr[hb] = st_new
        og = og_ref[:, sl]
        o_ref[:, sl] = _head_out(o, nrm_ref[hb], og * _sigmoid(og)).astype(o_ref.dtype)

    @pl.when(c == n_chunks - 1)
    def _():
        for hb in range(HB):
            s_ref[hb] = st_scr[hb].T


def hgrn_mixer(proj, lb_logits, norm, s0, *, B, T, row_off, cols, H, dk, dv, CH, SB, HB, layer):
    assert dk == dv
    nC = T // CH
    rb = row_off // CH
    cq, cf, ci, co = cols
    depth = lb_logits.shape[0]
    wb = HB * dk

    def col(c0):
        return pl.BlockSpec((CH, wb), lambda b, h, c: (rb + b * nC + c, c0 // wb + h))

    in_specs = [
        col(cq), col(cf), col(ci), col(co),
        pl.BlockSpec((depth, wb), lambda b, h, c: (0, h)),
        pl.BlockSpec((HB, 1, dv), lambda b, h, c: (h, 0, 0)),
        pl.BlockSpec((None, HB, dk, dv), lambda b, h, c: (b, h, 0, 0)),
    ]
    out_specs = [
        pl.BlockSpec((CH, wb), lambda b, h, c: (b * nC + c, h)),
        pl.BlockSpec((None, HB, dk, dv), lambda b, h, c: (b, h, 0, 0)),
    ]
    out_shape = [jax.ShapeDtypeStruct((B * T, H * dv), BF16), jax.ShapeDtypeStruct((B, H, dk, dv), F32)]
    return pl.pallas_call(
        functools.partial(_hgrn_kernel, SB=SB, HB=HB, dk=dk, layer=layer, n_chunks=nC),
        grid=(B, H // HB, nC),
        in_specs=in_specs, out_specs=out_specs, out_shape=out_shape,
        scratch_shapes=[pltpu.VMEM((HB, dv, dk), F32)],
        compiler_params=_cparams(("parallel", "parallel", "arbitrary")),
        name="hgrn",
    )(proj, proj, proj, proj, lb_logits, norm.reshape(H, 1, dv), s0)


def _fox_cumsum_kernel(x_ref, b_ref, lf_ref, c_ref):
    lf = _log_sigmoid(x_ref[...] + b_ref[...])
    lf_ref[...] = lf
    c_ref[...] = _cumsum_lanes(lf, 1)


def fox_cumsum(df_t, bias):
    B, H, T = df_t.shape
    spec = pl.BlockSpec((None, H, T), lambda b: (b, 0, 0))
    return pl.pallas_call(
        _fox_cumsum_kernel,
        grid=(B,),
        in_specs=[spec, pl.BlockSpec((H, 1), lambda b: (0, 0))],
        out_specs=[spec, spec],
        out_shape=[jax.ShapeDtypeStruct((B, H, T), F32)] * 2,
        compiler_params=_cparams(("parallel",)),
        name="fox_cumsum",
    )(df_t, bias.reshape(H, 1))


def _fox_prompt_kernel(q_ref, k_ref, v_ref, og_ref, cr_ref, o_ref, *, TQ, T, dh):
    kb = k_ref[...].astype(BF16)
    vb = v_ref[...].astype(BF16)
    c_row = cr_ref[...]
    for i in range(T // TQ):
        lo, hi = i * TQ, (i + 1) * TQ
        q = (q_ref[lo:hi, :] * (dh ** -0.5)).astype(BF16)
        s = _nt(q, kb[:hi]) - c_row[:, :hi]
        qpos = lo + lax.broadcasted_iota(jnp.int32, (TQ, hi), 0)
        kpos = lax.broadcasted_iota(jnp.int32, (TQ, hi), 1)
        s = jnp.where(kpos <= qpos, s, NEG)
        p = jnp.exp(s - jnp.max(s, axis=1, keepdims=True))
        l = jnp.sum(p, axis=1, keepdims=True)
        o = _nn(p.astype(BF16), vb[:hi]) / l
        o_ref[lo:hi, :] = (o * _sigmoid(og_ref[lo:hi, :])).astype(o_ref.dtype)


def fox_prompt(proj, c_row, *, B, T, row_off, cols, H, dh, TQ):
    cq, ck, cv, co = cols
    rb = row_off // T

    def col(c0):
        return pl.BlockSpec((T, dh), lambda b, h: (rb + b, c0 // dh + h))

    in_specs = [col(cq), col(ck), col(cv), col(co),
                pl.BlockSpec((None, None, 1, T), lambda b, h: (b, h, 0, 0))]
    return pl.pallas_call(
        functools.partial(_fox_prompt_kernel, TQ=TQ, T=T, dh=dh),
        grid=(B, H),
        in_specs=in_specs,
        out_specs=pl.BlockSpec((T, dh), lambda b, h: (b, h)),
        out_shape=jax.ShapeDtypeStruct((B * T, H * dh), BF16),
        compiler_params=_cparams(("parallel", "parallel")),
        name="fox_prompt",
    )(proj, proj, proj, proj, c_row)


def _fox_sample_kernel(pt_ref, q_ref, kn_ref, vn_ref, og_ref, df_ref, fb_ref, *rest, PP, T, H, dh, n_steps):
    k_refs = rest[0:PP]
    v_refs = rest[PP:2 * PP]
    f_refs = rest[2 * PP:3 * PP]
    o_ref, lf_ref = rest[3 * PP], rest[3 * PP + 1]
    qall, acc, m_scr, l_scr, carry, crow, hmask = rest[3 * PP + 2:]
    s_id = pl.program_id(1)
    R = T * H
    W = k_refs[0].shape[0]

    @pl.when(s_id == 0)
    def _():
        qall[...] = (q_ref[...].reshape(R, dh) * (dh ** -0.5)).astype(BF16)
        acc[...] = jnp.zeros_like(acc)
        m_scr[...] = jnp.full_like(m_scr, NEG)
        l_scr[...] = jnp.zeros_like(l_scr)
        carry[...] = jnp.zeros_like(carry)
        lf_new = _log_sigmoid(df_ref[...] + fb_ref[...])
        lf_ref[...] = lf_new
        crow[...] = _cumsum_lanes(lf_new, H)
        rr = lax.broadcasted_iota(jnp.int32, (R, W), 0)
        ll = lax.broadcasted_iota(jnp.int32, (R, W), 1)
        hmask[...] = jnp.where((rr % H) == (ll % H), 0.0, NEG)

    lf = jnp.concatenate([f_refs[p][...] for p in range(PP)], axis=0)
    lane = lax.broadcasted_iota(jnp.int32, (PP, W), 1)
    suf = lf
    tot = lf
    s = H
    while s < W:
        suf = suf + jnp.where(lane + s < W, pltpu.roll(suf, W - s, 1), 0.0)
        tot = tot + pltpu.roll(tot, s, 1)
        s *= 2
    tot_cum = _cumsum_rows(tot)
    later = carry[...]
    d_all = suf - lf + (tot_cum - tot) + later
    carry[...] = later + tot_cum[PP - 1:PP, :]

    qa = qall[...]
    hm = hmask[...]
    m_pg, l_pg, a_pg = [], [], []
    for p in range(PP):
        kb = k_refs[p][...].astype(BF16)
        vb = v_refs[p][...].astype(BF16)
        st = _nt(qa, kb) + (hm + d_all[p:p + 1, :])
        m_p = jnp.max(st, axis=1, keepdims=True)
        pm = jnp.exp(st - m_p)
        m_pg.append(m_p)
        l_pg.append(jnp.sum(pm, axis=1, keepdims=True))
        a_pg.append(_nn(pm.astype(BF16), vb))
    m_prev = m_scr[...]
    m_old = m_prev
    for m_p in m_pg:
        m_old = jnp.maximum(m_old, m_p)
    alpha = jnp.exp(m_prev - m_old)
    l_old = alpha * l_scr[...]
    a_old = alpha * acc[...]
    for m_p, l_p, a_p in zip(m_pg, l_pg, a_pg):
        w_p = jnp.exp(m_p - m_old)
        l_old = l_old + w_p * l_p
        a_old = a_old + w_p * a_p
    m_scr[...] = m_old
    l_scr[...] = l_old
    acc[...] = a_old

    @pl.when(s_id == n_steps - 1)
    def _():
        knb = kn_ref[...].reshape(R, dh).astype(BF16)
        vnb = vn_ref[...].reshape(R, dh).astype(BF16)
        r2 = lax.broadcasted_iota(jnp.int32, (R, R), 0)
        l2 = lax.broadcasted_iota(jnp.int32, (R, R), 1)
        ok = ((r2 % H) == (l2 % H)) & (l2 // H <= r2 // H)
        sn = jnp.where(ok, _nt(qa, knb) - crow[...], NEG)
        m_fin = jnp.maximum(m_old, jnp.max(sn, axis=1, keepdims=True))
        a2 = jnp.exp(m_old - m_fin)
        pn = jnp.exp(sn - m_fin)
        l_fin = a2 * l_old + jnp.sum(pn, axis=1, keepdims=True)
        out = (a2 * a_old + _nn(pn.astype(BF16), vnb)) / l_fin
        out = out * _sigmoid(og_ref[...].reshape(R, dh))
        o_ref[...] = out.reshape(T, H, dh).astype(o_ref.dtype)


def fox_sample(proj3, df_flat, fb_flat, cache_k, cache_v, cache_f, page_table, *, B, T, page_off, cols, H, dh, PP):
    R = T * H
    W = cache_k.shape[1]
    n_pages = page_table.shape[1]
    n_steps = n_pages // PP
    cq, ck, cv, co = cols

    def rowspec(c0):
        return pl.BlockSpec((T, H, dh), lambda b, s, pt: (b, c0 // H, 0))

    def page(p):
        return lambda b, s, pt: (page_off + pt[b, n_pages - 1 - (s * PP + p)], 0, 0)

    in_specs = [rowspec(cq), rowspec(ck), rowspec(cv), rowspec(co),
                pl.BlockSpec((None, 1, R), lambda b, s, pt: (b, 0, 0)),
                pl.BlockSpec((1, R), lambda b, s, pt: (0, 0))]
    in_specs += [pl.BlockSpec((None, W, dh), page(p)) for p in range(PP)]
    in_specs += [pl.BlockSpec((None, W, dh), page(p)) for p in range(PP)]
    in_specs += [pl.BlockSpec((None, 1, W), page(p)) for p in range(PP)]
    grid_spec = pltpu.PrefetchScalarGridSpec(
        num_scalar_prefetch=1,
        grid=(B, n_steps),
        in_specs=in_specs,
        out_specs=[pl.BlockSpec((T, H, dh), lambda b, s, pt: (b, 0, 0)),
                   pl.BlockSpec((None, 1, R), lambda b, s, pt: (b, 0, 0))],
        scratch_shapes=[
            pltpu.VMEM((R, dh), BF16),
            pltpu.VMEM((R, dh), F32),
            pltpu.VMEM((R, 1), F32),
            pltpu.VMEM((R, 1), F32),
            pltpu.VMEM((1, W), F32),
            pltpu.VMEM((1, R), F32),
            pltpu.VMEM((R, W), F32),
        ],
    )
    return pl.pallas_call(
        functools.partial(_fox_sample_kernel, PP=PP, T=T, H=H, dh=dh, n_steps=n_steps),
        grid_spec=grid_spec,
        out_shape=[jax.ShapeDtypeStruct((B * T, H, dh), BF16), jax.ShapeDtypeStruct((B, 1, R), F32)],
        compiler_params=_cparams(("parallel", "arbitrary")),
        name="fox_sample",
    )(page_table, proj3, proj3, proj3, proj3, df_flat, fb_flat,
      *([cache_k] * PP), *([cache_v] * PP), *([cache_f] * PP))


def _pad_lanes(w):
    return jnp.pad(w, ((0, 0), (0, LANES - w.shape[1])))


def _row_tile(m, cap):
    best = None
    for t in range(BF16_ROWS, min(m, cap) + 1, BF16_ROWS):
        if m % t == 0:
            best = t
    assert best is not None, (m, cap)
    return best


def _col_tile(n, cap):
    t = cap
    while n % t:
        t //= 2
    return t


def kernel(x_prompt, x_sample, state_mlstm_C, state_mlstm_n, state_mlstm_m, state_gla_S, state_hgrn_S, cache_fox_k, cache_fox_v, cache_fox_logf, page_table, norm_mix_pre, norm_mix_post, norm_ffn_pre, norm_ffn_post, ffn_w_gate, ffn_w_up, ffn_w_down, w_in_even, w_out_even, mlstm_b_i, mlstm_b_f, mlstm_norm, gla_w_gk2, gla_b_gk, gla_norm, w_in_odd, w_out_odd, hgrn_lb_logits, hgrn_norm, fox_b_f):
    bp, tp, D = x_prompt.shape
    bs, ts, _ = x_sample.shape
    mp, ms = bp * tp, bs * ts
    M = mp + ms
    depth = norm_mix_pre.shape[0]
    a_heads, a_dk, a_dv = state_mlstm_C.shape[2:]
    b_heads, b_dk, b_dv = state_gla_S.shape[2:]
    c_heads, c_dk, c_dv = state_hgrn_S.shape[2:]
    d_heads = fox_b_f.shape[1]
    half = D // 2
    d_dh = half // d_heads
    ff = ffn_w_gate.shape[3]
    n_phys, page = cache_fox_k.shape[1:3]

    tm = _row_tile(M, MM_ROWS_CAP)
    tr = _row_tile(M, NORM_ROWS_CAP)
    wd_bf = [ffn_w_down[0, 0].astype(BF16)]

    def ffn(h, layer, j):
        u = ffn_up(h, ffn_w_gate, ffn_w_up, (layer, j), tm=tm, tn=_col_tile(ff, FFN_COLS))
        nxt = (layer, 1) if j == 0 else (layer + 1, 0)
        ride = (ffn_w_down, nxt, CAST_ROWS) if nxt[0] < depth else None
        res = matmul([(u, wd_bf[-1], (), 0)], out_dtype=BF16, tm=tm, tn=_col_tile(D, PROJ_COLS), nk=2, ride=ride)
        if ride is None:
            return res
        wd_bf.append(res[1])
        return res[0]

    groups = ((bp, tp, 0), (bs, ts, mp))
    outs = {k: ([], []) for k in ('c', 'n', 'm', 'g', 'h', 'k', 'v', 'f')}

    x, h = prenorm_join(x_prompt.reshape(mp, D), x_sample.reshape(ms, D), norm_ffn_pre[0, 0], tr=NORM_ROWS_CAP)
    for layer in range(depth):
        y = ffn(h, layer, 0)
        x, h = resid_norm(x, y, norm_ffn_post[layer, 0], norm_mix_pre[layer], scale=0.5, tr=tr)
        if layer % 2 == 0:
            e = layer // 2
            wt = jnp.swapaxes(w_in_even, 1, 2)
            n_a = a_heads * (2 * a_dk + 2 * a_dv)
            n_b = b_heads * (2 * b_dk + 2 * b_dv)
            o_b = n_a + 2 * a_heads
            wt_small = jnp.concatenate([wt[e, n_a:o_b], wt[e, o_b + n_b:]], axis=0)
            wt_small = jnp.pad(wt_small, ((0, LANES - wt_small.shape[0]), (0, 0)))
            proj_a = matmul_nt(h, wt, (e,), n=n_a, out_dtype=F32, tm=tm, tn=_col_tile(n_a, PROJ_COLS))
            proj_b = matmul_nt(h, wt[e, o_b:o_b + n_b], (), n=n_b, out_dtype=F32, tm=tm, tn=_col_tile(n_b, PROJ_COLS))
            small = matmul_nt(h, wt_small, (), n=LANES, out_dtype=F32, tm=tm, tn=LANES)
            gate_bias = _pad_lanes(jnp.concatenate([mlstm_b_i[e], mlstm_b_f[e]])[None, :])
            a_cols = (0, a_heads * a_dk, 2 * a_heads * a_dk, 2 * a_heads * a_dk + a_heads * a_dv)
            w2 = gla_w_gk2[e]
            w2pad = jnp.zeros((LANES, b_heads * b_dk), F32).at[2 * a_heads:2 * a_heads + w2.shape[0]].set(w2)
            w2pad = w2pad.reshape(LANES, b_heads, b_dk).transpose(1, 0, 2).astype(BF16)
            b_cols = (0, b_heads * b_dk, 2 * b_heads * b_dk, 2 * b_heads * b_dk + b_heads * b_dv)
            ha, hb = [], []
            for gi, (B, T, r0) in enumerate(groups):
                if gi == 0:
                    c0, n0, m0, s0 = (jnp.zeros((B,) + s.shape[2:], F32)
                                      for s in (state_mlstm_C, state_mlstm_n, state_mlstm_m, state_gla_S))
                else:
                    c0, n0, m0, s0 = state_mlstm_C[e], state_mlstm_n[e], state_mlstm_m[e], state_gla_S[e]
                CH = min(T, SCAN_CHUNK)
                ha_g, c1, n1, m1 = mlstm_mixer(proj_a, small, gate_bias, mlstm_norm[e], c0, n0, m0,
                                               B=B, T=T, row_off=r0, cols=a_cols,
                                               H=a_heads, dk=a_dk, dv=a_dv, L=CH,
                                               HB=math.gcd(a_heads, MLSTM_HEADS_PER_STEP))
                hb_g, s1 = gla_mixer(proj_b, small, w2pad, gla_b_gk[e], gla_norm[e], s0,
                                     B=B, T=T, row_off=r0, cols=b_cols,
                                     H=b_heads, dk=b_dk, dv=b_dv, CH=CH, SB=min(CH, SCAN_SUB),
                                     HB=math.gcd(b_heads, GLA_HEADS_PER_STEP))
                ha.append(ha_g); hb.append(hb_g)
                outs['c'][gi].append(c1); outs['n'][gi].append(n1)
                outs['m'][gi].append(m1); outs['g'][gi].append(s1)
            wo = w_out_even.astype(BF16)
            mix = matmul([(jnp.concatenate(ha, axis=0), wo, (e,), 0), (jnp.concatenate(hb, axis=0), wo, (e,), 1)],
                         out_dtype=BF16, tm=tm, tn=_col_tile(D, PROJ_COLS))
        else:
            o = layer // 2
            n_main = 4 * c_heads * c_dk + 4 * half
            wt = jnp.swapaxes(w_in_odd, 1, 2)
            wt_small = wt[o, n_main:]
            wt_small = jnp.pad(wt_small, ((0, LANES - wt_small.shape[0]), (0, 0)))
            proj = matmul_nt(h, wt, (o,), n=n_main, out_dtype=F32, tm=tm, tn=_col_tile(n_main, PROJ_COLS))
            small = matmul_nt(h, wt_small, (), n=LANES, out_dtype=F32, tm=tm, tn=LANES)
            cw = c_heads * c_dk
            dq = 4 * cw
            d_cols = (dq, dq + half, dq + 2 * half, dq + 3 * half)
            hc, hd = [], []
            for gi, (B, T, r0) in enumerate(groups):
                s0 = jnp.zeros((B,) + state_hgrn_S.shape[2:], F32) if gi == 0 else state_hgrn_S[o]
                CH = min(T, SCAN_CHUNK)
                hc_g, s1 = hgrn_mixer(proj, hgrn_lb_logits, hgrn_norm[o], s0,
                                      B=B, T=T, row_off=r0, cols=(0, cw, 2 * cw, 3 * cw),
                                      H=c_heads, dk=c_dk, dv=c_dv, CH=CH, SB=min(CH, SCAN_SUB),
                                      HB=math.gcd(c_heads, HGRN_HEADS_PER_STEP), layer=layer)
                df = small[r0:r0 + B * T, :d_heads]
                if gi == 0:
                    lf_t, c_t = fox_cumsum(df.reshape(B, T, d_heads).transpose(0, 2, 1), fox_b_f[o])
                    logf = lf_t.transpose(0, 2, 1)
                    hd_g = fox_prompt(proj, c_t.reshape(B, d_heads, 1, T), B=B, T=T, row_off=r0,
                                      cols=d_cols, H=d_heads, dh=d_dh, TQ=min(T, FOX_Q_BLOCK))
                else:
                    n_cb = proj.shape[1] // d_dh
                    proj3 = proj[r0:r0 + B * T].reshape(B * T, n_cb, d_dh)
                    n_odd = cache_fox_k.shape[0]
                    hd3, lf = fox_sample(
                        proj3, df.reshape(B, 1, T * d_heads), jnp.tile(fox_b_f[o], T)[None, :],
                        cache_fox_k.reshape(n_odd * n_phys, page * d_heads, d_dh),
                        cache_fox_v.reshape(n_odd * n_phys, page * d_heads, d_dh),
                        cache_fox_logf.reshape(n_odd * n_phys, 1, page * d_heads),
                        page_table, B=B, T=T, page_off=o * n_phys,
                        cols=tuple(c // d_dh for c in d_cols), H=d_heads, dh=d_dh, PP=FOX_PAGES_PER_STEP)
                    hd_g = hd3.reshape(B * T, half)
                    logf = lf.reshape(B, T, d_heads)
                hc.append(hc_g); hd.append(hd_g)
                outs['h'][gi].append(s1)
                k_new, v_new = split_heads(proj, rows=B * T, row_off=r0, ck=d_cols[1], cv=d_cols[2],
                                           H=d_heads, dh=d_dh, tr=min(B * T, SPLIT_ROWS))
                outs['k'][gi].append(k_new.reshape(B, T, d_heads, d_dh))
                outs['v'][gi].append(v_new.reshape(B, T, d_heads, d_dh))
                outs['f'][gi].append(logf)
            wo = w_out_odd.astype(BF16)
            mix = matmul([(jnp.concatenate(hc, axis=0), wo, (o,), 0), (jnp.concatenate(hd, axis=0), wo, (o,), 1)],
                         out_dtype=BF16, tm=tm, tn=_col_tile(D, PROJ_COLS))
        x, h = resid_norm(x, mix, norm_mix_post[layer], norm_ffn_pre[layer, 1], scale=1.0, tr=tr)
        y = ffn(h, layer, 1)
        if layer + 1 < depth:
            x, h = resid_norm(x, y, norm_ffn_post[layer, 1], norm_ffn_pre[layer + 1, 0], scale=0.5, tr=tr)
        else:
            y_p, y_s = resid_split(x, y, norm_ffn_post[layer, 1], scale=0.5, rows_a=mp, tr=NORM_ROWS_CAP)

    names = ('c', 'n', 'm', 'g', 'h', 'k', 'v', 'f')
    res_p = tuple(jnp.stack(outs[k][0]) for k in names)
    res_s = tuple(jnp.stack(outs[k][1]) for k in names)
    return (y_p.reshape(bp, tp, D), y_s.reshape(bs, ts, D)) + res_p + res_s
```
